```python
import math
import jax, jax.numpy as jnp
from jax import lax
import numpy as np

D_MODEL = 2048
BATCH = 2
SEQ = 16384
DEPTH = 4
DEC_BATCH = 4
DEC_SEQ = 2048
PAST_LEN = 128

HEAD_DIM = 128
GROUP_HEADS = 4
GROUP_W = GROUP_HEADS * HEAD_DIM
N_GROUPS = 4
D_MIX = N_GROUPS * GROUP_W
F_W = GROUP_W
CONV_W = GROUP_W
CONV_K = 3
MLSTM_HEADS = 4
MLSTM_W = MLSTM_HEADS * HEAD_DIM
MLSTM_CHUNK = 128
ATTN_Q_HEADS = 4
ATTN_KV_HEADS = 2
ATTN_W = ATTN_Q_HEADS * HEAD_DIM
Q_BLOCK = 128
GRID_W = 64
ROPE_THETA = 10000.0
ROPE_FREQS = HEAD_DIM // 4
CA_HEADS = 4
CA_W = CA_HEADS * HEAD_DIM
N_MEM = 256
D_FF = 4 * D_MODEL
RMS_EPS = 1e-6
IN_SIZES = (F_W, CONV_W, CONV_W, CONV_W, MLSTM_W, MLSTM_W, MLSTM_W, MLSTM_W, 2 * MLSTM_HEADS, 2 * MLSTM_HEADS, ATTN_Q_HEADS * HEAD_DIM, ATTN_KV_HEADS * HEAD_DIM, ATTN_KV_HEADS * HEAD_DIM)
N_IN = F_W + 3 * CONV_W + 4 * MLSTM_W + 4 * MLSTM_HEADS + (ATTN_Q_HEADS + 2 * ATTN_KV_HEADS) * HEAD_DIM

kernel_name = 'hybrid_parallel_group_encoder'


def rms_norm(x, g):
    xf = x.astype(jnp.float32)
    y = xf * lax.rsqrt(jnp.mean(xf * xf, axis=-1, keepdims=True) + RMS_EPS)
    return (y * g.astype(jnp.float32)).astype(x.dtype)


def head_norm(y):
    B, S, W = y.shape
    yf = y.astype(jnp.float32).reshape(B, S, W // HEAD_DIM, HEAD_DIM)
    yf = yf * lax.rsqrt(jnp.mean(yf * yf, axis=-1, keepdims=True) + RMS_EPS)
    return yf.reshape(B, S, W).astype(y.dtype)


def axial_rope_tables(n_tok):
    rows = n_tok // GRID_W
    row = jnp.repeat(jnp.arange(rows), GRID_W).astype(jnp.float32)
    col = jnp.tile(jnp.arange(GRID_W), rows).astype(jnp.float32)
    inv = ROPE_THETA ** (-jnp.arange(ROPE_FREQS, dtype=jnp.float32) / ROPE_FREQS)
    ang = jnp.stack([row[:, None] * inv, col[:, None] * inv], axis=1)[:, :, None, :]
    return jnp.cos(ang), jnp.sin(ang)


def apply_axial_rope(x, cos, sin):
    xs = x.astype(jnp.float32).reshape(x.shape[:-1] + (2, 2, ROPE_FREQS))
    rot = jnp.concatenate([-xs[..., 1:2, :], xs[..., 0:1, :]], axis=-2)
    return (xs * cos + rot * sin).reshape(x.shape)


def fourier_heads(a):
    B, S, _ = a.shape
    af = a.astype(jnp.float32).reshape(B, S, GROUP_HEADS, HEAD_DIM)
    return jnp.fft.fft2(af, axes=(1, 3), norm='ortho').real.reshape(B, S, F_W).astype(a.dtype)


def short_conv(z, w, b):
    zp = jnp.pad(z, ((0, 0), (1, 1), (0, 0)))
    return zp[:, :-2] * w[0] + zp[:, 1:-1] * w[1] + zp[:, 2:] * w[2] + b


def mlstm_chunkwise(q, k, v, li, lf):
    B, H, S, d = q.shape
    L = MLSTM_CHUNK
    nc = S // L
    q = q.reshape(B, H, nc, L, d)
    k = k.reshape(B, H, nc, L, d)
    v = v.reshape(B, H, nc, L, d)
    li = li.reshape(B, H, nc, L)
    b = jnp.cumsum(lf.reshape(B, H, nc, L), axis=-1)
    g = b[..., -1]
    causal = jnp.tril(jnp.ones((L, L), dtype=bool))
    dmat = jnp.where(causal, b[..., :, None] - b[..., None, :] + li[..., None, :], -jnp.inf)
    w_end = g[..., None] - b + li
    m_loc = jnp.max(w_end, axis=-1)
    e = jnp.exp(w_end - m_loc[..., None])
    ke = k * e[..., None]
    c_loc = jnp.einsum('bhcld,bhcle->bhcde', ke, v)
    n_loc = jnp.sum(ke, axis=3)

    def step(carry, inp):
        c, n, m = carry
        c_l, n_l, m_l, g_c = inp
        m_new = jnp.maximum(g_c + m, m_l)
        a = jnp.exp(g_c + m - m_new)
        bb = jnp.exp(m_l - m_new)
        c_new = a[..., None, None] * c + bb[..., None, None] * c_l
        n_new = a[..., None] * n + bb[..., None] * n_l
        return (c_new, n_new, m_new), (c, n, m)

    init = (jnp.zeros((B, H, d, d), jnp.float32), jnp.zeros((B, H, d), jnp.float32), jnp.zeros((B, H), jnp.float32))
    xs = (jnp.moveaxis(c_loc, 2, 0), jnp.moveaxis(n_loc, 2, 0), jnp.moveaxis(m_loc, 2, 0), jnp.moveaxis(g, 2, 0))
    _, (c_prev, n_prev, m_prev) = lax.scan(step, init, xs)
    c_prev = jnp.moveaxis(c_prev, 0, 2)
    n_prev = jnp.moveaxis(n_prev, 0, 2)
    m_prev = jnp.moveaxis(m_prev, 0, 2)
    inter_log = b + m_prev[..., None]
    m_t = jnp.maximum(inter_log, jnp.max(dmat, axis=-1))
    p = jnp.exp(dmat - m_t[..., None])
    inter_w = jnp.exp(inter_log - m_t)
    a_mat = p * jnp.einsum('bhcld,bhcsd->bhcls', q, k)
    num = jnp.einsum('bhcls,bhcsd->bhcld', a_mat, v) + inter_w[..., None] * jnp.einsum('bhcld,bhcde->bhcle', q, c_prev)
    den = jnp.sum(a_mat, axis=-1) + inter_w * jnp.einsum('bhcld,bhcd->bhcl', q, n_prev)
    h = num / jnp.maximum(jnp.abs(den), jnp.exp(-m_t))[..., None]
    return h.reshape(B, H, S, d)


def bidir_mlstm(mq, mk, mv, mi, mf, i_bias, f_bias):
    B, S, _ = mq.shape

    def heads(t):
        return t.astype(jnp.float32).reshape(B, S, MLSTM_HEADS, HEAD_DIM).transpose(0, 2, 1, 3)

    q = heads(mq)
    k = heads(mk) * (HEAD_DIM ** -0.5)
    v = heads(mv)
    li = (mi.astype(jnp.float32).reshape(B, S, 2, MLSTM_HEADS) + i_bias.astype(jnp.float32)).transpose(2, 0, 3, 1)
    lf = jax.nn.log_sigmoid(mf.astype(jnp.float32).reshape(B, S, 2, MLSTM_HEADS) + f_bias.astype(jnp.float32)).transpose(2, 0, 3, 1)
    h_fwd = mlstm_chunkwise(q, k, v, li[0], lf[0])
    h_bwd = jnp.flip(mlstm_chunkwise(jnp.flip(q, 2), jnp.flip(k, 2), jnp.flip(v, 2), jnp.flip(li[1], -1), jnp.flip(lf[1], -1)), 2)
    return (h_fwd + h_bwd).transpose(0, 2, 1, 3).reshape(B, S, MLSTM_W).astype(mq.dtype)


def axial_gqa(aq, ak, av, q_gain, k_gain, cos, sin):
    B, S, _ = aq.shape
    dt = aq.dtype
    G = ATTN_Q_HEADS // ATTN_KV_HEADS
    q = aq.reshape(B, S, ATTN_Q_HEADS, HEAD_DIM).transpose(0, 2, 1, 3)
    k = ak.reshape(B, S, ATTN_KV_HEADS, HEAD_DIM).transpose(0, 2, 1, 3)
    v = av.reshape(B, S, ATTN_KV_HEADS, HEAD_DIM).transpose(0, 2, 1, 3)
    q = apply_axial_rope(rms_norm(q, q_gain), cos, sin).astype(dt)
    k = apply_axial_rope(rms_norm(k, k_gain), cos, sin).astype(dt)
    nb = S // Q_BLOCK
    qb = q.reshape(B, ATTN_KV_HEADS, G, nb, Q_BLOCK, HEAD_DIM).transpose(3, 0, 1, 2, 4, 5)
    scale = HEAD_DIM ** -0.5

    def block(qi):
        s = jnp.einsum('bkgqd,bksd->bkgqs', qi, k).astype(jnp.float32) * scale
        p = jax.nn.softmax(s, axis=-1).astype(dt)
        return jnp.einsum('bkgqs,bksd->bkgqd', p, v)

    o = lax.map(block, qb)
    return o.transpose(1, 0, 4, 2, 3, 5).reshape(B, S, ATTN_W)


def token_mixers(h, cos, sin, w_in, conv_w, conv_b, i_bias, f_bias, q_gain, k_gain, out_gain, w_out):
    u = h @ w_in
    cuts = np.cumsum(IN_SIZES)[:-1].tolist()
    a_in, cb, cc, cx, mq, mk, mv, mo, mi, mf, aq, ak, av = jnp.split(u, cuts, axis=-1)
    y_a = fourier_heads(a_in)
    y_b = cb * short_conv(cc * cx, conv_w, conv_b)
    y_c = bidir_mlstm(mq, mk, mv, mi, mf, i_bias, f_bias)
    y_d = axial_gqa(aq, ak, av, q_gain, k_gain, cos, sin)
    y = jnp.concatenate([head_norm(y_a), head_norm(y_b), jax.nn.sigmoid(mo) * head_norm(y_c), head_norm(y_d)], axis=-1) * out_gain
    return y @ w_out


def memory_cross_attn(h, mem_n, w_q, w_kv, q_gain, k_gain, w_o):
    B, S, _ = h.shape
    M = mem_n.shape[1]
    dt = h.dtype
    q = (h @ w_q).reshape(B, S, CA_HEADS, HEAD_DIM).transpose(0, 2, 1, 3)
    kv = (mem_n @ w_kv).reshape(B, M, 2, CA_HEADS, HEAD_DIM)
    k = kv[:, :, 0].transpose(0, 2, 1, 3)
    v = kv[:, :, 1].transpose(0, 2, 1, 3)
    q = rms_norm(q, q_gain)
    k = rms_norm(k, k_gain)
    s = jnp.einsum('bhqd,bhkd->bhqk', q, k).astype(jnp.float32) * (HEAD_DIM ** -0.5)
    p = jax.nn.softmax(s, axis=-1).astype(dt)
    o = jnp.einsum('bhqk,bhkd->bhqd', p, v).transpose(0, 2, 1, 3).reshape(B, S, CA_W)
    return o @ w_o


def trunk(x, mem, g_mix, w_in, conv_w, conv_b, i_bias, f_bias, attn_q_norm, attn_k_norm, out_gain, w_out, g_ca, g_mem, w_ca_q, w_ca_kv, ca_q_norm, ca_k_norm, w_ca_o, g_mlp, w_up, w_down):
    cos, sin = axial_rope_tables(x.shape[1])
    for l in range(DEPTH):
        x = x + token_mixers(rms_norm(x, g_mix[l]), cos, sin, w_in[l], conv_w[l], conv_b[l], i_bias[l], f_bias[l], attn_q_norm[l], attn_k_norm[l], out_gain[l], w_out[l])
        x = x + memory_cross_attn(rms_norm(x, g_ca[l]), rms_norm(mem, g_mem[l]), w_ca_q[l], w_ca_kv[l], ca_q_norm[l], ca_k_norm[l], w_ca_o[l])
        h = rms_norm(x, g_mlp[l])
        x = x + jnp.square(jax.nn.relu(h @ w_up[l])) @ w_down[l]
    return x


def setup_inputs(seed: int = 0) -> dict:
    key = jax.random.key(seed)
    ks = jax.random.split(key, 32)

    def nrm(k, shape, scale):
        return jax.random.normal(k, shape, jnp.float32) * scale

    def gain(k, shape):
        return 1.0 + 0.02 * jax.random.normal(k, shape, jnp.float32)

    return {
        'x_prompt': nrm(ks[0], (BATCH, SEQ, D_MODEL), 1.0),
        'x_sample': nrm(ks[1], (DEC_BATCH, DEC_SEQ, D_MODEL), 1.0),
        'mem_prompt': nrm(ks[2], (BATCH, N_MEM, D_MODEL), 1.0),
        'mem_sample': nrm(ks[3], (DEC_BATCH, N_MEM, D_MODEL), 1.0),
        'g_mix': gain(ks[4], (DEPTH, D_MODEL)),
        'w_in': nrm(ks[5], (DEPTH, D_MODEL, N_IN), D_MODEL ** -0.5),
        'conv_w': nrm(ks[6], (DEPTH, CONV_K, CONV_W), CONV_K ** -0.5),
        'conv_b': nrm(ks[7], (DEPTH, CONV_W), 0.02),
        'i_bias': nrm(ks[8], (DEPTH, 2, MLSTM_HEADS), 0.1),
        'f_bias': jax.random.uniform(ks[9], (DEPTH, 2, MLSTM_HEADS), jnp.float32, 3.0, 6.0),
        'attn_q_norm': gain(ks[10], (DEPTH, HEAD_DIM)),
        'attn_k_norm': gain(ks[11], (DEPTH, HEAD_DIM)),
        'out_gain': gain(ks[12], (DEPTH, D_MIX)),
        'w_out': nrm(ks[13], (DEPTH, D_MIX, D_MODEL), D_MIX ** -0.5),
        'g_ca': gain(ks[14], (DEPTH, D_MODEL)),
        'g_mem': gain(ks[15], (DEPTH, D_MODEL)),
        'w_ca_q': nrm(ks[16], (DEPTH, D_MODEL, CA_W), D_MODEL ** -0.5),
        'w_ca_kv': nrm(ks[17], (DEPTH, D_MODEL, 2 * CA_W), D_MODEL ** -0.5),
        'ca_q_norm': gain(ks[18], (DEPTH, HEAD_DIM)),
        'ca_k_norm': gain(ks[19], (DEPTH, HEAD_DIM)),
        'w_ca_o': nrm(ks[20], (DEPTH, CA_W, D_MODEL), CA_W ** -0.5),
        'g_mlp': gain(ks[21], (DEPTH, D_MODEL)),
        'w_up': nrm(ks[22], (DEPTH, D_MODEL, D_FF), D_MODEL ** -0.5),
        'w_down': nrm(ks[23], (DEPTH, D_FF, D_MODEL), D_FF ** -0.5),
    }


def reference(x_prompt, x_sample, mem_prompt, mem_sample, g_mix, w_in, conv_w, conv_b, i_bias, f_bias, attn_q_norm, attn_k_norm, out_gain, w_out, g_ca, g_mem, w_ca_q, w_ca_kv, ca_q_norm, ca_k_norm, w_ca_o, g_mlp, w_up, w_down):
    y_prompt = trunk(x_prompt, mem_prompt, g_mix, w_in, conv_w, conv_b, i_bias, f_bias, attn_q_norm, attn_k_norm, out_gain, w_out, g_ca, g_mem, w_ca_q, w_ca_kv, ca_q_norm, ca_k_norm, w_ca_o, g_mlp, w_up, w_down)
    y_sample = trunk(x_sample, mem_sample, g_mix, w_in, conv_w, conv_b, i_bias, f_bias, attn_q_norm, attn_k_norm, out_gain, w_out, g_ca, g_mem, w_ca_q, w_ca_kv, ca_q_norm, ca_k_norm, w_ca_o, g_mlp, w_up, w_down)
    return (y_prompt, y_sample)
```

```python
import functools
import math

import jax
import jax.numpy as jnp
from jax import lax
from jax.experimental import pallas as pl
from jax.experimental.pallas import tpu as pltpu

F32 = jnp.float32
BF16 = jnp.bfloat16

D_MODEL = 2048
DEPTH = 4
HEAD_DIM = 128
GROUP_W = 4 * HEAD_DIM
N_COLBLK = 10
GATE_W = 2 * HEAD_DIM
MLSTM_HEADS = 4
MLSTM_CHUNK = 128
ATTN_KV_HEADS = 2
GRID_W = 64
ROPE_THETA = 10000.0
ROPE_FREQS = HEAD_DIM // 4
CA_HEADS = 4
N_MEM = 256
D_FF = 4 * D_MODEL
RMS_EPS = 1e-6
QK_SCALE = HEAD_DIM ** -0.5

BLK_FOURIER, BLK_CB, BLK_CC, BLK_CX, BLK_MQ, BLK_MK, BLK_MV, BLK_MO, BLK_AQ, BLK_AKV = range(N_COLBLK)

V7X_VMEM_LIMIT = 56 * 1024 * 1024
NORM_ROWS = 256


def _params(semantics, vmem=V7X_VMEM_LIMIT):
    return pltpu.CompilerParams(dimension_semantics=semantics, vmem_limit_bytes=vmem)


def _head_norm(y):
    return y * lax.rsqrt(jnp.mean(y * y, axis=-1, keepdims=True) + RMS_EPS)


def _norm_rows_to(x_ref, g_ref, h_ref):
    tm = x_ref.shape[0]
    g = g_ref[...]

    def body(r, carry):
        rows = pl.ds(pl.multiple_of(r * NORM_ROWS, NORM_ROWS), NORM_ROWS)
        xf = x_ref[rows, :]
        ms = jnp.mean(xf * xf, axis=-1, keepdims=True)
        h_ref[rows, :] = (xf * lax.rsqrt(ms + RMS_EPS) * g).astype(BF16)
        return carry

    lax.fori_loop(0, tm // NORM_ROWS, body, 0)


def _norm_proj_kernel(x_ref, g_ref, w_ref, wg_ref, u_ref, gate_ref, h_ref):
    @pl.when(pl.program_id(1) == 0)
    def _():
        _norm_rows_to(x_ref, g_ref, h_ref)
        gate_ref[...] = jnp.dot(h_ref[...], wg_ref[...], preferred_element_type=F32)

    res = jnp.dot(h_ref[...], w_ref[...], preferred_element_type=F32)
    for k in range(u_ref.shape[0]):
        u_ref[k] = res[:, k * GROUP_W:(k + 1) * GROUP_W].astype(BF16)


def _norm_proj(x2, g, w_main, w_gate, tm=1024, nb=2):
    T = x2.shape[0]
    tn = nb * GROUP_W
    return pl.pallas_call(
        _norm_proj_kernel,
        grid=(T // tm, N_COLBLK // nb),
        in_specs=[
            pl.BlockSpec((tm, D_MODEL), lambda i, j: (i, 0)),
            pl.BlockSpec((1, D_MODEL), lambda i, j: (0, 0)),
            pl.BlockSpec((D_MODEL, tn), lambda i, j: (0, j)),
            pl.BlockSpec((D_MODEL, GATE_W), lambda i, j: (0, 0)),
        ],
        out_specs=[
            pl.BlockSpec((nb, tm, GROUP_W), lambda i, j: (j, i, 0)),
            pl.BlockSpec((tm, GATE_W), lambda i, j: (i, 0)),
        ],
        out_shape=[
            jax.ShapeDtypeStruct((N_COLBLK, T, GROUP_W), BF16),
            jax.ShapeDtypeStruct((T, GATE_W), F32),
        ],
        scratch_shapes=[pltpu.VMEM((tm, D_MODEL), BF16)],
        compiler_params=_params(("parallel", "arbitrary")),
        name="norm_proj",
    )(x2, g, w_main, w_gate)


def _fourier_tables(S):
    n2 = HEAD_DIM
    n1 = S // n2
    i1 = jnp.arange(n1, dtype=jnp.int32)
    a1 = (2.0 * math.pi / n1) * ((i1[:, None] * i1[None, :]) % n1).astype(F32)
    f1 = jnp.concatenate([jnp.cos(a1), -jnp.sin(a1)], axis=0).astype(BF16)
    i2 = jnp.arange(n2, dtype=jnp.int32)
    p = i1[:, None, None] + n1 * i2[None, :, None]
    k = (p * i2[None, None, :]) % S
    a2 = (2.0 * math.pi / S) * k.astype(F32)
    c2, s2 = jnp.cos(a2), jnp.sin(a2)
    g = jnp.concatenate([jnp.concatenate([c2, s2], axis=2),
                         jnp.concatenate([-s2, c2], axis=2)], axis=1).astype(BF16)
    ac = (2.0 * math.pi / HEAD_DIM) * ((i2[:, None] * i2[None, :]) % HEAD_DIM).astype(F32)
    cs = jnp.concatenate([jnp.cos(ac), jnp.sin(ac)], axis=0).astype(BF16)
    return f1, g, cs


def _fourier_stage1_kernel(a_ref, f1_ref, y_ref):
    n1 = a_ref.shape[0]
    res = jnp.dot(f1_ref[...], a_ref[...], preferred_element_type=F32)
    y_ref[0] = res[:n1].astype(BF16)
    y_ref[1] = res[n1:].astype(BF16)


def _fourier_stage2_kernel(y_ref, g_ref, cs_ref, o_ref, *, scale):
    n_p = y_ref.shape[1]
    cs = cs_ref[...]
    for p in range(n_p):
        ycat = jnp.concatenate([y_ref[0, p], y_ref[1, p]], axis=0)
        x = jnp.dot(g_ref[p], ycat, preferred_element_type=F32).astype(BF16)
        for grp in range(4):
            lanes = slice(grp * HEAD_DIM, (grp + 1) * HEAD_DIM)
            xg = jnp.concatenate([x[:HEAD_DIM, lanes], x[HEAD_DIM:, lanes]], axis=1)
            o = jnp.dot(xg, cs, preferred_element_type=F32) * scale
            col = p * GROUP_W + grp * HEAD_DIM
            o_ref[:, col:col + HEAD_DIM] = _head_norm(o).astype(BF16)


def _fourier(u3, B, S, tables):
    f1, g, cs = tables
    n2 = HEAD_DIM
    n1 = S // n2
    T = B * S
    ncol = min(16, n2)
    a_view = u3.reshape(N_COLBLK, B, n1, n2 * GROUP_W)
    y = pl.pallas_call(
        _fourier_stage1_kernel,
        grid=(B, n2 // ncol),
        in_specs=[
            pl.BlockSpec((None, None, n1, ncol * GROUP_W), lambda b, j: (BLK_FOURIER, b, 0, j)),
            pl.BlockSpec((2 * n1, n1), lambda b, j: (0, 0)),
        ],
        out_specs=pl.BlockSpec((None, 2, n1, ncol * GROUP_W), lambda b, j: (b, 0, 0, j)),
        out_shape=jax.ShapeDtypeStruct((B, 2, n1, n2 * GROUP_W), BF16),
        compiler_params=_params(("parallel", "parallel")),
        name="fourier_stage1",
    )(a_view, f1)
    y5 = y.reshape(B, 2, n1, n2, GROUP_W)
    n_p = 8
    out = pl.pallas_call(
        functools.partial(_fourier_stage2_kernel, scale=1.0 / math.sqrt(S * HEAD_DIM)),
        grid=(B, n1 // n_p),
        in_specs=[
            pl.BlockSpec((None, 2, n_p, n2, GROUP_W), lambda b, j: (b, 0, j, 0, 0)),
            pl.BlockSpec((n_p, 2 * n2, 2 * n2), lambda b, j: (j, 0, 0)),
            pl.BlockSpec((2 * HEAD_DIM, HEAD_DIM), lambda b, j: (0, 0)),
        ],
        out_specs=pl.BlockSpec((None, n2, n_p * GROUP_W), lambda b, j: (b, 0, j)),
        out_shape=jax.ShapeDtypeStruct((B, n2, n1 * GROUP_W), BF16),
        compiler_params=_params(("parallel", "parallel")),
        name="fourier_stage2",
    )(y5, g, cs)
    return out.reshape(T, GROUP_W)


def _log_sigmoid(x):
    return jnp.minimum(x, 0.0) - jnp.log1p(jnp.exp(-jnp.abs(x)))


def _mlstm_chunk(d, q, k, v, gi, gf, ib, fb, tri, mask, c_ref, n_ref, m_ref, out_ref, rows):
    li = gi + ib
    lf = _log_sigmoid(gf + fb)
    b = jnp.dot(tri, lf, precision=lax.Precision.HIGHEST, preferred_element_type=F32)
    g = jnp.sum(lf, axis=0, keepdims=True)
    w_end = g - b + li
    m_loc = jnp.max(w_end, axis=0, keepdims=True)
    e = jnp.exp(w_end - m_loc)
    a_t = (li - b).T
    m_old = m_ref[d:d + 1, :]
    m_new = jnp.maximum(g + m_old, m_loc)
    a_sc = jnp.exp(g + m_old - m_new)
    b_sc = jnp.exp(m_loc - m_new)
    for h in range(MLSTM_HEADS):
        hp = MLSTM_HEADS * d + h
        lanes = slice(h * HEAD_DIM, (h + 1) * HEAD_DIM)
        qh = q[:, lanes]
        vh = v[:, lanes]
        ks = k[:, lanes].astype(F32) * QK_SCALE
        bc = b[:, hp:hp + 1]
        dm = jnp.where(mask, bc + a_t[hp:hp + 1, :], -jnp.inf)
        inter_log = bc + m_old[:, hp:hp + 1]
        m_t = jnp.maximum(inter_log, jnp.max(dm, axis=-1, keepdims=True))
        p = jnp.exp(dm - m_t)
        inter_w = jnp.exp(inter_log - m_t)
        qk = lax.dot_general(qh, ks.astype(BF16), (((1,), (1,)), ((), ())), preferred_element_type=F32)
        a_mat = p * qk
        c_prev = c_ref[hp]
        n_prev = n_ref[hp:hp + 1, :]
        num = (jnp.dot(a_mat.astype(BF16), vh, preferred_element_type=F32)
               + inter_w * jnp.dot(qh, c_prev.astype(BF16), preferred_element_type=F32))
        den = (jnp.sum(a_mat, axis=-1, keepdims=True)
               + inter_w * jnp.sum(qh.astype(F32) * n_prev, axis=-1, keepdims=True))
        hout = num / jnp.maximum(jnp.abs(den), jnp.exp(-m_t))
        out_ref[rows, lanes] = hout.astype(out_ref.dtype)
        ke = ks * e[:, hp:hp + 1]
        c_loc = jnp.dot(ke.T.astype(BF16), vh, preferred_element_type=F32)
        n_loc = jnp.sum(ke, axis=0, keepdims=True)
        ah = a_sc[:, hp:hp + 1]
        bh = b_sc[:, hp:hp + 1]
        c_ref[hp] = ah * c_prev + bh * c_loc
        n_ref[hp:hp + 1, :] = ah * n_prev + bh * n_loc
    m_ref[d:d + 1, :] = m_new


def _mlstm_kernel(qf_ref, kf_ref, vf_ref, gf_ref, qb_ref, kb_ref, vb_ref, gb_ref, ib_ref, fb_ref,
                  hf_ref, hb_ref, c_ref, n_ref, m_ref):
    L = MLSTM_CHUNK
    n_sub = qf_ref.shape[0] // L

    @pl.when(pl.program_id(1) == 0)
    def _():
        c_ref[...] = jnp.zeros_like(c_ref)
        n_ref[...] = jnp.zeros_like(n_ref)
        m_ref[...] = jnp.zeros_like(m_ref)

    row = lax.broadcasted_iota(jnp.int32, (L, L), 0)
    col = lax.broadcasted_iota(jnp.int32, (L, L), 1)
    mask_f = col <= row
    mask_b = col >= row
    tri_f = jnp.where(mask_f, 1.0, 0.0).astype(F32)
    tri_b = jnp.where(mask_b, 1.0, 0.0).astype(F32)
    ib = ib_ref[...]
    fb = fb_ref[...]

    def body(j, carry):
        rf = pl.ds(pl.multiple_of(j * L, L), L)
        rb = pl.ds(pl.multiple_of((n_sub - 1 - j) * L, L), L)
        _mlstm_chunk(0, qf_ref[rf, :], kf_ref[rf, :], vf_ref[rf, :], gf_ref[rf, :HEAD_DIM], gf_ref[rf, HEAD_DIM:],
                     ib, fb, tri_f, mask_f, c_ref, n_ref, m_ref, hf_ref, rf)
        _mlstm_chunk(1, qb_ref[rb, :], kb_ref[rb, :], vb_ref[rb, :], gb_ref[rb, :HEAD_DIM], gb_ref[rb, HEAD_DIM:],
                     ib, fb, tri_b, mask_b, c_ref, n_ref, m_ref, hb_ref, rb)
        return carry

    lax.fori_loop(0, n_sub, body, 0)


def _mlstm(u3, gates, ib, fb, B, S, n_sub=4):
    T = B * S
    rows = n_sub * MLSTM_CHUNK
    nblk = S // rows
    u4 = u3.reshape(N_COLBLK, B, S, GROUP_W)
    g3 = gates.reshape(B, S, GATE_W)

    def fwd(blk):
        return pl.BlockSpec((None, None, rows, GROUP_W), lambda b, c: (blk, b, c, 0))

    def bwd(blk):
        return pl.BlockSpec((None, None, rows, GROUP_W), lambda b, c: (blk, b, nblk - 1 - c, 0))

    hf, hb = pl.pallas_call(
        _mlstm_kernel,
        grid=(B, nblk),
        in_specs=[
            fwd(BLK_MQ), fwd(BLK_MK), fwd(BLK_MV),
            pl.BlockSpec((None, rows, GATE_W), lambda b, c: (b, c, 0)),
            bwd(BLK_MQ), bwd(BLK_MK), bwd(BLK_MV),
            pl.BlockSpec((None, rows, GATE_W), lambda b, c: (b, nblk - 1 - c, 0)),
            pl.BlockSpec((1, HEAD_DIM), lambda b, c: (0, 0)),
            pl.BlockSpec((1, HEAD_DIM), lambda b, c: (0, 0)),
        ],
        out_specs=[
            pl.BlockSpec((None, rows, GROUP_W), lambda b, c: (b, c, 0)),
            pl.BlockSpec((None, rows, GROUP_W), lambda b, c: (b, nblk - 1 - c, 0)),
        ],
        out_shape=[jax.ShapeDtypeStruct((B, S, GROUP_W), BF16)] * 2,
        scratch_shapes=[
            pltpu.VMEM((2 * MLSTM_HEADS, HEAD_DIM, HEAD_DIM), F32),
            pltpu.VMEM((2 * MLSTM_HEADS, HEAD_DIM), F32),
            pltpu.VMEM((8, HEAD_DIM), F32),
        ],
        compiler_params=_params(("parallel", "arbitrary")),
        name="mlstm",
    )(u4, u4, u4, g3, u4, u4, u4, g3, ib, fb)
    return hf.reshape(T, GROUP_W), hb.reshape(T, GROUP_W)


def _rope_tables(S):
    rows = S // GRID_W
    row = jnp.repeat(jnp.arange(rows), GRID_W).astype(F32)
    colp = jnp.tile(jnp.arange(GRID_W), rows).astype(F32)
    inv = ROPE_THETA ** (-jnp.arange(ROPE_FREQS, dtype=F32) / ROPE_FREQS)
    ar = row[:, None] * inv
    ac = colp[:, None] * inv
    ang = jnp.concatenate([ar, ar, ac, ac], axis=1)
    first_half = (jnp.arange(HEAD_DIM) % (2 * ROPE_FREQS)) < ROPE_FREQS
    return jnp.cos(ang), jnp.where(first_half, -1.0, 1.0) * jnp.sin(ang)


def _rope(x, cos, sin_signed, first_half):
    partner = jnp.where(first_half, pltpu.roll(x, HEAD_DIM - ROPE_FREQS, axis=1), pltpu.roll(x, ROPE_FREQS, axis=1))
    return x * cos + partner * sin_signed


def _attn_prep_kernel(aq_ref, akv_ref, cos_ref, sin_ref, qg_ref, kg_ref, q_ref, kt_ref):
    cos = cos_ref[...]
    sin = sin_ref[...]
    lane = lax.broadcasted_iota(jnp.int32, cos.shape, 1)
    first_half = (lane % (2 * ROPE_FREQS)) < ROPE_FREQS
    qg = qg_ref[...]
    kg = kg_ref[...]
    for h in range(4):
        lanes = slice(h * HEAD_DIM, (h + 1) * HEAD_DIM)
        qh = _head_norm(aq_ref[:, lanes].astype(F32)) * qg
        q_ref[:, lanes] = (_rope(qh, cos, sin, first_half) * QK_SCALE).astype(BF16)
    for h in range(ATTN_KV_HEADS):
        lanes = slice(h * HEAD_DIM, (h + 1) * HEAD_DIM)
        kh = _head_norm(akv_ref[:, lanes].astype(F32)) * kg
        kt_ref[h] = _rope(kh, cos, sin, first_half).T.astype(BF16)


def _flash_kernel(q_ref, kt_ref, v_ref, o_ref, m_ref, l_ref, acc_ref):
    tq = q_ref.shape[0]
    nkc, _, tk = kt_ref.shape
    q2 = jnp.concatenate([q_ref[:, :HEAD_DIM], q_ref[:, HEAD_DIM:]], axis=0)
    m_ref[...] = jnp.full_like(m_ref, -jnp.inf)
    l_ref[...] = jnp.zeros_like(l_ref)
    acc_ref[...] = jnp.zeros_like(acc_ref)

    def body(c, carry):
        s = jnp.dot(q2, kt_ref[c], preferred_element_type=F32)
        m_prev = m_ref[...]
        m_new = jnp.maximum(m_prev, jnp.max(s, axis=-1, keepdims=True))
        alpha = jnp.exp(m_prev - m_new)
        p = jnp.exp(s - m_new)
        l_ref[...] = alpha * l_ref[...] + jnp.sum(p, axis=-1, keepdims=True)
        vc = v_ref[pl.ds(pl.multiple_of(c * tk, tk), tk), :]
        acc_ref[...] = alpha * acc_ref[...] + jnp.dot(p.astype(BF16), vc, preferred_element_type=F32)
        m_ref[...] = m_new
        return carry

    lax.fori_loop(0, nkc, body, 0)
    o = acc_ref[...] * pl.reciprocal(l_ref[...], approx=False)
    o_ref[:, :HEAD_DIM] = o[:tq].astype(o_ref.dtype)
    o_ref[:, HEAD_DIM:] = o[tq:].astype(o_ref.dtype)


def _axial_gqa(u3, q_gain, k_gain, rope, B, S, tq=512, tk=512):
    T = B * S
    cos, sin = rope
    u4 = u3.reshape(N_COLBLK, B, S, GROUP_W)
    nkc = S // tk
    q_r, kt = pl.pallas_call(
        _attn_prep_kernel,
        grid=(B, nkc),
        in_specs=[
            pl.BlockSpec((None, None, tk, GROUP_W), lambda b, i: (BLK_AQ, b, i, 0)),
            pl.BlockSpec((None, None, tk, GROUP_W), lambda b, i: (BLK_AKV, b, i, 0)),
            pl.BlockSpec((tk, HEAD_DIM), lambda b, i: (i, 0)),
            pl.BlockSpec((tk, HEAD_DIM), lambda b, i: (i, 0)),
            pl.BlockSpec((1, HEAD_DIM), lambda b, i: (0, 0)),
            pl.BlockSpec((1, HEAD_DIM), lambda b, i: (0, 0)),
        ],
        out_specs=[
            pl.BlockSpec((None, tk, GROUP_W), lambda b, i: (b, i, 0)),
            pl.BlockSpec((None, ATTN_KV_HEADS, None, HEAD_DIM, tk), lambda b, i: (b, 0, i, 0, 0)),
        ],
        out_shape=[
            jax.ShapeDtypeStruct((B, S, GROUP_W), BF16),
            jax.ShapeDtypeStruct((B, ATTN_KV_HEADS, nkc, HEAD_DIM, tk), BF16),
        ],
        compiler_params=_params(("parallel", "parallel")),
        name="attn_prep",
    )(u4, u4, cos, sin, q_gain, k_gain)
    out = pl.pallas_call(
        _flash_kernel,
        grid=(B, ATTN_KV_HEADS, S // tq),
        in_specs=[
            pl.BlockSpec((None, tq, 2 * HEAD_DIM), lambda b, k, i: (b, i, k)),
            pl.BlockSpec((None, None, nkc, HEAD_DIM, tk), lambda b, k, i: (b, k, 0, 0, 0)),
            pl.BlockSpec((None, None, S, HEAD_DIM), lambda b, k, i: (BLK_AKV, b, 0, ATTN_KV_HEADS + k)),
        ],
        out_specs=pl.BlockSpec((None, tq, 2 * HEAD_DIM), lambda b, k, i: (b, i, k)),
        out_shape=jax.ShapeDtypeStruct((B, S, GROUP_W), BF16),
        scratch_shapes=[
            pltpu.VMEM((2 * tq, 1), F32),
            pltpu.VMEM((2 * tq, 1), F32),
            pltpu.VMEM((2 * tq, HEAD_DIM), F32),
        ],
        compiler_params=_params(("parallel", "parallel", "arbitrary")),
        name="flash_gqa",
    )(q_r, kt, u4)
    return out.reshape(T, GROUP_W)


def _combine_kernel(x_ref, ya_ref, cb_ref, cc_ref, cx_ref, ccp_ref, cxp_ref, ccn_ref, cxn_ref, mo_ref,
                    hf_ref, hb_ref, yd_ref, cw_ref, cbias_ref, gain_ref, w_ref, o_ref, y_ref, *, tiles_per_seq):
    tm = x_ref.shape[0]
    i = pl.program_id(0)
    halo = ccp_ref.shape[0]
    at_start = (i % tiles_per_seq) == 0
    at_end = (i % tiles_per_seq) == tiles_per_seq - 1
    gain = gain_ref[...]

    z = cc_ref[...].astype(F32) * cx_ref[...].astype(F32)
    z_before = ccp_ref[halo - 1:halo, :].astype(F32) * cxp_ref[halo - 1:halo, :].astype(F32)
    z_after = ccn_ref[0:1, :].astype(F32) * cxn_ref[0:1, :].astype(F32)
    z_before = jnp.where(at_start, 0.0, z_before)
    z_after = jnp.where(at_end, 0.0, z_after)
    rid = lax.broadcasted_iota(jnp.int32, z.shape, 0)
    z_m1 = jnp.where(rid == 0, z_before, pltpu.roll(z, 1, axis=0))
    z_p1 = jnp.where(rid == tm - 1, z_after, pltpu.roll(z, tm - 1, axis=0))
    conv = z_m1 * cw_ref[0:1, :] + z * cw_ref[1:2, :] + z_p1 * cw_ref[2:3, :] + cbias_ref[...]
    yb = cb_ref[...].astype(F32) * conv

    for grp in range(4):
        lanes = slice(grp * HEAD_DIM, (grp + 1) * HEAD_DIM)
        y_ref[:, lanes] = (ya_ref[:, lanes].astype(F32) * gain[:, lanes]).astype(BF16)
        gl = slice(GROUP_W + grp * HEAD_DIM, GROUP_W + (grp + 1) * HEAD_DIM)
        y_ref[:, gl] = (_head_norm(yb[:, lanes]) * gain[:, gl]).astype(BF16)
        gl = slice(2 * GROUP_W + grp * HEAD_DIM, 2 * GROUP_W + (grp + 1) * HEAD_DIM)
        yc = _head_norm(hf_ref[:, lanes].astype(F32) + hb_ref[:, lanes].astype(F32))
        y_ref[:, gl] = (jax.nn.sigmoid(mo_ref[:, lanes].astype(F32)) * yc * gain[:, gl]).astype(BF16)
        gl = slice(3 * GROUP_W + grp * HEAD_DIM, 3 * GROUP_W + (grp + 1) * HEAD_DIM)
        y_ref[:, gl] = (_head_norm(yd_ref[:, lanes].astype(F32)) * gain[:, gl]).astype(BF16)

    o_ref[...] = x_ref[...] + jnp.dot(y_ref[...], w_ref[...], preferred_element_type=F32)


def _combine_out(x2, ya, u3, hf, hb, yd, conv_w, conv_b, out_gain, w_out, S, tm=256, halo=16):
    T = x2.shape[0]
    hb_per_tile = tm // halo
    n_halo = T // halo

    def u_blk(blk):
        return pl.BlockSpec((None, tm, GROUP_W), lambda i: (blk, i, 0))

    def u_prev(blk):
        return pl.BlockSpec((None, halo, GROUP_W), lambda i: (blk, jnp.maximum(i * hb_per_tile - 1, 0), 0))

    def u_next(blk):
        return pl.BlockSpec((None, halo, GROUP_W), lambda i: (blk, jnp.minimum((i + 1) * hb_per_tile, n_halo - 1), 0))

    tok = pl.BlockSpec((tm, GROUP_W), lambda i: (i, 0))
    return pl.pallas_call(
        functools.partial(_combine_kernel, tiles_per_seq=S // tm),
        grid=(T // tm,),
        in_specs=[
            pl.BlockSpec((tm, D_MODEL), lambda i: (i, 0)),
            tok,
            u_blk(BLK_CB), u_blk(BLK_CC), u_blk(BLK_CX),
            u_prev(BLK_CC), u_prev(BLK_CX), u_next(BLK_CC), u_next(BLK_CX),
            u_blk(BLK_MO),
            tok, tok, tok,
            pl.BlockSpec((8, GROUP_W), lambda i: (0, 0)),
            pl.BlockSpec((1, GROUP_W), lambda i: (0, 0)),
            pl.BlockSpec((1, D_MODEL), lambda i: (0, 0)),
            pl.BlockSpec((D_MODEL, D_MODEL), lambda i: (0, 0)),
        ],
        out_specs=pl.BlockSpec((tm, D_MODEL), lambda i: (i, 0)),
        out_shape=jax.ShapeDtypeStruct((T, D_MODEL), F32),
        scratch_shapes=[pltpu.VMEM((tm, D_MODEL), BF16)],
        compiler_params=_params(("parallel",)),
        name="combine_out",
    )(x2, ya, u3, u3, u3, u3, u3, u3, u3, u3, hf, hb, yd, conv_w, conv_b, out_gain, w_out)


def _mem_kv_kernel(mem_ref, g_ref, w_ref, kg_ref, kt_ref, v_ref, h_ref):
    _norm_rows_to(mem_ref, g_ref, h_ref)
    kv = jnp.dot(h_ref[...], w_ref[...], preferred_element_type=F32)
    kg = kg_ref[...]
    for h in range(CA_HEADS):
        lanes = slice(h * HEAD_DIM, (h + 1) * HEAD_DIM)
        kt_ref[h] = (_head_norm(kv[:, lanes]) * kg).T.astype(BF16)
    v_ref[...] = kv[:, CA_HEADS * HEAD_DIM:].astype(BF16)


def _mem_kv(mem, g_mem, w_kv, k_gain):
    B = mem.shape[0]
    return pl.pallas_call(
        _mem_kv_kernel,
        grid=(B,),
        in_specs=[
            pl.BlockSpec((None, N_MEM, D_MODEL), lambda b: (b, 0, 0)),
            pl.BlockSpec((1, D_MODEL), lambda b: (0, 0)),
            pl.BlockSpec((D_MODEL, 2 * GROUP_W), lambda b: (0, 0)),
            pl.BlockSpec((1, HEAD_DIM), lambda b: (0, 0)),
        ],
        out_specs=[
            pl.BlockSpec((None, CA_HEADS, HEAD_DIM, N_MEM), lambda b: (b, 0, 0, 0)),
            pl.BlockSpec((None, N_MEM, GROUP_W), lambda b: (b, 0, 0)),
        ],
        out_shape=[
            jax.ShapeDtypeStruct((B, CA_HEADS, HEAD_DIM, N_MEM), BF16),
            jax.ShapeDtypeStruct((B, N_MEM, GROUP_W), BF16),
        ],
        scratch_shapes=[pltpu.VMEM((N_MEM, D_MODEL), BF16)],
        compiler_params=_params(("parallel",)),
        name="mem_kv",
    )(mem, g_mem, w_kv, k_gain)


def _cross_attn_kernel(x_ref, g_ref, wq_ref, qg_ref, kt_ref, v_ref, wo_ref, o_ref, h_ref, oc_ref):
    _norm_rows_to(x_ref, g_ref, h_ref)
    q = jnp.dot(h_ref[...], wq_ref[...], preferred_element_type=F32)
    qg = qg_ref[...]
    for h in range(CA_HEADS):
        lanes = slice(h * HEAD_DIM, (h + 1) * HEAD_DIM)
        qh = (_head_norm(q[:, lanes]) * (qg * QK_SCALE)).astype(BF16)
        s = jnp.dot(qh, kt_ref[h], preferred_element_type=F32)
        e = jnp.exp(s - jnp.max(s, axis=-1, keepdims=True))
        o = jnp.dot(e.astype(BF16), v_ref[:, lanes], preferred_element_type=F32)
        oc_ref[:, lanes] = (o * pl.reciprocal(jnp.sum(e, axis=-1, keepdims=True), approx=False)).astype(BF16)
    o_ref[...] = x_ref[...] + jnp.dot(oc_ref[...], wo_ref[...], preferred_element_type=F32)


def _cross_attn(x2, g_ca, w_q, q_gain, kt, v, w_o, B, S, tm=512):
    T = x2.shape[0]
    x3 = x2.reshape(B, S, D_MODEL)
    out = pl.pallas_call(
        _cross_attn_kernel,
        grid=(B, S // tm),
        in_specs=[
            pl.BlockSpec((None, tm, D_MODEL), lambda b, i: (b, i, 0)),
            pl.BlockSpec((1, D_MODEL), lambda b, i: (0, 0)),
            pl.BlockSpec((D_MODEL, GROUP_W), lambda b, i: (0, 0)),
            pl.BlockSpec((1, HEAD_DIM), lambda b, i: (0, 0)),
            pl.BlockSpec((None, CA_HEADS, HEAD_DIM, N_MEM), lambda b, i: (b, 0, 0, 0)),
            pl.BlockSpec((None, N_MEM, GROUP_W), lambda b, i: (b, 0, 0)),
            pl.BlockSpec((GROUP_W, D_MODEL), lambda b, i: (0, 0)),
        ],
        out_specs=pl.BlockSpec((None, tm, D_MODEL), lambda b, i: (b, i, 0)),
        out_shape=jax.ShapeDtypeStruct((B, S, D_MODEL), F32),
        scratch_shapes=[pltpu.VMEM((tm, D_MODEL), BF16), pltpu.VMEM((tm, GROUP_W), BF16)],
        compiler_params=_params(("parallel", "parallel")),
        name="cross_attn",
    )(x3, g_ca, w_q, q_gain, kt, v, w_o)
    return out.reshape(T, D_MODEL)


def _mlp_kernel(x_ref, g_ref, wu_ref, wd_ref, o_ref, h_ref):
    @pl.when(pl.program_id(1) == 0)
    def _():
        _norm_rows_to(x_ref, g_ref, h_ref)
        o_ref[...] = x_ref[...]

    up = jnp.dot(h_ref[...], wu_ref[...], preferred_element_type=F32)
    act = jnp.square(jnp.maximum(up, 0.0)).astype(BF16)
    o_ref[...] += jnp.dot(act, wd_ref[...], preferred_element_type=F32)


def _mlp(x2, g, w_up, w_down, tm=512, tf=1024):
    T = x2.shape[0]
    return pl.pallas_call(
        _mlp_kernel,
        grid=(T // tm, D_FF // tf),
        in_specs=[
            pl.BlockSpec((tm, D_MODEL), lambda i, j: (i, 0)),
            pl.BlockSpec((1, D_MODEL), lambda i, j: (0, 0)),
            pl.BlockSpec((D_MODEL, tf), lambda i, j: (0, j)),
            pl.BlockSpec((tf, D_MODEL), lambda i, j: (j, 0)),
        ],
        out_specs=pl.BlockSpec((tm, D_MODEL), lambda i, j: (i, 0)),
        out_shape=jax.ShapeDtypeStruct((T, D_MODEL), F32),
        scratch_shapes=[pltpu.VMEM((tm, D_MODEL), BF16)],
        compiler_params=_params(("parallel", "arbitrary")),
        name="mlp",
    )(x2, g, w_up, w_down)


def _prep_weights(w_in, conv_w, i_bias, f_bias, w_out, w_ca_q, w_ca_kv, w_ca_o, w_up, w_down):
    n_main = 8 * GROUP_W
    w_main = jnp.concatenate([w_in[:, :, :n_main], w_in[:, :, n_main + 16:]], axis=2).astype(BF16)
    pad = jnp.zeros((DEPTH, D_MODEL, HEAD_DIM - 8), F32)
    w_gate = jnp.concatenate([w_in[:, :, n_main:n_main + 8], pad, w_in[:, :, n_main + 8:n_main + 16], pad],
                             axis=2).astype(BF16)
    lane_pad = jnp.zeros((DEPTH, 1, HEAD_DIM - 8), F32)
    ib = jnp.concatenate([i_bias.reshape(DEPTH, 1, 8), lane_pad], axis=2)
    fb = jnp.concatenate([f_bias.reshape(DEPTH, 1, 8), lane_pad], axis=2)
    cw = jnp.concatenate([conv_w, jnp.zeros((DEPTH, 5, GROUP_W), F32)], axis=1)
    return dict(w_main=w_main, w_gate=w_gate, ib=ib, fb=fb, cw=cw, w_out=w_out.astype(BF16),
                w_ca_q=w_ca_q.astype(BF16), w_ca_kv=w_ca_kv.astype(BF16), w_ca_o=w_ca_o.astype(BF16),
                w_up=w_up.astype(BF16), w_down=w_down.astype(BF16))


def _trunk(x, mem, pw, g_mix, conv_b, attn_q_norm, attn_k_norm, out_gain, g_ca, g_mem, ca_q_norm, ca_k_norm, g_mlp):
    B, S, _ = x.shape
    T = B * S
    x2 = x.reshape(T, D_MODEL)
    rope = _rope_tables(S)
    ftab = _fourier_tables(S)
    for l in range(DEPTH):
        u3, gates = _norm_proj(x2, g_mix[l][None], pw["w_main"][l], pw["w_gate"][l])
        ya = _fourier(u3, B, S, ftab)
        hf, hb = _mlstm(u3, gates, pw["ib"][l], pw["fb"][l], B, S)
        yd = _axial_gqa(u3, attn_q_norm[l][None], attn_k_norm[l][None], rope, B, S)
        x2 = _combine_out(x2, ya, u3, hf, hb, yd, pw["cw"][l], conv_b[l][None], out_gain[l][None], pw["w_out"][l], S)
        kt, v = _mem_kv(mem, g_mem[l][None], pw["w_ca_kv"][l], ca_k_norm[l][None])
        x2 = _cross_attn(x2, g_ca[l][None], pw["w_ca_q"][l], ca_q_norm[l][None], kt, v, pw["w_ca_o"][l], B, S)
        x2 = _mlp(x2, g_mlp[l][None], pw["w_up"][l], pw["w_down"][l])
    return x2.reshape(B, S, D_MODEL)


def kernel(x_prompt, x_sample, mem_prompt, mem_sample, g_mix, w_in, conv_w, conv_b, i_bias, f_bias, attn_q_norm,
           attn_k_norm, out_gain, w_out, g_ca, g_mem, w_ca_q, w_ca_kv, ca_q_norm, ca_k_norm, w_ca_o, g_mlp, w_up,
           w_down):
    pw = _prep_weights(w_in, conv_w, i_bias, f_bias, w_out, w_ca_q, w_ca_kv, w_ca_o, w_up, w_down)
    args = (pw, g_mix, conv_b, attn_q_norm, attn_k_norm, out_gain, g_ca, g_mem, ca_q_norm, ca_k_norm, g_mlp)
    y_prompt = _trunk(x_prompt, mem_prompt, *args)
    y_sample = _trunk(x_sample, mem_sample, *args)
    return (y_prompt, y_sample)
```

```python
import functools
import math

import jax
import jax.numpy as jnp
from jax import lax
from jax.experimental import pallas as pl
from jax.experimental.pallas import tpu as pltpu

F32 = jnp.float32
BF16 = jnp.bfloat16

D_MODEL = 2048
DEPTH = 4
HEAD_DIM = 128
GROUP_W = 4 * HEAD_DIM
N_COLBLK = 10
GATE_W = 2 * HEAD_DIM
MLSTM_HEADS = 4
MLSTM_CHUNK = 128
ATTN_KV_HEADS = 2
GRID_W = 64
ROPE_THETA = 10000.0
ROPE_FREQS = HEAD_DIM // 4
CA_HEADS = 4
N_MEM = 256
D_FF = 4 * D_MODEL
RMS_EPS = 1e-6
QK_SCALE = HEAD_DIM ** -0.5
LOG2_E = math.log2(math.e)

BLK_FOURIER, BLK_CB, BLK_CC, BLK_CX, BLK_MQ, BLK_MK, BLK_MV, BLK_MO, BLK_AQ, BLK_AKV = range(N_COLBLK)

V7X_VMEM_LIMIT = 56 * 1024 * 1024
NORM_ROWS = 256


def _params(semantics, vmem=V7X_VMEM_LIMIT):
    return pltpu.CompilerParams(dimension_semantics=semantics, vmem_limit_bytes=vmem)


def _head_norm(y):
    return y * lax.rsqrt(jnp.mean(y * y, axis=-1, keepdims=True) + RMS_EPS)


def _norm_rows_to(x_ref, g_ref, h_ref):
    tm = x_ref.shape[0]
    g = g_ref[...]

    def body(r, carry):
        rows = pl.ds(pl.multiple_of(r * NORM_ROWS, NORM_ROWS), NORM_ROWS)
        xf = x_ref[rows, :]
        ms = jnp.mean(xf * xf, axis=-1, keepdims=True)
        h_ref[rows, :] = (xf * lax.rsqrt(ms + RMS_EPS) * g).astype(BF16)
        return carry

    lax.fori_loop(0, tm // NORM_ROWS, body, 0)


def _norm_proj_kernel(x_ref, g_ref, w_ref, wg_ref, u_ref, gate_ref, h_ref):
    @pl.when(pl.program_id(1) == 0)
    def _():
        _norm_rows_to(x_ref, g_ref, h_ref)
        gate_ref[...] = jnp.dot(h_ref[...], wg_ref[...], preferred_element_type=F32)

    res = jnp.dot(h_ref[...], w_ref[...], preferred_element_type=F32)
    for k in range(u_ref.shape[0]):
        u_ref[k] = res[:, k * GROUP_W:(k + 1) * GROUP_W].astype(BF16)


def _norm_proj(x2, g, w_main, w_gate, tm=1024, nb=2):
    T = x2.shape[0]
    tn = nb * GROUP_W
    return pl.pallas_call(
        _norm_proj_kernel,
        grid=(T // tm, N_COLBLK // nb),
        in_specs=[
            pl.BlockSpec((tm, D_MODEL), lambda i, j: (i, 0)),
            pl.BlockSpec((1, D_MODEL), lambda i, j: (0, 0)),
            pl.BlockSpec((D_MODEL, tn), lambda i, j: (0, j)),
            pl.BlockSpec((D_MODEL, GATE_W), lambda i, j: (0, 0)),
        ],
        out_specs=[
            pl.BlockSpec((nb, tm, GROUP_W), lambda i, j: (j, i, 0)),
            pl.BlockSpec((tm, GATE_W), lambda i, j: (i, 0)),
        ],
        out_shape=[
            jax.ShapeDtypeStruct((N_COLBLK, T, GROUP_W), BF16),
            jax.ShapeDtypeStruct((T, GATE_W), F32),
        ],
        scratch_shapes=[pltpu.VMEM((tm, D_MODEL), BF16)],
        compiler_params=_params(("parallel", "arbitrary")),
        name="norm_proj",
    )(x2, g, w_main, w_gate)


def _fourier_tables(S):
    n2 = HEAD_DIM
    n1 = S // n2
    i1 = jnp.arange(n1, dtype=jnp.int32)
    a1 = (2.0 * math.pi / n1) * ((i1[:, None] * i1[None, :]) % n1).astype(F32)
    f1 = jnp.concatenate([jnp.cos(a1), -jnp.sin(a1)], axis=0).astype(BF16)
    i2 = jnp.arange(n2, dtype=jnp.int32)
    p = i1[:, None, None] + n1 * i2[None, :, None]
    k = (p * i2[None, None, :]) % S
    a2 = (2.0 * math.pi / S) * k.astype(F32)
    c2, s2 = jnp.cos(a2), jnp.sin(a2)
    g = jnp.concatenate([jnp.concatenate([c2, s2], axis=2),
                         jnp.concatenate([-s2, c2], axis=2)], axis=1).astype(BF16)
    ac = (2.0 * math.pi / HEAD_DIM) * ((i2[:, None] * i2[None, :]) % HEAD_DIM).astype(F32)
    cs = jnp.concatenate([jnp.cos(ac), jnp.sin(ac)], axis=0).astype(BF16)
    return f1, g, cs


def _fourier_stage1_kernel(a_ref, f1_ref, y_ref):
    n1 = a_ref.shape[0]
    res = jnp.dot(f1_ref[...], a_ref[...], preferred_element_type=F32)
    y_ref[0] = res[:n1].astype(BF16)
    y_ref[1] = res[n1:].astype(BF16)


def _fourier_stage2_kernel(y_ref, g_ref, cs_ref, o_ref, *, scale):
    n_p = y_ref.shape[1]
    cs = cs_ref[...]
    for p in range(n_p):
        ycat = jnp.concatenate([y_ref[0, p], y_ref[1, p]], axis=0)
        x = jnp.dot(g_ref[p], ycat, preferred_element_type=F32).astype(BF16)
        for grp in range(4):
            lanes = slice(grp * HEAD_DIM, (grp + 1) * HEAD_DIM)
            xg = jnp.concatenate([x[:HEAD_DIM, lanes], x[HEAD_DIM:, lanes]], axis=1)
            o = jnp.dot(xg, cs, preferred_element_type=F32) * scale
            col = p * GROUP_W + grp * HEAD_DIM
            o_ref[:, col:col + HEAD_DIM] = _head_norm(o).astype(BF16)


def _fourier(u3, B, S, tables):
    f1, g, cs = tables
    n2 = HEAD_DIM
    n1 = S // n2
    T = B * S
    ncol = min(16, n2)
    a_view = u3[BLK_FOURIER].reshape(B, n1, n2 * GROUP_W)
    y = pl.pallas_call(
        _fourier_stage1_kernel,
        grid=(B, n2 // ncol),
        in_specs=[
            pl.BlockSpec((None, n1, ncol * GROUP_W), lambda b, j: (b, 0, j)),
            pl.BlockSpec((2 * n1, n1), lambda b, j: (0, 0)),
        ],
        out_specs=pl.BlockSpec((None, 2, n1, ncol * GROUP_W), lambda b, j: (b, 0, 0, j)),
        out_shape=jax.ShapeDtypeStruct((B, 2, n1, n2 * GROUP_W), BF16),
        compiler_params=_params(("parallel", "parallel")),
        name="fourier_stage1",
    )(a_view, f1)
    y5 = y.reshape(B, 2, n1, n2, GROUP_W)
    n_p = 8
    out = pl.pallas_call(
        functools.partial(_fourier_stage2_kernel, scale=1.0 / math.sqrt(S * HEAD_DIM)),
        grid=(B, n1 // n_p),
        in_specs=[
            pl.BlockSpec((None, 2, n_p, n2, GROUP_W), lambda b, j: (b, 0, j, 0, 0)),
            pl.BlockSpec((n_p, 2 * n2, 2 * n2), lambda b, j: (j, 0, 0)),
            pl.BlockSpec((2 * HEAD_DIM, HEAD_DIM), lambda b, j: (0, 0)),
        ],
        out_specs=pl.BlockSpec((None, n2, n_p * GROUP_W), lambda b, j: (b, 0, j)),
        out_shape=jax.ShapeDtypeStruct((B, n2, n1 * GROUP_W), BF16),
        compiler_params=_params(("parallel", "parallel")),
        name="fourier_stage2",
    )(y5, g, cs)
    return out.reshape(T, GROUP_W)


def _log_sigmoid(x):
    return jnp.minimum(x, 0.0) - jnp.log1p(jnp.exp(-jnp.abs(x)))


def _mlstm_chunk(d, q, k, v, gi, gf, ib, fb, tri, mask, c_ref, n_ref, m_ref, out_ref, rows):
    li = gi + ib
    lf = _log_sigmoid(gf + fb)
    b = jnp.dot(tri, lf, precision=lax.Precision.HIGHEST, preferred_element_type=F32)
    g = jnp.sum(lf, axis=0, keepdims=True)
    w_end = g - b + li
    m_loc = jnp.max(w_end, axis=0, keepdims=True)
    e = jnp.exp(w_end - m_loc)
    a_t = (li - b).T
    m_old = m_ref[d:d + 1, :]
    m_new = jnp.maximum(g + m_old, m_loc)
    a_sc = jnp.exp(g + m_old - m_new)
    b_sc = jnp.exp(m_loc - m_new)
    for h in range(MLSTM_HEADS):
        hp = MLSTM_HEADS * d + h
        lanes = slice(h * HEAD_DIM, (h + 1) * HEAD_DIM)
        qh = q[:, lanes]
        vh = v[:, lanes]
        ks = k[:, lanes].astype(F32) * QK_SCALE
        bc = b[:, hp:hp + 1]
        dm = jnp.where(mask, bc + a_t[hp:hp + 1, :], -jnp.inf)
        inter_log = bc + m_old[:, hp:hp + 1]
        m_t = jnp.maximum(inter_log, jnp.max(dm, axis=-1, keepdims=True))
        p = jnp.exp(dm - m_t)
        inter_w = jnp.exp(inter_log - m_t)
        qk = lax.dot_general(qh, ks.astype(BF16), (((1,), (1,)), ((), ())), preferred_element_type=F32)
        a_mat = p * qk
        c_prev = c_ref[hp]
        n_prev = n_ref[hp:hp + 1, :]
        num = (jnp.dot(a_mat.astype(BF16), vh, preferred_element_type=F32)
               + inter_w * jnp.dot(qh, c_prev.astype(BF16), preferred_element_type=F32))
        den = (jnp.sum(a_mat, axis=-1, keepdims=True)
               + inter_w * jnp.sum(qh.astype(F32) * n_prev, axis=-1, keepdims=True))
        hout = num / jnp.maximum(jnp.abs(den), jnp.exp(-m_t))
        out_ref[rows, lanes] = hout.astype(out_ref.dtype)
        ke = ks * e[:, hp:hp + 1]
        c_loc = jnp.dot(ke.T.astype(BF16), vh, preferred_element_type=F32)
        n_loc = jnp.sum(ke, axis=0, keepdims=True)
        ah = a_sc[:, hp:hp + 1]
        bh = b_sc[:, hp:hp + 1]
        c_ref[hp] = ah * c_prev + bh * c_loc
        n_ref[hp:hp + 1, :] = ah * n_prev + bh * n_loc
    m_ref[d:d + 1, :] = m_new


def _mlstm_kernel(qf_ref, kf_ref, vf_ref, gf_ref, qb_ref, kb_ref, vb_ref, gb_ref, ib_ref, fb_ref,
                  hf_ref, hb_ref, c_ref, n_ref, m_ref):
    L = MLSTM_CHUNK
    n_sub = qf_ref.shape[0] // L

    @pl.when(pl.program_id(1) == 0)
    def _():
        c_ref[...] = jnp.zeros_like(c_ref)
        n_ref[...] = jnp.zeros_like(n_ref)
        m_ref[...] = jnp.zeros_like(m_ref)

    row = lax.broadcasted_iota(jnp.int32, (L, L), 0)
    col = lax.broadcasted_iota(jnp.int32, (L, L), 1)
    mask_f = col <= row
    mask_b = col >= row
    tri_f = jnp.where(mask_f, 1.0, 0.0).astype(F32)
    tri_b = jnp.where(mask_b, 1.0, 0.0).astype(F32)
    ib = ib_ref[...]
    fb = fb_ref[...]

    def body(j, carry):
        rf = pl.ds(pl.multiple_of(j * L, L), L)
        rb = pl.ds(pl.multiple_of((n_sub - 1 - j) * L, L), L)
        _mlstm_chunk(0, qf_ref[rf, :], kf_ref[rf, :], vf_ref[rf, :], gf_ref[rf, :HEAD_DIM], gf_ref[rf, HEAD_DIM:],
                     ib, fb, tri_f, mask_f, c_ref, n_ref, m_ref, hf_ref, rf)
        _mlstm_chunk(1, qb_ref[rb, :], kb_ref[rb, :], vb_ref[rb, :], gb_ref[rb, :HEAD_DIM], gb_ref[rb, HEAD_DIM:],
                     ib, fb, tri_b, mask_b, c_ref, n_ref, m_ref, hb_ref, rb)
        return carry

    lax.fori_loop(0, n_sub, body, 0)


def _mlstm(u3, gates, ib, fb, B, S, n_sub=4):
    T = B * S
    rows = n_sub * MLSTM_CHUNK
    nblk = S // rows
    u4 = u3.reshape(N_COLBLK, B, S, GROUP_W)
    g3 = gates.reshape(B, S, GATE_W)

    def fwd(blk):
        return pl.BlockSpec((None, None, rows, GROUP_W), lambda b, c: (blk, b, c, 0))

    def bwd(blk):
        return pl.BlockSpec((None, None, rows, GROUP_W), lambda b, c: (blk, b, nblk - 1 - c, 0))

    hf, hb = pl.pallas_call(
        _mlstm_kernel,
        grid=(B, nblk),
        in_specs=[
            fwd(BLK_MQ), fwd(BLK_MK), fwd(BLK_MV),
            pl.BlockSpec((None, rows, GATE_W), lambda b, c: (b, c, 0)),
            bwd(BLK_MQ), bwd(BLK_MK), bwd(BLK_MV),
            pl.BlockSpec((None, rows, GATE_W), lambda b, c: (b, nblk - 1 - c, 0)),
            pl.BlockSpec((1, HEAD_DIM), lambda b, c: (0, 0)),
            pl.BlockSpec((1, HEAD_DIM), lambda b, c: (0, 0)),
        ],
        out_specs=[
            pl.BlockSpec((None, rows, GROUP_W), lambda b, c: (b, c, 0)),
            pl.BlockSpec((None, rows, GROUP_W), lambda b, c: (b, nblk - 1 - c, 0)),
        ],
        out_shape=[jax.ShapeDtypeStruct((B, S, GROUP_W), BF16)] * 2,
        scratch_shapes=[
            pltpu.VMEM((2 * MLSTM_HEADS, HEAD_DIM, HEAD_DIM), F32),
            pltpu.VMEM((2 * MLSTM_HEADS, HEAD_DIM), F32),
            pltpu.VMEM((8, HEAD_DIM), F32),
        ],
        compiler_params=_params(("parallel", "arbitrary")),
        name="mlstm",
    )(u4, u4, u4, g3, u4, u4, u4, g3, ib, fb)
    return hf.reshape(T, GROUP_W), hb.reshape(T, GROUP_W)


def _rope_tables(S):
    rows = S // GRID_W
    row = jnp.repeat(jnp.arange(rows), GRID_W).astype(F32)
    colp = jnp.tile(jnp.arange(GRID_W), rows).astype(F32)
    inv = ROPE_THETA ** (-jnp.arange(ROPE_FREQS, dtype=F32) / ROPE_FREQS)
    ar = row[:, None] * inv
    ac = colp[:, None] * inv
    ang = jnp.concatenate([ar, ar, ac, ac], axis=1)
    first_half = (jnp.arange(HEAD_DIM) % (2 * ROPE_FREQS)) < ROPE_FREQS
    return jnp.cos(ang), jnp.where(first_half, -1.0, 1.0) * jnp.sin(ang)


def _rope(x, cos, sin_signed, first_half):
    partner = jnp.where(first_half, pltpu.roll(x, HEAD_DIM - ROPE_FREQS, axis=1), pltpu.roll(x, ROPE_FREQS, axis=1))
    return x * cos + partner * sin_signed


def _attn_prep_kernel(aq_ref, akv_ref, cos_ref, sin_ref, qg_ref, kg_ref, q_ref, kt_ref, va_ref):
    cos = cos_ref[...]
    sin = sin_ref[...]
    lane = lax.broadcasted_iota(jnp.int32, cos.shape, 1)
    first_half = (lane % (2 * ROPE_FREQS)) < ROPE_FREQS
    qg = qg_ref[...]
    kg = kg_ref[...]
    for h in range(4):
        lanes = slice(h * HEAD_DIM, (h + 1) * HEAD_DIM)
        qh = _head_norm(aq_ref[:, lanes].astype(F32)) * qg
        q_ref[:, lanes] = (_rope(qh, cos, sin, first_half) * (QK_SCALE * LOG2_E)).astype(BF16)
    for h in range(ATTN_KV_HEADS):
        lanes = slice(h * HEAD_DIM, (h + 1) * HEAD_DIM)
        kh = _head_norm(akv_ref[:, lanes].astype(F32)) * kg
        kt_ref[h] = _rope(kh, cos, sin, first_half).T.astype(BF16)
        vl = slice((ATTN_KV_HEADS + h) * HEAD_DIM, (ATTN_KV_HEADS + h + 1) * HEAD_DIM)
        va_ref[h, :, :HEAD_DIM] = akv_ref[:, vl]
        va_ref[h, :, HEAD_DIM:] = jnp.ones((akv_ref.shape[0], HEAD_DIM), BF16)


def _flash_kernel(q_ref, kt_ref, va_ref, o_ref, m_ref, acc_ref):
    tq = q_ref.shape[0]
    nkc, _, tk = kt_ref.shape
    q2 = jnp.concatenate([q_ref[:, :HEAD_DIM], q_ref[:, HEAD_DIM:]], axis=0)
    m_ref[...] = jnp.full_like(m_ref, -jnp.inf)
    acc_ref[...] = jnp.zeros_like(acc_ref)

    def body(c, carry):
        s = jnp.dot(q2, kt_ref[c], preferred_element_type=F32)
        m_prev = m_ref[...]
        m_new = jnp.maximum(m_prev, jnp.max(s, axis=-1, keepdims=True))
        alpha = jnp.exp2(m_prev - m_new)
        p = jnp.exp2(s - pltpu.repeat(m_new, tk // HEAD_DIM, axis=1))
        vc = va_ref[pl.ds(pl.multiple_of(c * tk, tk), tk), :]
        acc_ref[...] = (pltpu.repeat(alpha, 2, axis=1) * acc_ref[...]
                        + jnp.dot(p.astype(BF16), vc, preferred_element_type=F32))
        m_ref[...] = m_new
        return carry

    lax.fori_loop(0, nkc, body, 0)
    o = acc_ref[:, :HEAD_DIM] / acc_ref[:, HEAD_DIM:]
    o_ref[:, :HEAD_DIM] = o[:tq].astype(o_ref.dtype)
    o_ref[:, HEAD_DIM:] = o[tq:].astype(o_ref.dtype)


def _axial_gqa(u3, q_gain, k_gain, rope, B, S, tq=512, tk=1024):
    T = B * S
    cos, sin = rope
    u4 = u3.reshape(N_COLBLK, B, S, GROUP_W)
    nkc = S // tk
    q_r, kt, va = pl.pallas_call(
        _attn_prep_kernel,
        grid=(B, nkc),
        in_specs=[
            pl.BlockSpec((None, None, tk, GROUP_W), lambda b, i: (BLK_AQ, b, i, 0)),
            pl.BlockSpec((None, None, tk, GROUP_W), lambda b, i: (BLK_AKV, b, i, 0)),
            pl.BlockSpec((tk, HEAD_DIM), lambda b, i: (i, 0)),
            pl.BlockSpec((tk, HEAD_DIM), lambda b, i: (i, 0)),
            pl.BlockSpec((1, HEAD_DIM), lambda b, i: (0, 0)),
            pl.BlockSpec((1, HEAD_DIM), lambda b, i: (0, 0)),
        ],
        out_specs=[
            pl.BlockSpec((None, tk, GROUP_W), lambda b, i: (b, i, 0)),
            pl.BlockSpec((None, ATTN_KV_HEADS, None, HEAD_DIM, tk), lambda b, i: (b, 0, i, 0, 0)),
            pl.BlockSpec((None, ATTN_KV_HEADS, tk, 2 * HEAD_DIM), lambda b, i: (b, 0, i, 0)),
        ],
        out_shape=[
            jax.ShapeDtypeStruct((B, S, GROUP_W), BF16),
            jax.ShapeDtypeStruct((B, ATTN_KV_HEADS, nkc, HEAD_DIM, tk), BF16),
            jax.ShapeDtypeStruct((B, ATTN_KV_HEADS, S, 2 * HEAD_DIM), BF16),
        ],
        compiler_params=_params(("parallel", "parallel")),
        name="attn_prep",
    )(u4, u4, cos, sin, q_gain, k_gain)
    out = pl.pallas_call(
        _flash_kernel,
        grid=(B, ATTN_KV_HEADS, S // tq),
        in_specs=[
            pl.BlockSpec((None, tq, 2 * HEAD_DIM), lambda b, k, i: (b, i, k)),
            pl.BlockSpec((None, None, nkc, HEAD_DIM, tk), lambda b, k, i: (b, k, 0, 0, 0)),
            pl.BlockSpec((None, None, S, 2 * HEAD_DIM), lambda b, k, i: (b, k, 0, 0)),
        ],
        out_specs=pl.BlockSpec((None, tq, 2 * HEAD_DIM), lambda b, k, i: (b, i, k)),
        out_shape=jax.ShapeDtypeStruct((B, S, GROUP_W), BF16),
        scratch_shapes=[
            pltpu.VMEM((2 * tq, HEAD_DIM), F32),
            pltpu.VMEM((2 * tq, 2 * HEAD_DIM), F32),
        ],
        compiler_params=_params(("parallel", "parallel", "arbitrary")),
        name="flash_gqa",
    )(q_r, kt, va)
    return out.reshape(T, GROUP_W)


def _combine_kernel(x_ref, ya_ref, cb_ref, cc_ref, cx_ref, ccp_ref, cxp_ref, ccn_ref, cxn_ref, mo_ref,
                    hf_ref, hb_ref, yd_ref, cw_ref, cbias_ref, gain_ref, w_ref, o_ref, y_ref, *, tiles_per_seq):
    tm = x_ref.shape[0]
    i = pl.program_id(0)
    halo = ccp_ref.shape[0]
    at_start = (i % tiles_per_seq) == 0
    at_end = (i % tiles_per_seq) == tiles_per_seq - 1
    gain = gain_ref[...]

    z = cc_ref[...].astype(F32) * cx_ref[...].astype(F32)
    z_before = ccp_ref[halo - 1:halo, :].astype(F32) * cxp_ref[halo - 1:halo, :].astype(F32)
    z_after = ccn_ref[0:1, :].astype(F32) * cxn_ref[0:1, :].astype(F32)
    z_before = jnp.where(at_start, 0.0, z_before)
    z_after = jnp.where(at_end, 0.0, z_after)
    rid = lax.broadcasted_iota(jnp.int32, z.shape, 0)
    z_m1 = jnp.where(rid == 0, z_before, pltpu.roll(z, 1, axis=0))
    z_p1 = jnp.where(rid == tm - 1, z_after, pltpu.roll(z, tm - 1, axis=0))
    conv = z_m1 * cw_ref[0:1, :] + z * cw_ref[1:2, :] + z_p1 * cw_ref[2:3, :] + cbias_ref[...]
    yb = cb_ref[...].astype(F32) * conv

    for grp in range(4):
        lanes = slice(grp * HEAD_DIM, (grp + 1) * HEAD_DIM)
        y_ref[:, lanes] = (ya_ref[:, lanes].astype(F32) * gain[:, lanes]).astype(BF16)
        gl = slice(GROUP_W + grp * HEAD_DIM, GROUP_W + (grp + 1) * HEAD_DIM)
        y_ref[:, gl] = (_head_norm(yb[:, lanes]) * gain[:, gl]).astype(BF16)
        gl = slice(2 * GROUP_W + grp * HEAD_DIM, 2 * GROUP_W + (grp + 1) * HEAD_DIM)
        yc = _head_norm(hf_ref[:, lanes].astype(F32) + hb_ref[:, lanes].astype(F32))
        y_ref[:, gl] = (jax.nn.sigmoid(mo_ref[:, lanes].astype(F32)) * yc * gain[:, gl]).astype(BF16)
        gl = slice(3 * GROUP_W + grp * HEAD_DIM, 3 * GROUP_W + (grp + 1) * HEAD_DIM)
        y_ref[:, gl] = (_head_norm(yd_ref[:, lanes].astype(F32)) * gain[:, gl]).astype(BF16)

    o_ref[...] = x_ref[...] + jnp.dot(y_ref[...], w_ref[...], preferred_element_type=F32)


def _combine_out(x2, ya, u3, hf, hb, yd, conv_w, conv_b, out_gain, w_out, S, tm=256, halo=16):
    T = x2.shape[0]
    hb_per_tile = tm // halo
    n_halo = T // halo

    def u_blk(blk):
        return pl.BlockSpec((None, tm, GROUP_W), lambda i: (blk, i, 0))

    def u_prev(blk):
        return pl.BlockSpec((None, halo, GROUP_W), lambda i: (blk, jnp.maximum(i * hb_per_tile - 1, 0), 0))

    def u_next(blk):
        return pl.BlockSpec((None, halo, GROUP_W), lambda i: (blk, jnp.minimum((i + 1) * hb_per_tile, n_halo - 1), 0))

    tok = pl.BlockSpec((tm, GROUP_W), lambda i: (i, 0))
    return pl.pallas_call(
        functools.partial(_combine_kernel, tiles_per_seq=S // tm),
        grid=(T // tm,),
        in_specs=[
            pl.BlockSpec((tm, D_MODEL), lambda i: (i, 0)),
            tok,
            u_blk(BLK_CB), u_blk(BLK_CC), u_blk(BLK_CX),
            u_prev(BLK_CC), u_prev(BLK_CX), u_next(BLK_CC), u_next(BLK_CX),
            u_blk(BLK_MO),
            tok, tok, tok,
            pl.BlockSpec((8, GROUP_W), lambda i: (0, 0)),
            pl.BlockSpec((1, GROUP_W), lambda i: (0, 0)),
            pl.BlockSpec((1, D_MODEL), lambda i: (0, 0)),
            pl.BlockSpec((D_MODEL, D_MODEL), lambda i: (0, 0)),
        ],
        out_specs=pl.BlockSpec((tm, D_MODEL), lambda i: (i, 0)),
        out_shape=jax.ShapeDtypeStruct((T, D_MODEL), F32),
        scratch_shapes=[pltpu.VMEM((tm, D_MODEL), BF16)],
        compiler_params=_params(("parallel",)),
        name="combine_out",
    )(x2, ya, u3, u3, u3, u3, u3, u3, u3, u3, hf, hb, yd, conv_w, conv_b, out_gain, w_out)


def _mem_kv_kernel(mem_ref, g_ref, w_ref, kg_ref, kt_ref, v_ref, h_ref):
    _norm_rows_to(mem_ref, g_ref, h_ref)
    kv = jnp.dot(h_ref[...], w_ref[...], preferred_element_type=F32)
    kg = kg_ref[...]
    for h in range(CA_HEADS):
        lanes = slice(h * HEAD_DIM, (h + 1) * HEAD_DIM)
        kt_ref[h] = (_head_norm(kv[:, lanes]) * kg).T.astype(BF16)
    v_ref[...] = kv[:, CA_HEADS * HEAD_DIM:].astype(BF16)


def _mem_kv(mem, g_mem, w_kv, k_gain):
    B = mem.shape[0]
    return pl.pallas_call(
        _mem_kv_kernel,
        grid=(B,),
        in_specs=[
            pl.BlockSpec((None, N_MEM, D_MODEL), lambda b: (b, 0, 0)),
            pl.BlockSpec((1, D_MODEL), lambda b: (0, 0)),
            pl.BlockSpec((D_MODEL, 2 * GROUP_W), lambda b: (0, 0)),
            pl.BlockSpec((1, HEAD_DIM), lambda b: (0, 0)),
        ],
        out_specs=[
            pl.BlockSpec((None, CA_HEADS, HEAD_DIM, N_MEM), lambda b: (b, 0, 0, 0)),
            pl.BlockSpec((None, N_MEM, GROUP_W), lambda b: (b, 0, 0)),
        ],
        out_shape=[
            jax.ShapeDtypeStruct((B, CA_HEADS, HEAD_DIM, N_MEM), BF16),
            jax.ShapeDtypeStruct((B, N_MEM, GROUP_W), BF16),
        ],
        scratch_shapes=[pltpu.VMEM((N_MEM, D_MODEL), BF16)],
        compiler_params=_params(("parallel",)),
        name="mem_kv",
    )(mem, g_mem, w_kv, k_gain)


def _cross_attn_kernel(x_ref, g_ref, wq_ref, qg_ref, kt_ref, v_ref, wo_ref, o_ref, h_ref, oc_ref):
    _norm_rows_to(x_ref, g_ref, h_ref)
    q = jnp.dot(h_ref[...], wq_ref[...], preferred_element_type=F32)
    qg = qg_ref[...]
    for h in range(CA_HEADS):
        lanes = slice(h * HEAD_DIM, (h + 1) * HEAD_DIM)
        qh = (_head_norm(q[:, lanes]) * (qg * QK_SCALE)).astype(BF16)
        s = jnp.dot(qh, kt_ref[h], preferred_element_type=F32)
        e = jnp.exp(s - jnp.max(s, axis=-1, keepdims=True))
        o = jnp.dot(e.astype(BF16), v_ref[:, lanes], preferred_element_type=F32)
        oc_ref[:, lanes] = (o * pl.reciprocal(jnp.sum(e, axis=-1, keepdims=True), approx=False)).astype(BF16)
    o_ref[...] = x_ref[...] + jnp.dot(oc_ref[...], wo_ref[...], preferred_element_type=F32)


def _cross_attn(x2, g_ca, w_q, q_gain, kt, v, w_o, B, S, tm=512):
    T = x2.shape[0]
    x3 = x2.reshape(B, S, D_MODEL)
    out = pl.pallas_call(
        _cross_attn_kernel,
        grid=(B, S // tm),
        in_specs=[
            pl.BlockSpec((None, tm, D_MODEL), lambda b, i: (b, i, 0)),
            pl.BlockSpec((1, D_MODEL), lambda b, i: (0, 0)),
            pl.BlockSpec((D_MODEL, GROUP_W), lambda b, i: (0, 0)),
            pl.BlockSpec((1, HEAD_DIM), lambda b, i: (0, 0)),
            pl.BlockSpec((None, CA_HEADS, HEAD_DIM, N_MEM), lambda b, i: (b, 0, 0, 0)),
            pl.BlockSpec((None, N_MEM, GROUP_W), lambda b, i: (b, 0, 0)),
            pl.BlockSpec((GROUP_W, D_MODEL), lambda b, i: (0, 0)),
        ],
        out_specs=pl.BlockSpec((None, tm, D_MODEL), lambda b, i: (b, i, 0)),
        out_shape=jax.ShapeDtypeStruct((B, S, D_MODEL), F32),
        scratch_shapes=[pltpu.VMEM((tm, D_MODEL), BF16), pltpu.VMEM((tm, GROUP_W), BF16)],
        compiler_params=_params(("parallel", "parallel")),
        name="cross_attn",
    )(x3, g_ca, w_q, q_gain, kt, v, w_o)
    return out.reshape(T, D_MODEL)


def _mlp_kernel(x_ref, g_ref, wu_ref, wd_ref, o_ref, h_ref):
    @pl.when(pl.program_id(1) == 0)
    def _():
        _norm_rows_to(x_ref, g_ref, h_ref)
        o_ref[...] = x_ref[...]

    up = jnp.dot(h_ref[...], wu_ref[...], preferred_element_type=F32)
    act = jnp.square(jnp.maximum(up, 0.0)).astype(BF16)
    o_ref[...] += jnp.dot(act, wd_ref[...], preferred_element_type=F32)


def _mlp(x2, g, w_up, w_down, tm=512, tf=1024):
    T = x2.shape[0]
    return pl.pallas_call(
        _mlp_kernel,
        grid=(T // tm, D_FF // tf),
        in_specs=[
            pl.BlockSpec((tm, D_MODEL), lambda i, j: (i, 0)),
            pl.BlockSpec((1, D_MODEL), lambda i, j: (0, 0)),
            pl.BlockSpec((D_MODEL, tf), lambda i, j: (0, j)),
            pl.BlockSpec((tf, D_MODEL), lambda i, j: (j, 0)),
        ],
        out_specs=pl.BlockSpec((tm, D_MODEL), lambda i, j: (i, 0)),
        out_shape=jax.ShapeDtypeStruct((T, D_MODEL), F32),
        scratch_shapes=[pltpu.VMEM((tm, D_MODEL), BF16)],
        compiler_params=_params(("parallel", "arbitrary")),
        name="mlp",
    )(x2, g, w_up, w_down)


def _prep_weights(w_in, conv_w, i_bias, f_bias, w_out, w_ca_q, w_ca_kv, w_ca_o, w_up, w_down):
    n_main = 8 * GROUP_W
    w_main = jnp.concatenate([w_in[:, :, :n_main], w_in[:, :, n_main + 16:]], axis=2).astype(BF16)
    pad = jnp.zeros((DEPTH, D_MODEL, HEAD_DIM - 8), F32)
    w_gate = jnp.concatenate([w_in[:, :, n_main:n_main + 8], pad, w_in[:, :, n_main + 8:n_main + 16], pad],
                             axis=2).astype(BF16)
    lane_pad = jnp.zeros((DEPTH, 1, HEAD_DIM - 8), F32)
    ib = jnp.concatenate([i_bias.reshape(DEPTH, 1, 8), lane_pad], axis=2)
    fb = jnp.concatenate([f_bias.reshape(DEPTH, 1, 8), lane_pad], axis=2)
    cw = jnp.concatenate([conv_w, jnp.zeros((DEPTH, 5, GROUP_W), F32)], axis=1)
    return dict(w_main=w_main, w_gate=w_gate, ib=ib, fb=fb, cw=cw, w_out=w_out.astype(BF16),
                w_ca_q=w_ca_q.astype(BF16), w_ca_kv=w_ca_kv.astype(BF16), w_ca_o=w_ca_o.astype(BF16),
                w_up=w_up.astype(BF16), w_down=w_down.astype(BF16))


def _trunk(x, mem, pw, g_mix, conv_b, attn_q_norm, attn_k_norm, out_gain, g_ca, g_mem, ca_q_norm, ca_k_norm, g_mlp):
    B, S, _ = x.shape
    T = B * S
    x2 = x.reshape(T, D_MODEL)
    rope = _rope_tables(S)
    ftab = _fourier_tables(S)
    for l in range(DEPTH):
        u3, gates = _norm_proj(x2, g_mix[l][None], pw["w_main"][l], pw["w_gate"][l])
        ya = _fourier(u3, B, S, ftab)
        hf, hb = _mlstm(u3, gates, pw["ib"][l], pw["fb"][l], B, S)
        yd = _axial_gqa(u3, attn_q_norm[l][None], attn_k_norm[l][None], rope, B, S)
        x2 = _combine_out(x2, ya, u3, hf, hb, yd, pw["cw"][l], conv_b[l][None], out_gain[l][None], pw["w_out"][l], S)
        kt, v = _mem_kv(mem, g_mem[l][None], pw["w_ca_kv"][l], ca_k_norm[l][None])
        x2 = _cross_attn(x2, g_ca[l][None], pw["w_ca_q"][l], ca_q_norm[l][None], kt, v, pw["w_ca_o"][l], B, S)
        x2 = _mlp(x2, g_mlp[l][None], pw["w_up"][l], pw["w_down"][l])
    return x2.reshape(B, S, D_MODEL)


def kernel(x_prompt, x_sample, mem_prompt, mem_sample, g_mix, w_in, conv_w, conv_b, i_bias, f_bias, attn_q_norm,
           attn_k_norm, out_gain, w_out, g_ca, g_mem, w_ca_q, w_ca_kv, ca_q_norm, ca_k_norm, w_ca_o, g_mlp, w_up,
           w_down):
    pw = _prep_weights(w_in, conv_w, i_bias, f_bias, w_out, w_ca_q, w_ca_kv, w_ca_o, w_up, w_down)
    args = (pw, g_mix, conv_b, attn_q_norm, attn_k_norm, out_gain, g_ca, g_mem, ca_q_norm, ca_k_norm, g_mlp)
    y_prompt = _trunk(x_prompt, mem_prompt, *args)
    y_sample = _trunk(x_sample, mem_sample, *args)
    return (y_prompt, y_sample)
```

```python
import functools
import math

import jax
import jax.numpy as jnp
from jax import lax
from jax.experimental import pallas as pl
from jax.experimental.pallas import tpu as pltpu

F32 = jnp.float32
BF16 = jnp.bfloat16

D_MODEL = 2048
DEPTH = 4
HEAD_DIM = 128
GROUP_W = 4 * HEAD_DIM
N_COLBLK = 10
GATE_W = 2 * HEAD_DIM
MLSTM_HEADS = 4
MLSTM_CHUNK = 128
ATTN_KV_HEADS = 2
GRID_W = 64
ROPE_THETA = 10000.0
ROPE_FREQS = HEAD_DIM // 4
CA_HEADS = 4
N_MEM = 256
D_FF = 4 * D_MODEL
RMS_EPS = 1e-6
QK_SCALE = HEAD_DIM ** -0.5
LOG2_E = math.log2(math.e)

BLK_FOURIER, BLK_CB, BLK_CC, BLK_CX, BLK_MQ, BLK_MK, BLK_MV, BLK_MO, BLK_AQ, BLK_AKV = range(N_COLBLK)

V7X_VMEM_LIMIT = 56 * 1024 * 1024
NORM_ROWS = 256


def _params(semantics, vmem=V7X_VMEM_LIMIT):
    return pltpu.CompilerParams(dimension_semantics=semantics, vmem_limit_bytes=vmem)


def _head_norm(y):
    return y * lax.rsqrt(jnp.mean(y * y, axis=-1, keepdims=True) + RMS_EPS)


def _norm_rows_to(x_ref, g_ref, h_ref):
    tm = x_ref.shape[0]
    g = g_ref[...]

    def body(r, carry):
        rows = pl.ds(pl.multiple_of(r * NORM_ROWS, NORM_ROWS), NORM_ROWS)
        xf = x_ref[rows, :]
        ms = jnp.mean(xf * xf, axis=-1, keepdims=True)
        h_ref[rows, :] = (xf * lax.rsqrt(ms + RMS_EPS) * g).astype(BF16)
        return carry

    lax.fori_loop(0, tm // NORM_ROWS, body, 0)


def _norm_proj_kernel(x_ref, g_ref, w_ref, wg_ref, u_ref, gate_ref, h_ref):
    @pl.when(pl.program_id(1) == 0)
    def _():
        _norm_rows_to(x_ref, g_ref, h_ref)
        gate_ref[...] = jnp.dot(h_ref[...], wg_ref[...], preferred_element_type=F32)

    res = jnp.dot(h_ref[...], w_ref[...], preferred_element_type=F32)
    for k in range(u_ref.shape[0]):
        u_ref[k] = res[:, k * GROUP_W:(k + 1) * GROUP_W].astype(BF16)


def _norm_proj(x2, g, w_main, w_gate, tm=1024, nb=2):
    T = x2.shape[0]
    tn = nb * GROUP_W
    return pl.pallas_call(
        _norm_proj_kernel,
        grid=(T // tm, N_COLBLK // nb),
        in_specs=[
            pl.BlockSpec((tm, D_MODEL), lambda i, j: (i, 0)),
            pl.BlockSpec((1, D_MODEL), lambda i, j: (0, 0)),
            pl.BlockSpec((D_MODEL, tn), lambda i, j: (0, j)),
            pl.BlockSpec((D_MODEL, GATE_W), lambda i, j: (0, 0)),
        ],
        out_specs=[
            pl.BlockSpec((nb, tm, GROUP_W), lambda i, j: (j, i, 0)),
            pl.BlockSpec((tm, GATE_W), lambda i, j: (i, 0)),
        ],
        out_shape=[
            jax.ShapeDtypeStruct((N_COLBLK, T, GROUP_W), BF16),
            jax.ShapeDtypeStruct((T, GATE_W), F32),
        ],
        scratch_shapes=[pltpu.VMEM((tm, D_MODEL), BF16)],
        compiler_params=_params(("parallel", "arbitrary")),
        name="norm_proj",
    )(x2, g, w_main, w_gate)


def _fourier_tables(S):
    n2 = HEAD_DIM
    n1 = S // n2
    i1 = jnp.arange(n1, dtype=jnp.int32)
    a1 = (2.0 * math.pi / n1) * ((i1[:, None] * i1[None, :]) % n1).astype(F32)
    f1 = jnp.concatenate([jnp.cos(a1), -jnp.sin(a1)], axis=0).astype(BF16)
    i2 = jnp.arange(n2, dtype=jnp.int32)
    p = i1[:, None, None] + n1 * i2[None, :, None]
    k = (p * i2[None, None, :]) % S
    a2 = (2.0 * math.pi / S) * k.astype(F32)
    c2, s2 = jnp.cos(a2), jnp.sin(a2)
    g = jnp.concatenate([jnp.concatenate([c2, s2], axis=2),
                         jnp.concatenate([-s2, c2], axis=2)], axis=1).astype(BF16)
    ac = (2.0 * math.pi / HEAD_DIM) * ((i2[:, None] * i2[None, :]) % HEAD_DIM).astype(F32)
    cs = jnp.concatenate([jnp.cos(ac), jnp.sin(ac)], axis=0).astype(BF16)
    return f1, g, cs


def _fourier_stage1_kernel(a_ref, f1_ref, y_ref):
    n1 = a_ref.shape[0]
    res = jnp.dot(f1_ref[...], a_ref[...], preferred_element_type=F32)
    y_ref[0] = res[:n1].astype(BF16)
    y_ref[1] = res[n1:].astype(BF16)


def _fourier_stage2_kernel(y_ref, g_ref, cs_ref, o_ref, *, scale):
    n_p = y_ref.shape[1]
    cs = cs_ref[...]
    for p in range(n_p):
        ycat = jnp.concatenate([y_ref[0, p], y_ref[1, p]], axis=0)
        x = jnp.dot(g_ref[p], ycat, preferred_element_type=F32).astype(BF16)
        for grp in range(4):
            lanes = slice(grp * HEAD_DIM, (grp + 1) * HEAD_DIM)
            xg = jnp.concatenate([x[:HEAD_DIM, lanes], x[HEAD_DIM:, lanes]], axis=1)
            o = jnp.dot(xg, cs, preferred_element_type=F32) * scale
            col = p * GROUP_W + grp * HEAD_DIM
            o_ref[:, col:col + HEAD_DIM] = _head_norm(o).astype(BF16)


def _fourier(u3, B, S, tables):
    f1, g, cs = tables
    n2 = HEAD_DIM
    n1 = S // n2
    T = B * S
    ncol = min(16, n2)
    a_view = u3[BLK_FOURIER].reshape(B, n1, n2 * GROUP_W)
    y = pl.pallas_call(
        _fourier_stage1_kernel,
        grid=(B, n2 // ncol),
        in_specs=[
            pl.BlockSpec((None, n1, ncol * GROUP_W), lambda b, j: (b, 0, j)),
            pl.BlockSpec((2 * n1, n1), lambda b, j: (0, 0)),
        ],
        out_specs=pl.BlockSpec((None, 2, n1, ncol * GROUP_W), lambda b, j: (b, 0, 0, j)),
        out_shape=jax.ShapeDtypeStruct((B, 2, n1, n2 * GROUP_W), BF16),
        compiler_params=_params(("parallel", "parallel")),
        name="fourier_stage1",
    )(a_view, f1)
    y5 = y.reshape(B, 2, n1, n2, GROUP_W)
    n_p = 8
    out = pl.pallas_call(
        functools.partial(_fourier_stage2_kernel, scale=1.0 / math.sqrt(S * HEAD_DIM)),
        grid=(B, n1 // n_p),
        in_specs=[
            pl.BlockSpec((None, 2, n_p, n2, GROUP_W), lambda b, j: (b, 0, j, 0, 0)),
            pl.BlockSpec((n_p, 2 * n2, 2 * n2), lambda b, j: (j, 0, 0)),
            pl.BlockSpec((2 * HEAD_DIM, HEAD_DIM), lambda b, j: (0, 0)),
        ],
        out_specs=pl.BlockSpec((None, n2, n_p * GROUP_W), lambda b, j: (b, 0, j)),
        out_shape=jax.ShapeDtypeStruct((B, n2, n1 * GROUP_W), BF16),
        compiler_params=_params(("parallel", "parallel")),
        name="fourier_stage2",
    )(y5, g, cs)
    return out.reshape(T, GROUP_W)


def _log_sigmoid(x):
    return jnp.minimum(x, 0.0) - jnp.log1p(jnp.exp(-jnp.abs(x)))


def _mlstm_chunk(d, q, k, v, gi, gf, ib, fb, tri, mask, c_ref, n_ref, m_ref, out_ref, rows):
    li = gi + ib
    lf = _log_sigmoid(gf + fb)
    b = jnp.dot(tri, lf, precision=lax.Precision.HIGHEST, preferred_element_type=F32)
    g = jnp.sum(lf, axis=0, keepdims=True)
    w_end = g - b + li
    m_loc = jnp.max(w_end, axis=0, keepdims=True)
    e = jnp.exp(w_end - m_loc)
    a_t = (li - b).T
    m_old = m_ref[d:d + 1, :]
    m_new = jnp.maximum(g + m_old, m_loc)
    a_sc = jnp.exp(g + m_old - m_new)
    b_sc = jnp.exp(m_loc - m_new)
    for h in range(MLSTM_HEADS):
        hp = MLSTM_HEADS * d + h
        lanes = slice(h * HEAD_DIM, (h + 1) * HEAD_DIM)
        qh = q[:, lanes]
        vh = v[:, lanes]
        ks = k[:, lanes].astype(F32) * QK_SCALE
        bc = b[:, hp:hp + 1]
        dm = jnp.where(mask, bc + a_t[hp:hp + 1, :], -jnp.inf)
        inter_log = bc + m_old[:, hp:hp + 1]
        m_t = jnp.maximum(inter_log, jnp.max(dm, axis=-1, keepdims=True))
        p = jnp.exp(dm - m_t)
        inter_w = jnp.exp(inter_log - m_t)
        qk = lax.dot_general(qh, ks.astype(BF16), (((1,), (1,)), ((), ())), preferred_element_type=F32)
        a_mat = p * qk
        c_prev = c_ref[hp]
        n_prev = n_ref[hp:hp + 1, :]
        num = (jnp.dot(a_mat.astype(BF16), vh, preferred_element_type=F32)
               + inter_w * jnp.dot(qh, c_prev.astype(BF16), preferred_element_type=F32))
        den = (jnp.sum(a_mat, axis=-1, keepdims=True)
               + inter_w * jnp.sum(qh.astype(F32) * n_prev, axis=-1, keepdims=True))
        hout = num / jnp.maximum(jnp.abs(den), jnp.exp(-m_t))
        out_ref[rows, lanes] = hout.astype(out_ref.dtype)
        ke = ks * e[:, hp:hp + 1]
        c_loc = jnp.dot(ke.T.astype(BF16), vh, preferred_element_type=F32)
        n_loc = jnp.sum(ke, axis=0, keepdims=True)
        ah = a_sc[:, hp:hp + 1]
        bh = b_sc[:, hp:hp + 1]
        c_ref[hp] = ah * c_prev + bh * c_loc
        n_ref[hp:hp + 1, :] = ah * n_prev + bh * n_loc
    m_ref[d:d + 1, :] = m_new


def _mlstm_kernel(qf_ref, kf_ref, vf_ref, gf_ref, qb_ref, kb_ref, vb_ref, gb_ref, ib_ref, fb_ref,
                  hf_ref, hb_ref, c_ref, n_ref, m_ref):
    L = MLSTM_CHUNK
    n_sub = qf_ref.shape[0] // L

    @pl.when(pl.program_id(1) == 0)
    def _():
        c_ref[...] = jnp.zeros_like(c_ref)
        n_ref[...] = jnp.zeros_like(n_ref)
        m_ref[...] = jnp.zeros_like(m_ref)

    row = lax.broadcasted_iota(jnp.int32, (L, L), 0)
    col = lax.broadcasted_iota(jnp.int32, (L, L), 1)
    mask_f = col <= row
    mask_b = col >= row
    tri_f = jnp.where(mask_f, 1.0, 0.0).astype(F32)
    tri_b = jnp.where(mask_b, 1.0, 0.0).astype(F32)
    ib = ib_ref[...]
    fb = fb_ref[...]

    def body(j, carry):
        rf = pl.ds(pl.multiple_of(j * L, L), L)
        rb = pl.ds(pl.multiple_of((n_sub - 1 - j) * L, L), L)
        _mlstm_chunk(0, qf_ref[rf, :], kf_ref[rf, :], vf_ref[rf, :], gf_ref[rf, :HEAD_DIM], gf_ref[rf, HEAD_DIM:],
                     ib, fb, tri_f, mask_f, c_ref, n_ref, m_ref, hf_ref, rf)
        _mlstm_chunk(1, qb_ref[rb, :], kb_ref[rb, :], vb_ref[rb, :], gb_ref[rb, :HEAD_DIM], gb_ref[rb, HEAD_DIM:],
                     ib, fb, tri_b, mask_b, c_ref, n_ref, m_ref, hb_ref, rb)
        return carry

    lax.fori_loop(0, n_sub, body, 0)


def _mlstm(u3, gates, ib, fb, B, S, n_sub=4):
    T = B * S
    rows = n_sub * MLSTM_CHUNK
    nblk = S // rows
    u4 = u3.reshape(N_COLBLK, B, S, GROUP_W)
    g3 = gates.reshape(B, S, GATE_W)

    def fwd(blk):
        return pl.BlockSpec((None, None, rows, GROUP_W), lambda b, c: (blk, b, c, 0))

    def bwd(blk):
        return pl.BlockSpec((None, None, rows, GROUP_W), lambda b, c: (blk, b, nblk - 1 - c, 0))

    hf, hb = pl.pallas_call(
        _mlstm_kernel,
        grid=(B, nblk),
        in_specs=[
            fwd(BLK_MQ), fwd(BLK_MK), fwd(BLK_MV),
            pl.BlockSpec((None, rows, GATE_W), lambda b, c: (b, c, 0)),
            bwd(BLK_MQ), bwd(BLK_MK), bwd(BLK_MV),
            pl.BlockSpec((None, rows, GATE_W), lambda b, c: (b, nblk - 1 - c, 0)),
            pl.BlockSpec((1, HEAD_DIM), lambda b, c: (0, 0)),
            pl.BlockSpec((1, HEAD_DIM), lambda b, c: (0, 0)),
        ],
        out_specs=[
            pl.BlockSpec((None, rows, GROUP_W), lambda b, c: (b, c, 0)),
            pl.BlockSpec((None, rows, GROUP_W), lambda b, c: (b, nblk - 1 - c, 0)),
        ],
        out_shape=[jax.ShapeDtypeStruct((B, S, GROUP_W), BF16)] * 2,
        scratch_shapes=[
            pltpu.VMEM((2 * MLSTM_HEADS, HEAD_DIM, HEAD_DIM), F32),
            pltpu.VMEM((2 * MLSTM_HEADS, HEAD_DIM), F32),
            pltpu.VMEM((8, HEAD_DIM), F32),
        ],
        compiler_params=_params(("parallel", "arbitrary")),
        name="mlstm",
    )(u4, u4, u4, g3, u4, u4, u4, g3, ib, fb)
    return hf.reshape(T, GROUP_W), hb.reshape(T, GROUP_W)


def _rope_tables(S):
    rows = S // GRID_W
    row = jnp.repeat(jnp.arange(rows), GRID_W).astype(F32)
    colp = jnp.tile(jnp.arange(GRID_W), rows).astype(F32)
    inv = ROPE_THETA ** (-jnp.arange(ROPE_FREQS, dtype=F32) / ROPE_FREQS)
    ar = row[:, None] * inv
    ac = colp[:, None] * inv
    ang = jnp.concatenate([ar, ar, ac, ac], axis=1)
    first_half = (jnp.arange(HEAD_DIM) % (2 * ROPE_FREQS)) < ROPE_FREQS
    return jnp.cos(ang), jnp.where(first_half, -1.0, 1.0) * jnp.sin(ang)


def _rope(x, cos, sin_signed, first_half):
    partner = jnp.where(first_half, pltpu.roll(x, HEAD_DIM - ROPE_FREQS, axis=1), pltpu.roll(x, ROPE_FREQS, axis=1))
    return x * cos + partner * sin_signed


def _attn_prep_kernel(aq_ref, akv_ref, cos_ref, sin_ref, qg_ref, kg_ref, q_ref, kt_ref, va_ref):
    cos = cos_ref[...]
    sin = sin_ref[...]
    lane = lax.broadcasted_iota(jnp.int32, cos.shape, 1)
    first_half = (lane % (2 * ROPE_FREQS)) < ROPE_FREQS
    qg = qg_ref[...]
    kg = kg_ref[...]
    for h in range(4):
        lanes = slice(h * HEAD_DIM, (h + 1) * HEAD_DIM)
        qh = _head_norm(aq_ref[:, lanes].astype(F32)) * qg
        q_ref[:, lanes] = (_rope(qh, cos, sin, first_half) * (QK_SCALE * LOG2_E)).astype(BF16)
    for h in range(ATTN_KV_HEADS):
        lanes = slice(h * HEAD_DIM, (h + 1) * HEAD_DIM)
        kh = _head_norm(akv_ref[:, lanes].astype(F32)) * kg
        kt_ref[h] = _rope(kh, cos, sin, first_half).T.astype(BF16)
        vl = slice((ATTN_KV_HEADS + h) * HEAD_DIM, (ATTN_KV_HEADS + h + 1) * HEAD_DIM)
        va_ref[h, :, :HEAD_DIM] = akv_ref[:, vl]
        va_ref[h, :, HEAD_DIM:] = jnp.ones((akv_ref.shape[0], HEAD_DIM), BF16)


def _flash_kernel(q_ref, kt_ref, va_ref, o_ref, m_ref, acc_ref, p_ref, a_ref):
    tq = q_ref.shape[0]
    nkc, _, tk = kt_ref.shape
    q2 = jnp.concatenate([q_ref[:, :HEAD_DIM], q_ref[:, HEAD_DIM:]], axis=0)
    m_ref[...] = jnp.full_like(m_ref, -jnp.inf)
    acc_ref[...] = jnp.zeros_like(acc_ref)

    def scores(c, slot):
        s = jnp.dot(q2, kt_ref[c], preferred_element_type=F32)
        m_prev = m_ref[...]
        m_new = jnp.maximum(m_prev, jnp.max(s, axis=-1, keepdims=True))
        a_ref[slot] = jnp.exp2(m_prev - m_new)
        p_ref[slot] = jnp.exp2(s - jnp.tile(m_new, (1, tk // HEAD_DIM))).astype(BF16)
        m_ref[...] = m_new

    def accumulate(c, slot):
        vc = va_ref[pl.ds(pl.multiple_of(c * tk, tk), tk), :]
        acc_ref[...] = (jnp.tile(a_ref[slot], (1, 2)) * acc_ref[...]
                        + jnp.dot(p_ref[slot], vc, preferred_element_type=F32))

    scores(0, 0)

    def pair(i, carry):
        c = 2 * i + 1
        scores(c, 1)
        accumulate(c - 1, 0)
        scores(c + 1, 0)
        accumulate(c, 1)
        return carry

    lax.fori_loop(0, (nkc - 2) // 2, pair, 0)
    accumulate(nkc - 2, 0)
    scores(nkc - 1, 1)
    accumulate(nkc - 1, 1)
    o = acc_ref[:, :HEAD_DIM] / acc_ref[:, HEAD_DIM:]
    o_ref[:, :HEAD_DIM] = o[:tq].astype(o_ref.dtype)
    o_ref[:, HEAD_DIM:] = o[tq:].astype(o_ref.dtype)


def _axial_gqa(u3, q_gain, k_gain, rope, B, S, tq=512, tk=1024):
    T = B * S
    cos, sin = rope
    u4 = u3.reshape(N_COLBLK, B, S, GROUP_W)
    nkc = S // tk
    assert nkc >= 2 and nkc % 2 == 0, "flash kernel pipelines kv chunks in pairs"
    q_r, kt, va = pl.pallas_call(
        _attn_prep_kernel,
        grid=(B, nkc),
        in_specs=[
            pl.BlockSpec((None, None, tk, GROUP_W), lambda b, i: (BLK_AQ, b, i, 0)),
            pl.BlockSpec((None, None, tk, GROUP_W), lambda b, i: (BLK_AKV, b, i, 0)),
            pl.BlockSpec((tk, HEAD_DIM), lambda b, i: (i, 0)),
            pl.BlockSpec((tk, HEAD_DIM), lambda b, i: (i, 0)),
            pl.BlockSpec((1, HEAD_DIM), lambda b, i: (0, 0)),
            pl.BlockSpec((1, HEAD_DIM), lambda b, i: (0, 0)),
        ],
        out_specs=[
            pl.BlockSpec((None, tk, GROUP_W), lambda b, i: (b, i, 0)),
            pl.BlockSpec((None, ATTN_KV_HEADS, None, HEAD_DIM, tk), lambda b, i: (b, 0, i, 0, 0)),
            pl.BlockSpec((None, ATTN_KV_HEADS, tk, 2 * HEAD_DIM), lambda b, i: (b, 0, i, 0)),
        ],
        out_shape=[
            jax.ShapeDtypeStruct((B, S, GROUP_W), BF16),
            jax.ShapeDtypeStruct((B, ATTN_KV_HEADS, nkc, HEAD_DIM, tk), BF16),
            jax.ShapeDtypeStruct((B, ATTN_KV_HEADS, S, 2 * HEAD_DIM), BF16),
        ],
        compiler_params=_params(("parallel", "parallel")),
        name="attn_prep",
    )(u4, u4, cos, sin, q_gain, k_gain)
    out = pl.pallas_call(
        _flash_kernel,
        grid=(B, ATTN_KV_HEADS, S // tq),
        in_specs=[
            pl.BlockSpec((None, tq, 2 * HEAD_DIM), lambda b, k, i: (b, i, k)),
            pl.BlockSpec((None, None, nkc, HEAD_DIM, tk), lambda b, k, i: (b, k, 0, 0, 0)),
            pl.BlockSpec((None, None, S, 2 * HEAD_DIM), lambda b, k, i: (b, k, 0, 0)),
        ],
        out_specs=pl.BlockSpec((None, tq, 2 * HEAD_DIM), lambda b, k, i: (b, i, k)),
        out_shape=jax.ShapeDtypeStruct((B, S, GROUP_W), BF16),
        scratch_shapes=[
            pltpu.VMEM((2 * tq, HEAD_DIM), F32),
            pltpu.VMEM((2 * tq, 2 * HEAD_DIM), F32),
            pltpu.VMEM((2, 2 * tq, tk), BF16),
            pltpu.VMEM((2, 2 * tq, HEAD_DIM), F32),
        ],
        compiler_params=_params(("parallel", "parallel", "arbitrary")),
        name="flash_gqa",
    )(q_r, kt, va)
    return out.reshape(T, GROUP_W)


def _combine_kernel(x_ref, ya_ref, cb_ref, cc_ref, cx_ref, ccp_ref, cxp_ref, ccn_ref, cxn_ref, mo_ref,
                    hf_ref, hb_ref, yd_ref, cw_ref, cbias_ref, gain_ref, w_ref, o_ref, y_ref, *, tiles_per_seq):
    tm = x_ref.shape[0]
    i = pl.program_id(0)
    halo = ccp_ref.shape[0]
    at_start = (i % tiles_per_seq) == 0
    at_end = (i % tiles_per_seq) == tiles_per_seq - 1
    gain = gain_ref[...]

    z = cc_ref[...].astype(F32) * cx_ref[...].astype(F32)
    z_before = ccp_ref[halo - 1:halo, :].astype(F32) * cxp_ref[halo - 1:halo, :].astype(F32)
    z_after = ccn_ref[0:1, :].astype(F32) * cxn_ref[0:1, :].astype(F32)
    z_before = jnp.where(at_start, 0.0, z_before)
    z_after = jnp.where(at_end, 0.0, z_after)
    rid = lax.broadcasted_iota(jnp.int32, z.shape, 0)
    z_m1 = jnp.where(rid == 0, z_before, pltpu.roll(z, 1, axis=0))
    z_p1 = jnp.where(rid == tm - 1, z_after, pltpu.roll(z, tm - 1, axis=0))
    conv = z_m1 * cw_ref[0:1, :] + z * cw_ref[1:2, :] + z_p1 * cw_ref[2:3, :] + cbias_ref[...]
    yb = cb_ref[...].astype(F32) * conv

    for grp in range(4):
        lanes = slice(grp * HEAD_DIM, (grp + 1) * HEAD_DIM)
        y_ref[:, lanes] = (ya_ref[:, lanes].astype(F32) * gain[:, lanes]).astype(BF16)
        gl = slice(GROUP_W + grp * HEAD_DIM, GROUP_W + (grp + 1) * HEAD_DIM)
        y_ref[:, gl] = (_head_norm(yb[:, lanes]) * gain[:, gl]).astype(BF16)
        gl = slice(2 * GROUP_W + grp * HEAD_DIM, 2 * GROUP_W + (grp + 1) * HEAD_DIM)
        yc = _head_norm(hf_ref[:, lanes].astype(F32) + hb_ref[:, lanes].astype(F32))
        y_ref[:, gl] = (jax.nn.sigmoid(mo_ref[:, lanes].astype(F32)) * yc * gain[:, gl]).astype(BF16)
        gl = slice(3 * GROUP_W + grp * HEAD_DIM, 3 * GROUP_W + (grp + 1) * HEAD_DIM)
        y_ref[:, gl] = (_head_norm(yd_ref[:, lanes].astype(F32)) * gain[:, gl]).astype(BF16)

    o_ref[...] = x_ref[...] + jnp.dot(y_ref[...], w_ref[...], preferred_element_type=F32)


def _combine_out(x2, ya, u3, hf, hb, yd, conv_w, conv_b, out_gain, w_out, S, tm=256, halo=16):
    T = x2.shape[0]
    hb_per_tile = tm // halo
    n_halo = T // halo

    def u_blk(blk):
        return pl.BlockSpec((None, tm, GROUP_W), lambda i: (blk, i, 0))

    def u_prev(blk):
        return pl.BlockSpec((None, halo, GROUP_W), lambda i: (blk, jnp.maximum(i * hb_per_tile - 1, 0), 0))

    def u_next(blk):
        return pl.BlockSpec((None, halo, GROUP_W), lambda i: (blk, jnp.minimum((i + 1) * hb_per_tile, n_halo - 1), 0))

    tok = pl.BlockSpec((tm, GROUP_W), lambda i: (i, 0))
    return pl.pallas_call(
        functools.partial(_combine_kernel, tiles_per_seq=S // tm),
        grid=(T // tm,),
        in_specs=[
            pl.BlockSpec((tm, D_MODEL), lambda i: (i, 0)),
            tok,
            u_blk(BLK_CB), u_blk(BLK_CC), u_blk(BLK_CX),
            u_prev(BLK_CC), u_prev(BLK_CX), u_next(BLK_CC), u_next(BLK_CX),
            u_blk(BLK_MO),
            tok, tok, tok,
            pl.BlockSpec((8, GROUP_W), lambda i: (0, 0)),
            pl.BlockSpec((1, GROUP_W), lambda i: (0, 0)),
            pl.BlockSpec((1, D_MODEL), lambda i: (0, 0)),
            pl.BlockSpec((D_MODEL, D_MODEL), lambda i: (0, 0)),
        ],
        out_specs=pl.BlockSpec((tm, D_MODEL), lambda i: (i, 0)),
        out_shape=jax.ShapeDtypeStruct((T, D_MODEL), F32),
        scratch_shapes=[pltpu.VMEM((tm, D_MODEL), BF16)],
        compiler_params=_params(("parallel",)),
        name="combine_out",
    )(x2, ya, u3, u3, u3, u3, u3, u3, u3, u3, hf, hb, yd, conv_w, conv_b, out_gain, w_out)


def _mem_kv_kernel(mem_ref, g_ref, w_ref, kg_ref, kt_ref, v_ref, h_ref):
    _norm_rows_to(mem_ref, g_ref, h_ref)
    kv = jnp.dot(h_ref[...], w_ref[...], preferred_element_type=F32)
    kg = kg_ref[...]
    for h in range(CA_HEADS):
        lanes = slice(h * HEAD_DIM, (h + 1) * HEAD_DIM)
        kt_ref[h] = (_head_norm(kv[:, lanes]) * kg).T.astype(BF16)
    v_ref[...] = kv[:, CA_HEADS * HEAD_DIM:].astype(BF16)


def _mem_kv(mem, g_mem, w_kv, k_gain):
    B = mem.shape[0]
    return pl.pallas_call(
        _mem_kv_kernel,
        grid=(B,),
        in_specs=[
            pl.BlockSpec((None, N_MEM, D_MODEL), lambda b: (b, 0, 0)),
            pl.BlockSpec((1, D_MODEL), lambda b: (0, 0)),
            pl.BlockSpec((D_MODEL, 2 * GROUP_W), lambda b: (0, 0)),
            pl.BlockSpec((1, HEAD_DIM), lambda b: (0, 0)),
        ],
        out_specs=[
            pl.BlockSpec((None, CA_HEADS, HEAD_DIM, N_MEM), lambda b: (b, 0, 0, 0)),
            pl.BlockSpec((None, N_MEM, GROUP_W), lambda b: (b, 0, 0)),
        ],
        out_shape=[
            jax.ShapeDtypeStruct((B, CA_HEADS, HEAD_DIM, N_MEM), BF16),
            jax.ShapeDtypeStruct((B, N_MEM, GROUP_W), BF16),
        ],
        scratch_shapes=[pltpu.VMEM((N_MEM, D_MODEL), BF16)],
        compiler_params=_params(("parallel",)),
        name="mem_kv",
    )(mem, g_mem, w_kv, k_gain)


def _cross_attn_kernel(x_ref, g_ref, wq_ref, qg_ref, kt_ref, v_ref, wo_ref, o_ref, h_ref, oc_ref):
    _norm_rows_to(x_ref, g_ref, h_ref)
    q = jnp.dot(h_ref[...], wq_ref[...], preferred_element_type=F32)
    qg = qg_ref[...]
    for h in range(CA_HEADS):
        lanes = slice(h * HEAD_DIM, (h + 1) * HEAD_DIM)
        qh = (_head_norm(q[:, lanes]) * (qg * QK_SCALE)).astype(BF16)
        s = jnp.dot(qh, kt_ref[h], preferred_element_type=F32)
        e = jnp.exp(s - jnp.max(s, axis=-1, keepdims=True))
        o = jnp.dot(e.astype(BF16), v_ref[:, lanes], preferred_element_type=F32)
        oc_ref[:, lanes] = (o * pl.reciprocal(jnp.sum(e, axis=-1, keepdims=True), approx=False)).astype(BF16)
    o_ref[...] = x_ref[...] + jnp.dot(oc_ref[...], wo_ref[...], preferred_element_type=F32)


def _cross_attn(x2, g_ca, w_q, q_gain, kt, v, w_o, B, S, tm=512):
    T = x2.shape[0]
    x3 = x2.reshape(B, S, D_MODEL)
    out = pl.pallas_call(
        _cross_attn_kernel,
        grid=(B, S // tm),
        in_specs=[
            pl.BlockSpec((None, tm, D_MODEL), lambda b, i: (b, i, 0)),
            pl.BlockSpec((1, D_MODEL), lambda b, i: (0, 0)),
            pl.BlockSpec((D_MODEL, GROUP_W), lambda b, i: (0, 0)),
            pl.BlockSpec((1, HEAD_DIM), lambda b, i: (0, 0)),
            pl.BlockSpec((None, CA_HEADS, HEAD_DIM, N_MEM), lambda b, i: (b, 0, 0, 0)),
            pl.BlockSpec((None, N_MEM, GROUP_W), lambda b, i: (b, 0, 0)),
            pl.BlockSpec((GROUP_W, D_MODEL), lambda b, i: (0, 0)),
        ],
        out_specs=pl.BlockSpec((None, tm, D_MODEL), lambda b, i: (b, i, 0)),
        out_shape=jax.ShapeDtypeStruct((B, S, D_MODEL), F32),
        scratch_shapes=[pltpu.VMEM((tm, D_MODEL), BF16), pltpu.VMEM((tm, GROUP_W), BF16)],
        compiler_params=_params(("parallel", "parallel")),
        name="cross_attn",
    )(x3, g_ca, w_q, q_gain, kt, v, w_o)
    return out.reshape(T, D_MODEL)


def _mlp_kernel(x_ref, g_ref, wu_ref, wd_ref, o_ref, h_ref):
    @pl.when(pl.program_id(1) == 0)
    def _():
        _norm_rows_to(x_ref, g_ref, h_ref)
        o_ref[...] = x_ref[...]

    up = jnp.dot(h_ref[...], wu_ref[...], preferred_element_type=F32)
    act = jnp.square(jnp.maximum(up, 0.0)).astype(BF16)
    o_ref[...] += jnp.dot(act, wd_ref[...], preferred_element_type=F32)


def _mlp(x2, g, w_up, w_down, tm=512, tf=1024):
    T = x2.shape[0]
    return pl.pallas_call(
        _mlp_kernel,
        grid=(T // tm, D_FF // tf),
        in_specs=[
            pl.BlockSpec((tm, D_MODEL), lambda i, j: (i, 0)),
            pl.BlockSpec((1, D_MODEL), lambda i, j: (0, 0)),
            pl.BlockSpec((D_MODEL, tf), lambda i, j: (0, j)),
            pl.BlockSpec((tf, D_MODEL), lambda i, j: (j, 0)),
        ],
        out_specs=pl.BlockSpec((tm, D_MODEL), lambda i, j: (i, 0)),
        out_shape=jax.ShapeDtypeStruct((T, D_MODEL), F32),
        scratch_shapes=[pltpu.VMEM((tm, D_MODEL), BF16)],
        compiler_params=_params(("parallel", "arbitrary")),
        name="mlp",
    )(x2, g, w_up, w_down)


def _prep_weights(w_in, conv_w, i_bias, f_bias, w_out, w_ca_q, w_ca_kv, w_ca_o, w_up, w_down):
    n_main = 8 * GROUP_W
    w_main = jnp.concatenate([w_in[:, :, :n_main], w_in[:, :, n_main + 16:]], axis=2).astype(BF16)
    pad = jnp.zeros((DEPTH, D_MODEL, HEAD_DIM - 8), F32)
    w_gate = jnp.concatenate([w_in[:, :, n_main:n_main + 8], pad, w_in[:, :, n_main + 8:n_main + 16], pad],
                             axis=2).astype(BF16)
    lane_pad = jnp.zeros((DEPTH, 1, HEAD_DIM - 8), F32)
    ib = jnp.concatenate([i_bias.reshape(DEPTH, 1, 8), lane_pad], axis=2)
    fb = jnp.concatenate([f_bias.reshape(DEPTH, 1, 8), lane_pad], axis=2)
    cw = jnp.concatenate([conv_w, jnp.zeros((DEPTH, 5, GROUP_W), F32)], axis=1)
    return dict(w_main=w_main, w_gate=w_gate, ib=ib, fb=fb, cw=cw, w_out=w_out.astype(BF16),
                w_ca_q=w_ca_q.astype(BF16), w_ca_kv=w_ca_kv.astype(BF16), w_ca_o=w_ca_o.astype(BF16),
                w_up=w_up.astype(BF16), w_down=w_down.astype(BF16))


def _trunk(x, mem, pw, g_mix, conv_b, attn_q_norm, attn_k_norm, out_gain, g_ca, g_mem, ca_q_norm, ca_k_norm, g_mlp):
    B, S, _ = x.shape
    T = B * S
    x2 = x.reshape(T, D_MODEL)
    rope = _rope_tables(S)
    ftab = _fourier_tables(S)
    for l in range(DEPTH):
        u3, gates = _norm_proj(x2, g_mix[l][None], pw["w_main"][l], pw["w_gate"][l])
        ya = _fourier(u3, B, S, ftab)
        hf, hb = _mlstm(u3, gates, pw["ib"][l], pw["fb"][l], B, S)
        yd = _axial_gqa(u3, attn_q_norm[l][None], attn_k_norm[l][None], rope, B, S)
        x2 = _combine_out(x2, ya, u3, hf, hb, yd, pw["cw"][l], conv_b[l][None], out_gain[l][None], pw["w_out"][l], S)
        kt, v = _mem_kv(mem, g_mem[l][None], pw["w_ca_kv"][l], ca_k_norm[l][None])
        x2 = _cross_attn(x2, g_ca[l][None], pw["w_ca_q"][l], ca_q_norm[l][None], kt, v, pw["w_ca_o"][l], B, S)
        x2 = _mlp(x2, g_mlp[l][None], pw["w_up"][l], pw["w_down"][l])
    return x2.reshape(B, S, D_MODEL)


def kernel(x_prompt, x_sample, mem_prompt, mem_sample, g_mix, w_in, conv_w, conv_b, i_bias, f_bias, attn_q_norm,
           attn_k_norm, out_gain, w_out, g_ca, g_mem, w_ca_q, w_ca_kv, ca_q_norm, ca_k_norm, w_ca_o, g_mlp, w_up,
           w_down):
    pw = _prep_weights(w_in, conv_w, i_bias, f_bias, w_out, w_ca_q, w_ca_kv, w_ca_o, w_up, w_down)
    args = (pw, g_mix, conv_b, attn_q_norm, attn_k_norm, out_gain, g_ca, g_mem, ca_q_norm, ca_k_norm, g_mlp)
    y_prompt = _trunk(x_prompt, mem_prompt, *args)
    y_sample = _trunk(x_sample, mem_sample, *args)
    return (y_prompt, y_sample)
```

```python
import functools
import math

import jax
import jax.numpy as jnp
from jax import lax
from jax.experimental import pallas as pl
from jax.experimental.pallas import tpu as pltpu

F32 = jnp.float32
BF16 = jnp.bfloat16

D_MODEL = 2048
DEPTH = 4
HEAD_DIM = 128
GROUP_W = 4 * HEAD_DIM
N_COLBLK = 10
GATE_W = 2 * HEAD_DIM
MLSTM_HEADS = 4
MLSTM_CHUNK = 128
ATTN_KV_HEADS = 2
GRID_W = 64
ROPE_THETA = 10000.0
ROPE_FREQS = HEAD_DIM // 4
CA_HEADS = 4
N_MEM = 256
D_FF = 4 * D_MODEL
RMS_EPS = 1e-6
QK_SCALE = HEAD_DIM ** -0.5
LOG2_E = math.log2(math.e)

BLK_FOURIER, BLK_CB, BLK_CC, BLK_CX, BLK_MQ, BLK_MK, BLK_MV, BLK_MO, BLK_AQ, BLK_AKV = range(N_COLBLK)

V7X_VMEM_LIMIT = 56 * 1024 * 1024
NORM_ROWS = 256


def _params(semantics, vmem=V7X_VMEM_LIMIT):
    return pltpu.CompilerParams(dimension_semantics=semantics, vmem_limit_bytes=vmem)


def _head_norm(y):
    return y * lax.rsqrt(jnp.mean(y * y, axis=-1, keepdims=True) + RMS_EPS)


def _norm_rows_to(x_ref, g_ref, h_ref):
    tm = x_ref.shape[0]
    g = g_ref[...]

    def body(r, carry):
        rows = pl.ds(pl.multiple_of(r * NORM_ROWS, NORM_ROWS), NORM_ROWS)
        xf = x_ref[rows, :]
        ms = jnp.mean(xf * xf, axis=-1, keepdims=True)
        h_ref[rows, :] = (xf * lax.rsqrt(ms + RMS_EPS) * g).astype(BF16)
        return carry

    lax.fori_loop(0, tm // NORM_ROWS, body, 0)


def _norm_proj_kernel(x_ref, g_ref, w_ref, wg_ref, u_ref, gate_ref, h_ref):
    @pl.when(pl.program_id(1) == 0)
    def _():
        _norm_rows_to(x_ref, g_ref, h_ref)
        gate_ref[...] = jnp.dot(h_ref[...], wg_ref[...], preferred_element_type=F32)

    res = jnp.dot(h_ref[...], w_ref[...], preferred_element_type=F32)
    for k in range(u_ref.shape[0]):
        u_ref[k] = res[:, k * GROUP_W:(k + 1) * GROUP_W].astype(BF16)


def _norm_proj(x2, g, w_main, w_gate, tm=1024, nb=2):
    T = x2.shape[0]
    tn = nb * GROUP_W
    return pl.pallas_call(
        _norm_proj_kernel,
        grid=(T // tm, N_COLBLK // nb),
        in_specs=[
            pl.BlockSpec((tm, D_MODEL), lambda i, j: (i, 0)),
            pl.BlockSpec((1, D_MODEL), lambda i, j: (0, 0)),
            pl.BlockSpec((D_MODEL, tn), lambda i, j: (0, j)),
            pl.BlockSpec((D_MODEL, GATE_W), lambda i, j: (0, 0)),
        ],
        out_specs=[
            pl.BlockSpec((nb, tm, GROUP_W), lambda i, j: (j, i, 0)),
            pl.BlockSpec((tm, GATE_W), lambda i, j: (i, 0)),
        ],
        out_shape=[
            jax.ShapeDtypeStruct((N_COLBLK, T, GROUP_W), BF16),
            jax.ShapeDtypeStruct((T, GATE_W), F32),
        ],
        scratch_shapes=[pltpu.VMEM((tm, D_MODEL), BF16)],
        compiler_params=_params(("parallel", "arbitrary")),
        name="norm_proj",
    )(x2, g, w_main, w_gate)


def _fourier_tables(S):
    n2 = HEAD_DIM
    n1 = S // n2
    i1 = jnp.arange(n1, dtype=jnp.int32)
    a1 = (2.0 * math.pi / n1) * ((i1[:, None] * i1[None, :]) % n1).astype(F32)
    f1 = jnp.concatenate([jnp.cos(a1), -jnp.sin(a1)], axis=0).astype(BF16)
    i2 = jnp.arange(n2, dtype=jnp.int32)
    p = i1[:, None, None] + n1 * i2[None, :, None]
    k = (p * i2[None, None, :]) % S
    a2 = (2.0 * math.pi / S) * k.astype(F32)
    c2, s2 = jnp.cos(a2), jnp.sin(a2)
    g = jnp.concatenate([jnp.concatenate([c2, s2], axis=2),
                         jnp.concatenate([-s2, c2], axis=2)], axis=1).astype(BF16)
    ac = (2.0 * math.pi / HEAD_DIM) * ((i2[:, None] * i2[None, :]) % HEAD_DIM).astype(F32)
    cs = jnp.concatenate([jnp.cos(ac), jnp.sin(ac)], axis=0).astype(BF16)
    return f1, g, cs


def _fourier_stage1_kernel(a_ref, f1_ref, y_ref):
    n1 = a_ref.shape[0]
    res = jnp.dot(f1_ref[...], a_ref[...], preferred_element_type=F32)
    y_ref[0] = res[:n1].astype(BF16)
    y_ref[1] = res[n1:].astype(BF16)


def _fourier_stage2_kernel(y_ref, g_ref, cs_ref, o_ref, *, scale):
    n_p = y_ref.shape[1]
    cs = cs_ref[...]
    for p in range(n_p):
        ycat = jnp.concatenate([y_ref[0, p], y_ref[1, p]], axis=0)
        x = jnp.dot(g_ref[p], ycat, preferred_element_type=F32).astype(BF16)
        for grp in range(4):
            lanes = slice(grp * HEAD_DIM, (grp + 1) * HEAD_DIM)
            xg = jnp.concatenate([x[:HEAD_DIM, lanes], x[HEAD_DIM:, lanes]], axis=1)
            o = jnp.dot(xg, cs, preferred_element_type=F32) * scale
            col = p * GROUP_W + grp * HEAD_DIM
            o_ref[:, col:col + HEAD_DIM] = _head_norm(o).astype(BF16)


def _fourier(u3, B, S, tables):
    f1, g, cs = tables
    n2 = HEAD_DIM
    n1 = S // n2
    T = B * S
    ncol = min(16, n2)
    a_view = u3[BLK_FOURIER].reshape(B, n1, n2 * GROUP_W)
    y = pl.pallas_call(
        _fourier_stage1_kernel,
        grid=(B, n2 // ncol),
        in_specs=[
            pl.BlockSpec((None, n1, ncol * GROUP_W), lambda b, j: (b, 0, j)),
            pl.BlockSpec((2 * n1, n1), lambda b, j: (0, 0)),
        ],
        out_specs=pl.BlockSpec((None, 2, n1, ncol * GROUP_W), lambda b, j: (b, 0, 0, j)),
        out_shape=jax.ShapeDtypeStruct((B, 2, n1, n2 * GROUP_W), BF16),
        compiler_params=_params(("parallel", "parallel")),
        name="fourier_stage1",
    )(a_view, f1)
    y5 = y.reshape(B, 2, n1, n2, GROUP_W)
    n_p = 8
    out = pl.pallas_call(
        functools.partial(_fourier_stage2_kernel, scale=1.0 / math.sqrt(S * HEAD_DIM)),
        grid=(B, n1 // n_p),
        in_specs=[
            pl.BlockSpec((None, 2, n_p, n2, GROUP_W), lambda b, j: (b, 0, j, 0, 0)),
            pl.BlockSpec((n_p, 2 * n2, 2 * n2), lambda b, j: (j, 0, 0)),
            pl.BlockSpec((2 * HEAD_DIM, HEAD_DIM), lambda b, j: (0, 0)),
        ],
        out_specs=pl.BlockSpec((None, n2, n_p * GROUP_W), lambda b, j: (b, 0, j)),
        out_shape=jax.ShapeDtypeStruct((B, n2, n1 * GROUP_W), BF16),
        compiler_params=_params(("parallel", "parallel")),
        name="fourier_stage2",
    )(y5, g, cs)
    return out.reshape(T, GROUP_W)


def _log_sigmoid(x):
    return jnp.minimum(x, 0.0) - jnp.log1p(jnp.exp(-jnp.abs(x)))


def _mlstm_chunk(d, q, k, v, gi, gf, ib, fb, tri, mask, c_ref, n_ref, m_ref, out_ref, rows):
    li = gi + ib
    lf = _log_sigmoid(gf + fb)
    b = jnp.dot(tri, lf, precision=lax.Precision.HIGHEST, preferred_element_type=F32)
    g = jnp.sum(lf, axis=0, keepdims=True)
    w_end = g - b + li
    m_loc = jnp.max(w_end, axis=0, keepdims=True)
    e = jnp.exp(w_end - m_loc)
    a_t = (li - b).T
    m_old = m_ref[d:d + 1, :]
    m_new = jnp.maximum(g + m_old, m_loc)
    a_sc = jnp.exp(g + m_old - m_new)
    b_sc = jnp.exp(m_loc - m_new)
    for h in range(MLSTM_HEADS):
        hp = MLSTM_HEADS * d + h
        lanes = slice(h * HEAD_DIM, (h + 1) * HEAD_DIM)
        qh = q[:, lanes]
        vh = v[:, lanes]
        ks = k[:, lanes].astype(F32) * QK_SCALE
        bc = b[:, hp:hp + 1]
        dm = jnp.where(mask, bc + a_t[hp:hp + 1, :], -jnp.inf)
        inter_log = bc + m_old[:, hp:hp + 1]
        m_t = jnp.maximum(inter_log, jnp.max(dm, axis=-1, keepdims=True))
        p = jnp.exp(dm - m_t)
        inter_w = jnp.exp(inter_log - m_t)
        qk = lax.dot_general(qh, ks.astype(BF16), (((1,), (1,)), ((), ())), preferred_element_type=F32)
        a_mat = p * qk
        c_prev = c_ref[hp]
        n_prev = n_ref[hp:hp + 1, :]
        num = (jnp.dot(a_mat.astype(BF16), vh, preferred_element_type=F32)
               + inter_w * jnp.dot(qh, c_prev.astype(BF16), preferred_element_type=F32))
        den = (jnp.sum(a_mat, axis=-1, keepdims=True)
               + inter_w * jnp.sum(qh.astype(F32) * n_prev, axis=-1, keepdims=True))
        hout = num / jnp.maximum(jnp.abs(den), jnp.exp(-m_t))
        out_ref[rows, lanes] = hout.astype(out_ref.dtype)
        ke = ks * e[:, hp:hp + 1]
        c_loc = jnp.dot(ke.T.astype(BF16), vh, preferred_element_type=F32)
        n_loc = jnp.sum(ke, axis=0, keepdims=True)
        ah = a_sc[:, hp:hp + 1]
        bh = b_sc[:, hp:hp + 1]
        c_ref[hp] = ah * c_prev + bh * c_loc
        n_ref[hp:hp + 1, :] = ah * n_prev + bh * n_loc
    m_ref[d:d + 1, :] = m_new


def _mlstm_kernel(qf_ref, kf_ref, vf_ref, gf_ref, qb_ref, kb_ref, vb_ref, gb_ref, ib_ref, fb_ref,
                  hf_ref, hb_ref, c_ref, n_ref, m_ref):
    L = MLSTM_CHUNK
    n_sub = qf_ref.shape[0] // L

    @pl.when(pl.program_id(1) == 0)
    def _():
        c_ref[...] = jnp.zeros_like(c_ref)
        n_ref[...] = jnp.zeros_like(n_ref)
        m_ref[...] = jnp.zeros_like(m_ref)

    row = lax.broadcasted_iota(jnp.int32, (L, L), 0)
    col = lax.broadcasted_iota(jnp.int32, (L, L), 1)
    mask_f = col <= row
    mask_b = col >= row
    tri_f = jnp.where(mask_f, 1.0, 0.0).astype(F32)
    tri_b = jnp.where(mask_b, 1.0, 0.0).astype(F32)
    ib = ib_ref[...]
    fb = fb_ref[...]

    def body(j, carry):
        rf = pl.ds(pl.multiple_of(j * L, L), L)
        rb = pl.ds(pl.multiple_of((n_sub - 1 - j) * L, L), L)
        _mlstm_chunk(0, qf_ref[rf, :], kf_ref[rf, :], vf_ref[rf, :], gf_ref[rf, :HEAD_DIM], gf_ref[rf, HEAD_DIM:],
                     ib, fb, tri_f, mask_f, c_ref, n_ref, m_ref, hf_ref, rf)
        _mlstm_chunk(1, qb_ref[rb, :], kb_ref[rb, :], vb_ref[rb, :], gb_ref[rb, :HEAD_DIM], gb_ref[rb, HEAD_DIM:],
                     ib, fb, tri_b, mask_b, c_ref, n_ref, m_ref, hb_ref, rb)
        return carry

    lax.fori_loop(0, n_sub, body, 0)


def _mlstm(u3, gates, ib, fb, B, S, n_sub=4):
    T = B * S
    rows = n_sub * MLSTM_CHUNK
    nblk = S // rows
    u4 = u3.reshape(N_COLBLK, B, S, GROUP_W)
    g3 = gates.reshape(B, S, GATE_W)

    def fwd(blk):
        return pl.BlockSpec((None, None, rows, GROUP_W), lambda b, c: (blk, b, c, 0))

    def bwd(blk):
        return pl.BlockSpec((None, None, rows, GROUP_W), lambda b, c: (blk, b, nblk - 1 - c, 0))

    hf, hb = pl.pallas_call(
        _mlstm_kernel,
        grid=(B, nblk),
        in_specs=[
            fwd(BLK_MQ), fwd(BLK_MK), fwd(BLK_MV),
            pl.BlockSpec((None, rows, GATE_W), lambda b, c: (b, c, 0)),
            bwd(BLK_MQ), bwd(BLK_MK), bwd(BLK_MV),
            pl.BlockSpec((None, rows, GATE_W), lambda b, c: (b, nblk - 1 - c, 0)),
            pl.BlockSpec((1, HEAD_DIM), lambda b, c: (0, 0)),
            pl.BlockSpec((1, HEAD_DIM), lambda b, c: (0, 0)),
        ],
        out_specs=[
            pl.BlockSpec((None, rows, GROUP_W), lambda b, c: (b, c, 0)),
            pl.BlockSpec((None, rows, GROUP_W), lambda b, c: (b, nblk - 1 - c, 0)),
        ],
        out_shape=[jax.ShapeDtypeStruct((B, S, GROUP_W), BF16)] * 2,
        scratch_shapes=[
            pltpu.VMEM((2 * MLSTM_HEADS, HEAD_DIM, HEAD_DIM), F32),
            pltpu.VMEM((2 * MLSTM_HEADS, HEAD_DIM), F32),
            pltpu.VMEM((8, HEAD_DIM), F32),
        ],
        compiler_params=_params(("parallel", "arbitrary")),
        name="mlstm",
    )(u4, u4, u4, g3, u4, u4, u4, g3, ib, fb)
    return hf.reshape(T, GROUP_W), hb.reshape(T, GROUP_W)


def _rope_tables(S):
    rows = S // GRID_W
    row = jnp.repeat(jnp.arange(rows), GRID_W).astype(F32)
    colp = jnp.tile(jnp.arange(GRID_W), rows).astype(F32)
    inv = ROPE_THETA ** (-jnp.arange(ROPE_FREQS, dtype=F32) / ROPE_FREQS)
    ar = row[:, None] * inv
    ac = colp[:, None] * inv
    ang = jnp.concatenate([ar, ar, ac, ac], axis=1)
    first_half = (jnp.arange(HEAD_DIM) % (2 * ROPE_FREQS)) < ROPE_FREQS
    return jnp.cos(ang), jnp.where(first_half, -1.0, 1.0) * jnp.sin(ang)


def _rope(x, cos, sin_signed, first_half):
    partner = jnp.where(first_half, pltpu.roll(x, HEAD_DIM - ROPE_FREQS, axis=1), pltpu.roll(x, ROPE_FREQS, axis=1))
    return x * cos + partner * sin_signed


def _attn_prep_kernel(aq_ref, akv_ref, cos_ref, sin_ref, qg_ref, kg_ref, q_ref, kt_ref, va_ref):
    cos = cos_ref[...]
    sin = sin_ref[...]
    lane = lax.broadcasted_iota(jnp.int32, cos.shape, 1)
    first_half = (lane % (2 * ROPE_FREQS)) < ROPE_FREQS
    qg = qg_ref[...]
    kg = kg_ref[...]
    for h in range(4):
        lanes = slice(h * HEAD_DIM, (h + 1) * HEAD_DIM)
        qh = _head_norm(aq_ref[:, lanes].astype(F32)) * qg
        q_ref[:, lanes] = (_rope(qh, cos, sin, first_half) * (QK_SCALE * LOG2_E)).astype(BF16)
    for h in range(ATTN_KV_HEADS):
        lanes = slice(h * HEAD_DIM, (h + 1) * HEAD_DIM)
        kh = _head_norm(akv_ref[:, lanes].astype(F32)) * kg
        kt_ref[h] = _rope(kh, cos, sin, first_half).T.astype(BF16)
        vl = slice((ATTN_KV_HEADS + h) * HEAD_DIM, (ATTN_KV_HEADS + h + 1) * HEAD_DIM)
        va_ref[h, :, :HEAD_DIM] = akv_ref[:, vl]
        va_ref[h, :, HEAD_DIM:] = jnp.ones((akv_ref.shape[0], HEAD_DIM), BF16)


def _flash_kernel(q_ref, kt_ref, va_ref, o_ref, m_ref, acc_ref, s_ref, p_ref, a_ref, mp_ref):
    tq = q_ref.shape[0]
    nkc, _, tk = kt_ref.shape
    q2 = jnp.concatenate([q_ref[:, :HEAD_DIM], q_ref[:, HEAD_DIM:]], axis=0)
    m_ref[...] = jnp.full_like(m_ref, -jnp.inf)
    acc_ref[...] = jnp.zeros_like(acc_ref)

    def scores(c, slot):
        s = jnp.dot(q2, kt_ref[c], preferred_element_type=F32)
        s_ref[slot] = s
        part = s[:, :HEAD_DIM]
        for t in range(1, tk // HEAD_DIM):
            part = jnp.maximum(part, s[:, t * HEAD_DIM:(t + 1) * HEAD_DIM])
        mp_ref[slot] = part

    def softmax(slot):
        s = s_ref[slot]
        m_prev = m_ref[...]
        m_new = jnp.maximum(m_prev, jnp.max(mp_ref[slot], axis=-1, keepdims=True))
        a_ref[slot] = jnp.exp2(m_prev - m_new)
        p_ref[slot] = jnp.exp2(s - jnp.tile(m_new, (1, tk // HEAD_DIM))).astype(BF16)
        m_ref[...] = m_new

    def accumulate(c, slot):
        vc = va_ref[pl.ds(pl.multiple_of(c * tk, tk), tk), :]
        acc_ref[...] = (jnp.tile(a_ref[slot], (1, 2)) * acc_ref[...]
                        + jnp.dot(p_ref[slot], vc, preferred_element_type=F32))

    def step(c, slot):
        scores(c + 1, 1 - slot)
        accumulate(c - 1, 1 - slot)
        softmax(slot)

    scores(0, 0)
    scores(1, 1)
    softmax(0)

    def pair(i, carry):
        c = 2 * i + 1
        step(c, 1)
        step(c + 1, 0)
        return carry

    lax.fori_loop(0, (nkc - 2) // 2, pair, 0)
    accumulate(nkc - 2, 0)
    softmax(1)
    accumulate(nkc - 1, 1)
    o = acc_ref[:, :HEAD_DIM] / acc_ref[:, HEAD_DIM:]
    o_ref[:, :HEAD_DIM] = o[:tq].astype(o_ref.dtype)
    o_ref[:, HEAD_DIM:] = o[tq:].astype(o_ref.dtype)


def _axial_gqa(u3, q_gain, k_gain, rope, B, S, tq=512, tk=1024):
    T = B * S
    cos, sin = rope
    u4 = u3.reshape(N_COLBLK, B, S, GROUP_W)
    tk = min(tk, S // 2)
    nkc = S // tk
    assert nkc >= 2 and nkc % 2 == 0, "flash kernel pipelines kv chunks in pairs"
    q_r, kt, va = pl.pallas_call(
        _attn_prep_kernel,
        grid=(B, nkc),
        in_specs=[
            pl.BlockSpec((None, None, tk, GROUP_W), lambda b, i: (BLK_AQ, b, i, 0)),
            pl.BlockSpec((None, None, tk, GROUP_W), lambda b, i: (BLK_AKV, b, i, 0)),
            pl.BlockSpec((tk, HEAD_DIM), lambda b, i: (i, 0)),
            pl.BlockSpec((tk, HEAD_DIM), lambda b, i: (i, 0)),
            pl.BlockSpec((1, HEAD_DIM), lambda b, i: (0, 0)),
            pl.BlockSpec((1, HEAD_DIM), lambda b, i: (0, 0)),
        ],
        out_specs=[
            pl.BlockSpec((None, tk, GROUP_W), lambda b, i: (b, i, 0)),
            pl.BlockSpec((None, ATTN_KV_HEADS, None, HEAD_DIM, tk), lambda b, i: (b, 0, i, 0, 0)),
            pl.BlockSpec((None, ATTN_KV_HEADS, tk, 2 * HEAD_DIM), lambda b, i: (b, 0, i, 0)),
        ],
        out_shape=[
            jax.ShapeDtypeStruct((B, S, GROUP_W), BF16),
            jax.ShapeDtypeStruct((B, ATTN_KV_HEADS, nkc, HEAD_DIM, tk), BF16),
            jax.ShapeDtypeStruct((B, ATTN_KV_HEADS, S, 2 * HEAD_DIM), BF16),
        ],
        compiler_params=_params(("parallel", "parallel")),
        name="attn_prep",
    )(u4, u4, cos, sin, q_gain, k_gain)
    out = pl.pallas_call(
        _flash_kernel,
        grid=(B, ATTN_KV_HEADS, S // tq),
        in_specs=[
            pl.BlockSpec((None, tq, 2 * HEAD_DIM), lambda b, k, i: (b, i, k)),
            pl.BlockSpec((None, None, nkc, HEAD_DIM, tk), lambda b, k, i: (b, k, 0, 0, 0)),
            pl.BlockSpec((None, None, S, 2 * HEAD_DIM), lambda b, k, i: (b, k, 0, 0)),
        ],
        out_specs=pl.BlockSpec((None, tq, 2 * HEAD_DIM), lambda b, k, i: (b, i, k)),
        out_shape=jax.ShapeDtypeStruct((B, S, GROUP_W), BF16),
        scratch_shapes=[
            pltpu.VMEM((2 * tq, HEAD_DIM), F32),
            pltpu.VMEM((2 * tq, 2 * HEAD_DIM), F32),
            pltpu.VMEM((2, 2 * tq, tk), F32),
            pltpu.VMEM((2, 2 * tq, tk), BF16),
            pltpu.VMEM((2, 2 * tq, HEAD_DIM), F32),
            pltpu.VMEM((2, 2 * tq, HEAD_DIM), F32),
        ],
        compiler_params=_params(("parallel", "parallel", "arbitrary")),
        name="flash_gqa",
    )(q_r, kt, va)
    return out.reshape(T, GROUP_W)


def _combine_kernel(x_ref, ya_ref, cb_ref, cc_ref, cx_ref, ccp_ref, cxp_ref, ccn_ref, cxn_ref, mo_ref,
                    hf_ref, hb_ref, yd_ref, cw_ref, cbias_ref, gain_ref, w_ref, o_ref, y_ref, *, tiles_per_seq):
    tm = x_ref.shape[0]
    i = pl.program_id(0)
    halo = ccp_ref.shape[0]
    at_start = (i % tiles_per_seq) == 0
    at_end = (i % tiles_per_seq) == tiles_per_seq - 1
    gain = gain_ref[...]

    z = cc_ref[...].astype(F32) * cx_ref[...].astype(F32)
    z_before = ccp_ref[halo - 1:halo, :].astype(F32) * cxp_ref[halo - 1:halo, :].astype(F32)
    z_after = ccn_ref[0:1, :].astype(F32) * cxn_ref[0:1, :].astype(F32)
    z_before = jnp.where(at_start, 0.0, z_before)
    z_after = jnp.where(at_end, 0.0, z_after)
    rid = lax.broadcasted_iota(jnp.int32, z.shape, 0)
    z_m1 = jnp.where(rid == 0, z_before, pltpu.roll(z, 1, axis=0))
    z_p1 = jnp.where(rid == tm - 1, z_after, pltpu.roll(z, tm - 1, axis=0))
    conv = z_m1 * cw_ref[0:1, :] + z * cw_ref[1:2, :] + z_p1 * cw_ref[2:3, :] + cbias_ref[...]
    yb = cb_ref[...].astype(F32) * conv

    for grp in range(4):
        lanes = slice(grp * HEAD_DIM, (grp + 1) * HEAD_DIM)
        y_ref[:, lanes] = (ya_ref[:, lanes].astype(F32) * gain[:, lanes]).astype(BF16)
        gl = slice(GROUP_W + grp * HEAD_DIM, GROUP_W + (grp + 1) * HEAD_DIM)
        y_ref[:, gl] = (_head_norm(yb[:, lanes]) * gain[:, gl]).astype(BF16)
        gl = slice(2 * GROUP_W + grp * HEAD_DIM, 2 * GROUP_W + (grp + 1) * HEAD_DIM)
        yc = _head_norm(hf_ref[:, lanes].astype(F32) + hb_ref[:, lanes].astype(F32))
        y_ref[:, gl] = (jax.nn.sigmoid(mo_ref[:, lanes].astype(F32)) * yc * gain[:, gl]).astype(BF16)
        gl = slice(3 * GROUP_W + grp * HEAD_DIM, 3 * GROUP_W + (grp + 1) * HEAD_DIM)
        y_ref[:, gl] = (_head_norm(yd_ref[:, lanes].astype(F32)) * gain[:, gl]).astype(BF16)

    o_ref[...] = x_ref[...] + jnp.dot(y_ref[...], w_ref[...], preferred_element_type=F32)


def _combine_out(x2, ya, u3, hf, hb, yd, conv_w, conv_b, out_gain, w_out, S, tm=512, halo=16):
    T = x2.shape[0]
    hb_per_tile = tm // halo
    n_halo = T // halo

    def u_blk(blk):
        return pl.BlockSpec((None, tm, GROUP_W), lambda i: (blk, i, 0))

    def u_prev(blk):
        return pl.BlockSpec((None, halo, GROUP_W), lambda i: (blk, jnp.maximum(i * hb_per_tile - 1, 0), 0))

    def u_next(blk):
        return pl.BlockSpec((None, halo, GROUP_W), lambda i: (blk, jnp.minimum((i + 1) * hb_per_tile, n_halo - 1), 0))

    tok = pl.BlockSpec((tm, GROUP_W), lambda i: (i, 0))
    return pl.pallas_call(
        functools.partial(_combine_kernel, tiles_per_seq=S // tm),
        grid=(T // tm,),
        in_specs=[
            pl.BlockSpec((tm, D_MODEL), lambda i: (i, 0)),
            tok,
            u_blk(BLK_CB), u_blk(BLK_CC), u_blk(BLK_CX),
            u_prev(BLK_CC), u_prev(BLK_CX), u_next(BLK_CC), u_next(BLK_CX),
            u_blk(BLK_MO),
            tok, tok, tok,
            pl.BlockSpec((8, GROUP_W), lambda i: (0, 0)),
            pl.BlockSpec((1, GROUP_W), lambda i: (0, 0)),
            pl.BlockSpec((1, D_MODEL), lambda i: (0, 0)),
            pl.BlockSpec((D_MODEL, D_MODEL), lambda i: (0, 0)),
        ],
        out_specs=pl.BlockSpec((tm, D_MODEL), lambda i: (i, 0)),
        out_shape=jax.ShapeDtypeStruct((T, D_MODEL), F32),
        scratch_shapes=[pltpu.VMEM((tm, D_MODEL), BF16)],
        compiler_params=_params(("parallel",)),
        name="combine_out",
    )(x2, ya, u3, u3, u3, u3, u3, u3, u3, u3, hf, hb, yd, conv_w, conv_b, out_gain, w_out)


def _mem_kv_kernel(mem_ref, g_ref, w_ref, kg_ref, kt_ref, v_ref, h_ref):
    _norm_rows_to(mem_ref, g_ref, h_ref)
    kv = jnp.dot(h_ref[...], w_ref[...], preferred_element_type=F32)
    kg = kg_ref[...]
    for h in range(CA_HEADS):
        lanes = slice(h * HEAD_DIM, (h + 1) * HEAD_DIM)
        kt_ref[h] = (_head_norm(kv[:, lanes]) * kg).T.astype(BF16)
    v_ref[...] = kv[:, CA_HEADS * HEAD_DIM:].astype(BF16)


def _mem_kv(mem, g_mem, w_kv, k_gain):
    B = mem.shape[0]
    return pl.pallas_call(
        _mem_kv_kernel,
        grid=(B,),
        in_specs=[
            pl.BlockSpec((None, N_MEM, D_MODEL), lambda b: (b, 0, 0)),
            pl.BlockSpec((1, D_MODEL), lambda b: (0, 0)),
            pl.BlockSpec((D_MODEL, 2 * GROUP_W), lambda b: (0, 0)),
            pl.BlockSpec((1, HEAD_DIM), lambda b: (0, 0)),
        ],
        out_specs=[
            pl.BlockSpec((None, CA_HEADS, HEAD_DIM, N_MEM), lambda b: (b, 0, 0, 0)),
            pl.BlockSpec((None, N_MEM, GROUP_W), lambda b: (b, 0, 0)),
        ],
        out_shape=[
            jax.ShapeDtypeStruct((B, CA_HEADS, HEAD_DIM, N_MEM), BF16),
            jax.ShapeDtypeStruct((B, N_MEM, GROUP_W), BF16),
        ],
        scratch_shapes=[pltpu.VMEM((N_MEM, D_MODEL), BF16)],
        compiler_params=_params(("parallel",)),
        name="mem_kv",
    )(mem, g_mem, w_kv, k_gain)


def _cross_attn_kernel(x_ref, g_ref, wq_ref, qg_ref, kt_ref, v_ref, wo_ref, o_ref, h_ref, oc_ref):
    _norm_rows_to(x_ref, g_ref, h_ref)
    q = jnp.dot(h_ref[...], wq_ref[...], preferred_element_type=F32)
    qg = qg_ref[...]
    for h in range(CA_HEADS):
        lanes = slice(h * HEAD_DIM, (h + 1) * HEAD_DIM)
        qh = (_head_norm(q[:, lanes]) * (qg * QK_SCALE)).astype(BF16)
        s = jnp.dot(qh, kt_ref[h], preferred_element_type=F32)
        e = jnp.exp(s - jnp.max(s, axis=-1, keepdims=True))
        o = jnp.dot(e.astype(BF16), v_ref[:, lanes], preferred_element_type=F32)
        oc_ref[:, lanes] = (o * pl.reciprocal(jnp.sum(e, axis=-1, keepdims=True), approx=False)).astype(BF16)
    o_ref[...] = x_ref[...] + jnp.dot(oc_ref[...], wo_ref[...], preferred_element_type=F32)


def _cross_attn(x2, g_ca, w_q, q_gain, kt, v, w_o, B, S, tm=1024):
    T = x2.shape[0]
    x3 = x2.reshape(B, S, D_MODEL)
    out = pl.pallas_call(
        _cross_attn_kernel,
        grid=(B, S // tm),
        in_specs=[
            pl.BlockSpec((None, tm, D_MODEL), lambda b, i: (b, i, 0)),
            pl.BlockSpec((1, D_MODEL), lambda b, i: (0, 0)),
            pl.BlockSpec((D_MODEL, GROUP_W), lambda b, i: (0, 0)),
            pl.BlockSpec((1, HEAD_DIM), lambda b, i: (0, 0)),
            pl.BlockSpec((None, CA_HEADS, HEAD_DIM, N_MEM), lambda b, i: (b, 0, 0, 0)),
            pl.BlockSpec((None, N_MEM, GROUP_W), lambda b, i: (b, 0, 0)),
            pl.BlockSpec((GROUP_W, D_MODEL), lambda b, i: (0, 0)),
        ],
        out_specs=pl.BlockSpec((None, tm, D_MODEL), lambda b, i: (b, i, 0)),
        out_shape=jax.ShapeDtypeStruct((B, S, D_MODEL), F32),
        scratch_shapes=[pltpu.VMEM((tm, D_MODEL), BF16), pltpu.VMEM((tm, GROUP_W), BF16)],
        compiler_params=_params(("parallel", "parallel")),
        name="cross_attn",
    )(x3, g_ca, w_q, q_gain, kt, v, w_o)
    return out.reshape(T, D_MODEL)


def _mlp_kernel(x_ref, g_ref, wu_ref, wd_ref, o_ref, h_ref):
    @pl.when(pl.program_id(1) == 0)
    def _():
        _norm_rows_to(x_ref, g_ref, h_ref)
        o_ref[...] = x_ref[...]

    up = jnp.dot(h_ref[...], wu_ref[...], preferred_element_type=F32)
    act = jnp.square(jnp.maximum(up, 0.0)).astype(BF16)
    o_ref[...] += jnp.dot(act, wd_ref[...], preferred_element_type=F32)


def _mlp(x2, g, w_up, w_down, tm=512, tf=1024):
    T = x2.shape[0]
    return pl.pallas_call(
        _mlp_kernel,
        grid=(T // tm, D_FF // tf),
        in_specs=[
            pl.BlockSpec((tm, D_MODEL), lambda i, j: (i, 0)),
            pl.BlockSpec((1, D_MODEL), lambda i, j: (0, 0)),
            pl.BlockSpec((D_MODEL, tf), lambda i, j: (0, j)),
            pl.BlockSpec((tf, D_MODEL), lambda i, j: (j, 0)),
        ],
        out_specs=pl.BlockSpec((tm, D_MODEL), lambda i, j: (i, 0)),
        out_shape=jax.ShapeDtypeStruct((T, D_MODEL), F32),
        scratch_shapes=[pltpu.VMEM((tm, D_MODEL), BF16)],
        compiler_params=_params(("parallel", "arbitrary")),
        name="mlp",
    )(x2, g, w_up, w_down)


def _prep_weights(w_in, conv_w, i_bias, f_bias, w_out, w_ca_q, w_ca_kv, w_ca_o, w_up, w_down):
    n_main = 8 * GROUP_W
    w_main = jnp.concatenate([w_in[:, :, :n_main], w_in[:, :, n_main + 16:]], axis=2).astype(BF16)
    pad = jnp.zeros((DEPTH, D_MODEL, HEAD_DIM - 8), F32)
    w_gate = jnp.concatenate([w_in[:, :, n_main:n_main + 8], pad, w_in[:, :, n_main + 8:n_main + 16], pad],
                             axis=2).astype(BF16)
    lane_pad = jnp.zeros((DEPTH, 1, HEAD_DIM - 8), F32)
    ib = jnp.concatenate([i_bias.reshape(DEPTH, 1, 8), lane_pad], axis=2)
    fb = jnp.concatenate([f_bias.reshape(DEPTH, 1, 8), lane_pad], axis=2)
    cw = jnp.concatenate([conv_w, jnp.zeros((DEPTH, 5, GROUP_W), F32)], axis=1)
    return dict(w_main=w_main, w_gate=w_gate, ib=ib, fb=fb, cw=cw, w_out=w_out.astype(BF16),
                w_ca_q=w_ca_q.astype(BF16), w_ca_kv=w_ca_kv.astype(BF16), w_ca_o=w_ca_o.astype(BF16),
                w_up=w_up.astype(BF16), w_down=w_down.astype(BF16))


def _trunk(x, mem, pw, g_mix, conv_b, attn_q_norm, attn_k_norm, out_gain, g_ca, g_mem, ca_q_norm, ca_k_norm, g_mlp):
    B, S, _ = x.shape
    T = B * S
    x2 = x.reshape(T, D_MODEL)
    rope = _rope_tables(S)
    ftab = _fourier_tables(S)
    for l in range(DEPTH):
        u3, gates = _norm_proj(x2, g_mix[l][None], pw["w_main"][l], pw["w_gate"][l])
        ya = _fourier(u3, B, S, ftab)
        hf, hb = _mlstm(u3, gates, pw["ib"][l], pw["fb"][l], B, S)
        yd = _axial_gqa(u3, attn_q_norm[l][None], attn_k_norm[l][None], rope, B, S)
        x2 = _combine_out(x2, ya, u3, hf, hb, yd, pw["cw"][l], conv_b[l][None], out_gain[l][None], pw["w_out"][l], S)
        kt, v = _mem_kv(mem, g_mem[l][None], pw["w_ca_kv"][l], ca_k_norm[l][None])
        x2 = _cross_attn(x2, g_ca[l][None], pw["w_ca_q"][l], ca_q_norm[l][None], kt, v, pw["w_ca_o"][l], B, S)
        x2 = _mlp(x2, g_mlp[l][None], pw["w_up"][l], pw["w_down"][l])
    return x2.reshape(B, S, D_MODEL)


def kernel(x_prompt, x_sample, mem_prompt, mem_sample, g_mix, w_in, conv_w, conv_b, i_bias, f_bias, attn_q_norm,
           attn_k_norm, out_gain, w_out, g_ca, g_mem, w_ca_q, w_ca_kv, ca_q_norm, ca_k_norm, w_ca_o, g_mlp, w_up,
           w_down):
    pw = _prep_weights(w_in, conv_w, i_bias, f_bias, w_out, w_ca_q, w_ca_kv, w_ca_o, w_up, w_down)
    args = (pw, g_mix, conv_b, attn_q_norm, attn_k_norm, out_gain, g_ca, g_mem, ca_q_norm, ca_k_norm, g_mlp)
    y_prompt = _trunk(x_prompt, mem_prompt, *args)
    y_sample = _trunk(x_sample, mem_sample, *args)
    return (y_prompt, y_sample)
```

```python
import functools
import math

import jax
import jax.numpy as jnp
from jax import lax
from jax.experimental import pallas as pl
from jax.experimental.pallas import tpu as pltpu

F32 = jnp.float32
BF16 = jnp.bfloat16

D_MODEL = 2048
DEPTH = 4
HEAD_DIM = 128
GROUP_W = 4 * HEAD_DIM
N_COLBLK = 10
GATE_W = 2 * HEAD_DIM
MLSTM_HEADS = 4
MLSTM_CHUNK = 128
ATTN_KV_HEADS = 2
GRID_W = 64
ROPE_THETA = 10000.0
ROPE_FREQS = HEAD_DIM // 4
CA_HEADS = 4
N_MEM = 256
D_FF = 4 * D_MODEL
RMS_EPS = 1e-6
QK_SCALE = HEAD_DIM ** -0.5
LOG2_E = math.log2(math.e)
FRAME_MARGIN = 100.0

BLK_FOURIER, BLK_CB, BLK_CC, BLK_CX, BLK_MQ, BLK_MK, BLK_MV, BLK_MO, BLK_AQ, BLK_AKV = range(N_COLBLK)

V7X_VMEM_LIMIT = 56 * 1024 * 1024
NORM_ROWS = 256


def _params(semantics, vmem=V7X_VMEM_LIMIT):
    return pltpu.CompilerParams(dimension_semantics=semantics, vmem_limit_bytes=vmem)


def _head_norm(y):
    return y * lax.rsqrt(jnp.mean(y * y, axis=-1, keepdims=True) + RMS_EPS)


def _norm_rows_to(x_ref, g_ref, h_ref):
    tm = x_ref.shape[0]
    g = g_ref[...]

    def body(r, carry):
        rows = pl.ds(pl.multiple_of(r * NORM_ROWS, NORM_ROWS), NORM_ROWS)
        xf = x_ref[rows, :]
        ms = jnp.mean(xf * xf, axis=-1, keepdims=True)
        h_ref[rows, :] = (xf * lax.rsqrt(ms + RMS_EPS) * g).astype(BF16)
        return carry

    lax.fori_loop(0, tm // NORM_ROWS, body, 0)


def _norm_proj_kernel(x_ref, g_ref, w_ref, wg_ref, u_ref, gate_ref, h_ref):
    @pl.when(pl.program_id(1) == 0)
    def _():
        _norm_rows_to(x_ref, g_ref, h_ref)
        gate_ref[...] = jnp.dot(h_ref[...], wg_ref[...], preferred_element_type=F32)

    res = jnp.dot(h_ref[...], w_ref[...], preferred_element_type=F32)
    for k in range(u_ref.shape[0]):
        u_ref[k] = res[:, k * GROUP_W:(k + 1) * GROUP_W].astype(BF16)


def _norm_proj(x2, g, w_main, w_gate, tm=1024, nb=2):
    T = x2.shape[0]
    tn = nb * GROUP_W
    return pl.pallas_call(
        _norm_proj_kernel,
        grid=(T // tm, N_COLBLK // nb),
        in_specs=[
            pl.BlockSpec((tm, D_MODEL), lambda i, j: (i, 0)),
            pl.BlockSpec((1, D_MODEL), lambda i, j: (0, 0)),
            pl.BlockSpec((D_MODEL, tn), lambda i, j: (0, j)),
            pl.BlockSpec((D_MODEL, GATE_W), lambda i, j: (0, 0)),
        ],
        out_specs=[
            pl.BlockSpec((nb, tm, GROUP_W), lambda i, j: (j, i, 0)),
            pl.BlockSpec((tm, GATE_W), lambda i, j: (i, 0)),
        ],
        out_shape=[
            jax.ShapeDtypeStruct((N_COLBLK, T, GROUP_W), BF16),
            jax.ShapeDtypeStruct((T, GATE_W), F32),
        ],
        scratch_shapes=[pltpu.VMEM((tm, D_MODEL), BF16)],
        compiler_params=_params(("parallel", "arbitrary")),
        name="norm_proj",
    )(x2, g, w_main, w_gate)


def _fourier_tables(S):
    n2 = HEAD_DIM
    n1 = S // n2
    i1 = jnp.arange(n1, dtype=jnp.int32)
    a1 = (2.0 * math.pi / n1) * ((i1[:, None] * i1[None, :]) % n1).astype(F32)
    f1 = jnp.concatenate([jnp.cos(a1), -jnp.sin(a1)], axis=0).astype(BF16)
    i2 = jnp.arange(n2, dtype=jnp.int32)
    p = i1[:, None, None] + n1 * i2[None, :, None]
    k = (p * i2[None, None, :]) % S
    a2 = (2.0 * math.pi / S) * k.astype(F32)
    c2, s2 = jnp.cos(a2), jnp.sin(a2)
    g = jnp.concatenate([jnp.concatenate([c2, s2], axis=2),
                         jnp.concatenate([-s2, c2], axis=2)], axis=1).astype(BF16)
    ac = (2.0 * math.pi / HEAD_DIM) * ((i2[:, None] * i2[None, :]) % HEAD_DIM).astype(F32)
    cs = jnp.concatenate([jnp.cos(ac), jnp.sin(ac)], axis=0).astype(BF16)
    return f1, g, cs


def _fourier_stage1_kernel(a_ref, f1_ref, y_ref):
    n1 = a_ref.shape[0]
    res = jnp.dot(f1_ref[...], a_ref[...], preferred_element_type=F32)
    y_ref[0] = res[:n1].astype(BF16)
    y_ref[1] = res[n1:].astype(BF16)


def _fourier_stage2_kernel(y_ref, g_ref, cs_ref, o_ref, *, scale):
    n_p = y_ref.shape[1]
    cs = cs_ref[...]
    for p in range(n_p):
        ycat = jnp.concatenate([y_ref[0, p], y_ref[1, p]], axis=0)
        x = jnp.dot(g_ref[p], ycat, preferred_element_type=F32).astype(BF16)
        for grp in range(4):
            lanes = slice(grp * HEAD_DIM, (grp + 1) * HEAD_DIM)
            xg = jnp.concatenate([x[:HEAD_DIM, lanes], x[HEAD_DIM:, lanes]], axis=1)
            o = jnp.dot(xg, cs, preferred_element_type=F32) * scale
            col = p * GROUP_W + grp * HEAD_DIM
            o_ref[:, col:col + HEAD_DIM] = _head_norm(o).astype(BF16)


def _fourier(u3, B, S, tables):
    f1, g, cs = tables
    n2 = HEAD_DIM
    n1 = S // n2
    T = B * S
    ncol = min(16, n2)
    a_view = u3[BLK_FOURIER].reshape(B, n1, n2 * GROUP_W)
    y = pl.pallas_call(
        _fourier_stage1_kernel,
        grid=(B, n2 // ncol),
        in_specs=[
            pl.BlockSpec((None, n1, ncol * GROUP_W), lambda b, j: (b, 0, j)),
            pl.BlockSpec((2 * n1, n1), lambda b, j: (0, 0)),
        ],
        out_specs=pl.BlockSpec((None, 2, n1, ncol * GROUP_W), lambda b, j: (b, 0, 0, j)),
        out_shape=jax.ShapeDtypeStruct((B, 2, n1, n2 * GROUP_W), BF16),
        compiler_params=_params(("parallel", "parallel")),
        name="fourier_stage1",
    )(a_view, f1)
    y5 = y.reshape(B, 2, n1, n2, GROUP_W)
    n_p = 8
    out = pl.pallas_call(
        functools.partial(_fourier_stage2_kernel, scale=1.0 / math.sqrt(S * HEAD_DIM)),
        grid=(B, n1 // n_p),
        in_specs=[
            pl.BlockSpec((None, 2, n_p, n2, GROUP_W), lambda b, j: (b, 0, j, 0, 0)),
            pl.BlockSpec((n_p, 2 * n2, 2 * n2), lambda b, j: (j, 0, 0)),
            pl.BlockSpec((2 * HEAD_DIM, HEAD_DIM), lambda b, j: (0, 0)),
        ],
        out_specs=pl.BlockSpec((None, n2, n_p * GROUP_W), lambda b, j: (b, 0, j)),
        out_shape=jax.ShapeDtypeStruct((B, n2, n1 * GROUP_W), BF16),
        compiler_params=_params(("parallel", "parallel")),
        name="fourier_stage2",
    )(y5, g, cs)
    return out.reshape(T, GROUP_W)


def _log_sigmoid(x):
    return jnp.minimum(x, 0.0) - jnp.log1p(jnp.exp(-jnp.abs(x)))


def _mlstm_chunk(d, q, k, v, gi, gf, ib, fb, tri, mask, c_ref, n_ref, m_ref, out_ref, rows):
    li = gi + ib
    lf = _log_sigmoid(gf + fb)
    b = jnp.dot(tri, lf, precision=lax.Precision.HIGHEST, preferred_element_type=F32)
    g = jnp.sum(lf, axis=0, keepdims=True)
    w_end = g - b + li
    m_loc = jnp.max(w_end, axis=0, keepdims=True)
    e = jnp.exp(w_end - m_loc)
    a_t = (li - b).T
    m_old = m_ref[d:d + 1, :]
    m_new = jnp.maximum(g + m_old, m_loc)
    a_sc = jnp.exp(g + m_old - m_new)
    b_sc = jnp.exp(m_loc - m_new)
    for h in range(MLSTM_HEADS):
        hp = MLSTM_HEADS * d + h
        lanes = slice(h * HEAD_DIM, (h + 1) * HEAD_DIM)
        qh = q[:, lanes]
        vh = v[:, lanes]
        ks = k[:, lanes].astype(F32) * QK_SCALE
        bc = b[:, hp:hp + 1]
        dm = jnp.where(mask, bc + a_t[hp:hp + 1, :], -jnp.inf)
        inter_log = bc + m_old[:, hp:hp + 1]
        m_t = jnp.maximum(inter_log, jnp.max(dm, axis=-1, keepdims=True))
        p = jnp.exp(dm - m_t)
        inter_w = jnp.exp(inter_log - m_t)
        qk = lax.dot_general(qh, ks.astype(BF16), (((1,), (1,)), ((), ())), preferred_element_type=F32)
        a_mat = p * qk
        c_prev = c_ref[hp]
        n_prev = n_ref[hp:hp + 1, :]
        num = (jnp.dot(a_mat.astype(BF16), vh, preferred_element_type=F32)
               + inter_w * jnp.dot(qh, c_prev.astype(BF16), preferred_element_type=F32))
        den = (jnp.sum(a_mat, axis=-1, keepdims=True)
               + inter_w * jnp.sum(qh.astype(F32) * n_prev, axis=-1, keepdims=True))
        hout = num / jnp.maximum(jnp.abs(den), jnp.exp(-m_t))
        out_ref[rows, lanes] = hout.astype(out_ref.dtype)
        ke = ks * e[:, hp:hp + 1]
        c_loc = jnp.dot(ke.T.astype(BF16), vh, preferred_element_type=F32)
        n_loc = jnp.sum(ke, axis=0, keepdims=True)
        ah = a_sc[:, hp:hp + 1]
        bh = b_sc[:, hp:hp + 1]
        c_ref[hp] = ah * c_prev + bh * c_loc
        n_ref[hp:hp + 1, :] = ah * n_prev + bh * n_loc
    m_ref[d:d + 1, :] = m_new


def _mlstm_kernel(qf_ref, kf_ref, vf_ref, gf_ref, qb_ref, kb_ref, vb_ref, gb_ref, ib_ref, fb_ref,
                  hf_ref, hb_ref, c_ref, n_ref, m_ref):
    L = MLSTM_CHUNK
    n_sub = qf_ref.shape[0] // L

    @pl.when(pl.program_id(1) == 0)
    def _():
        c_ref[...] = jnp.zeros_like(c_ref)
        n_ref[...] = jnp.zeros_like(n_ref)
        m_ref[...] = jnp.zeros_like(m_ref)

    row = lax.broadcasted_iota(jnp.int32, (L, L), 0)
    col = lax.broadcasted_iota(jnp.int32, (L, L), 1)
    mask_f = col <= row
    mask_b = col >= row
    tri_f = jnp.where(mask_f, 1.0, 0.0).astype(F32)
    tri_b = jnp.where(mask_b, 1.0, 0.0).astype(F32)
    ib = ib_ref[...]
    fb = fb_ref[...]

    def body(j, carry):
        rf = pl.ds(pl.multiple_of(j * L, L), L)
        rb = pl.ds(pl.multiple_of((n_sub - 1 - j) * L, L), L)
        _mlstm_chunk(0, qf_ref[rf, :], kf_ref[rf, :], vf_ref[rf, :], gf_ref[rf, :HEAD_DIM], gf_ref[rf, HEAD_DIM:],
                     ib, fb, tri_f, mask_f, c_ref, n_ref, m_ref, hf_ref, rf)
        _mlstm_chunk(1, qb_ref[rb, :], kb_ref[rb, :], vb_ref[rb, :], gb_ref[rb, :HEAD_DIM], gb_ref[rb, HEAD_DIM:],
                     ib, fb, tri_b, mask_b, c_ref, n_ref, m_ref, hb_ref, rb)
        return carry

    lax.fori_loop(0, n_sub, body, 0)


def _mlstm(u3, gates, ib, fb, B, S, n_sub=4):
    T = B * S
    rows = n_sub * MLSTM_CHUNK
    nblk = S // rows
    u4 = u3.reshape(N_COLBLK, B, S, GROUP_W)
    g3 = gates.reshape(B, S, GATE_W)

    def fwd(blk):
        return pl.BlockSpec((None, None, rows, GROUP_W), lambda b, c: (blk, b, c, 0))

    def bwd(blk):
        return pl.BlockSpec((None, None, rows, GROUP_W), lambda b, c: (blk, b, nblk - 1 - c, 0))

    hf, hb = pl.pallas_call(
        _mlstm_kernel,
        grid=(B, nblk),
        in_specs=[
            fwd(BLK_MQ), fwd(BLK_MK), fwd(BLK_MV),
            pl.BlockSpec((None, rows, GATE_W), lambda b, c: (b, c, 0)),
            bwd(BLK_MQ), bwd(BLK_MK), bwd(BLK_MV),
            pl.BlockSpec((None, rows, GATE_W), lambda b, c: (b, nblk - 1 - c, 0)),
            pl.BlockSpec((1, HEAD_DIM), lambda b, c: (0, 0)),
            pl.BlockSpec((1, HEAD_DIM), lambda b, c: (0, 0)),
        ],
        out_specs=[
            pl.BlockSpec((None, rows, GROUP_W), lambda b, c: (b, c, 0)),
            pl.BlockSpec((None, rows, GROUP_W), lambda b, c: (b, nblk - 1 - c, 0)),
        ],
        out_shape=[jax.ShapeDtypeStruct((B, S, GROUP_W), BF16)] * 2,
        scratch_shapes=[
            pltpu.VMEM((2 * MLSTM_HEADS, HEAD_DIM, HEAD_DIM), F32),
            pltpu.VMEM((2 * MLSTM_HEADS, HEAD_DIM), F32),
            pltpu.VMEM((8, HEAD_DIM), F32),
        ],
        compiler_params=_params(("parallel", "arbitrary")),
        name="mlstm",
    )(u4, u4, u4, g3, u4, u4, u4, g3, ib, fb)
    return hf.reshape(T, GROUP_W), hb.reshape(T, GROUP_W)


def _rope_tables(S):
    rows = S // GRID_W
    row = jnp.repeat(jnp.arange(rows), GRID_W).astype(F32)
    colp = jnp.tile(jnp.arange(GRID_W), rows).astype(F32)
    inv = ROPE_THETA ** (-jnp.arange(ROPE_FREQS, dtype=F32) / ROPE_FREQS)
    ar = row[:, None] * inv
    ac = colp[:, None] * inv
    ang = jnp.concatenate([ar, ar, ac, ac], axis=1)
    first_half = (jnp.arange(HEAD_DIM) % (2 * ROPE_FREQS)) < ROPE_FREQS
    return jnp.cos(ang), jnp.where(first_half, -1.0, 1.0) * jnp.sin(ang)


def _rope(x, cos, sin_signed, first_half):
    partner = jnp.where(first_half, pltpu.roll(x, HEAD_DIM - ROPE_FREQS, axis=1), pltpu.roll(x, ROPE_FREQS, axis=1))
    return x * cos + partner * sin_signed


def _attn_prep_kernel(aq_ref, akv_ref, cos_ref, sin_ref, qg_ref, kg_ref, q_ref, kt_ref, va_ref, kn_ref):
    cos = cos_ref[...]
    sin = sin_ref[...]
    lane = lax.broadcasted_iota(jnp.int32, cos.shape, 1)
    first_half = (lane % (2 * ROPE_FREQS)) < ROPE_FREQS
    qg = qg_ref[...]
    kg = kg_ref[...]
    for h in range(4):
        lanes = slice(h * HEAD_DIM, (h + 1) * HEAD_DIM)
        qh = _head_norm(aq_ref[:, lanes].astype(F32)) * qg
        q_ref[:, lanes] = (_rope(qh, cos, sin, first_half) * (QK_SCALE * LOG2_E)).astype(BF16)
    for h in range(ATTN_KV_HEADS):
        lanes = slice(h * HEAD_DIM, (h + 1) * HEAD_DIM)
        kh = _head_norm(akv_ref[:, lanes].astype(F32)) * kg
        kb = _rope(kh, cos, sin, first_half).astype(BF16)
        kt_ref[h] = kb.astype(F32).T.astype(BF16)
        k2 = jnp.max(jnp.sum(jnp.square(kb.astype(F32)), axis=-1, keepdims=True), axis=0, keepdims=True)
        kn_ref[h] = jnp.broadcast_to(k2, kn_ref.shape[1:])
        vl = slice((ATTN_KV_HEADS + h) * HEAD_DIM, (ATTN_KV_HEADS + h + 1) * HEAD_DIM)
        va_ref[h, :, :HEAD_DIM] = akv_ref[:, vl]
        va_ref[h, :, HEAD_DIM:] = jnp.ones((akv_ref.shape[0], HEAD_DIM), BF16)


def _flash_kernel(q_ref, kt_ref, va_ref, kn_ref, o_ref, m_ref, acc_ref, p_ref, mp_ref):
    tq = q_ref.shape[0]
    nkc, _, tk = kt_ref.shape
    n_rep = tk // HEAD_DIM
    q2 = jnp.concatenate([q_ref[:, :HEAD_DIM], q_ref[:, HEAD_DIM:]], axis=0)

    def chunk(c):
        return kt_ref[c], va_ref[pl.ds(pl.multiple_of(c * tk, tk), tk), :]

    def lane_max(s):
        part = s[:, :HEAD_DIM]
        for t in range(1, n_rep):
            part = jnp.maximum(part, s[:, t * HEAD_DIM:(t + 1) * HEAD_DIM])
        return part

    kt, vc = chunk(0)
    s = jnp.dot(q2, kt, preferred_element_type=F32)
    m0 = jnp.broadcast_to(jnp.max(lane_max(s), axis=-1, keepdims=True), m_ref.shape)
    m_ref[...] = m0
    acc_ref[...] = jnp.dot(jnp.exp2(s - jnp.tile(m0, (1, n_rep))).astype(BF16), vc, preferred_element_type=F32)

    qf = q2.astype(F32)
    q_norm = jnp.sqrt(jnp.sum(qf * qf, axis=-1, keepdims=True))
    k_norm = jnp.sqrt(jnp.max(jnp.max(kn_ref[...], axis=0), axis=0, keepdims=True))
    safe = jnp.max(q_norm * k_norm - m0) <= FRAME_MARGIN

    def fast_scores(c, slot):
        kt, _ = chunk(c)
        s = jnp.dot(q2, kt, preferred_element_type=F32)
        p_ref[slot] = jnp.exp2(s - jnp.tile(m_ref[...], (1, n_rep))).astype(BF16)
        mp_ref[slot] = lane_max(s)

    def fast_accumulate(c, slot):
        _, vc = chunk(c)
        m_prev = m_ref[...]
        m_new = jnp.maximum(m_prev, jnp.max(mp_ref[slot], axis=-1, keepdims=True))
        alpha = jnp.exp2(m_prev - m_new)
        acc_ref[...] = (acc_ref[...] + jnp.dot(p_ref[slot], vc, preferred_element_type=F32)) * jnp.tile(alpha, (1, 2))
        m_ref[...] = m_new

    def fast():
        def pair(i, carry):
            c = 2 * i + 1
            fast_scores(c, 1)
            fast_accumulate(c, 1)
            fast_scores(c + 1, 0)
            fast_accumulate(c + 1, 0)
            return carry

        lax.fori_loop(0, (nkc - 2) // 2, pair, 0)
        fast_scores(nkc - 1, 1)
        fast_accumulate(nkc - 1, 1)

    def exact():
        def body(c, carry):
            kt, vc = chunk(c)
            s = jnp.dot(q2, kt, preferred_element_type=F32)
            m_prev = m_ref[...]
            m_new = jnp.maximum(m_prev, jnp.max(lane_max(s), axis=-1, keepdims=True))
            p = jnp.exp2(s - jnp.tile(m_new, (1, n_rep))).astype(BF16)
            acc_ref[...] = (jnp.tile(jnp.exp2(m_prev - m_new), (1, 2)) * acc_ref[...]
                            + jnp.dot(p, vc, preferred_element_type=F32))
            m_ref[...] = m_new
            return carry

        lax.fori_loop(1, nkc, body, 0)

    lax.cond(safe, fast, exact)
    o = acc_ref[:, :HEAD_DIM] / acc_ref[:, HEAD_DIM:]
    o_ref[:, :HEAD_DIM] = o[:tq].astype(o_ref.dtype)
    o_ref[:, HEAD_DIM:] = o[tq:].astype(o_ref.dtype)


def _axial_gqa(u3, q_gain, k_gain, rope, B, S, tq=512, tk=1024):
    T = B * S
    cos, sin = rope
    u4 = u3.reshape(N_COLBLK, B, S, GROUP_W)
    tk = min(tk, S // 2)
    nkc = S // tk
    assert nkc >= 2 and nkc % 2 == 0, "flash kernel pipelines kv chunks in pairs"
    q_r, kt, va, kn = pl.pallas_call(
        _attn_prep_kernel,
        grid=(B, nkc),
        in_specs=[
            pl.BlockSpec((None, None, tk, GROUP_W), lambda b, i: (BLK_AQ, b, i, 0)),
            pl.BlockSpec((None, None, tk, GROUP_W), lambda b, i: (BLK_AKV, b, i, 0)),
            pl.BlockSpec((tk, HEAD_DIM), lambda b, i: (i, 0)),
            pl.BlockSpec((tk, HEAD_DIM), lambda b, i: (i, 0)),
            pl.BlockSpec((1, HEAD_DIM), lambda b, i: (0, 0)),
            pl.BlockSpec((1, HEAD_DIM), lambda b, i: (0, 0)),
        ],
        out_specs=[
            pl.BlockSpec((None, tk, GROUP_W), lambda b, i: (b, i, 0)),
            pl.BlockSpec((None, ATTN_KV_HEADS, None, HEAD_DIM, tk), lambda b, i: (b, 0, i, 0, 0)),
            pl.BlockSpec((None, ATTN_KV_HEADS, tk, 2 * HEAD_DIM), lambda b, i: (b, 0, i, 0)),
            pl.BlockSpec((None, ATTN_KV_HEADS, None, 8, HEAD_DIM), lambda b, i: (b, 0, i, 0, 0)),
        ],
        out_shape=[
            jax.ShapeDtypeStruct((B, S, GROUP_W), BF16),
            jax.ShapeDtypeStruct((B, ATTN_KV_HEADS, nkc, HEAD_DIM, tk), BF16),
            jax.ShapeDtypeStruct((B, ATTN_KV_HEADS, S, 2 * HEAD_DIM), BF16),
            jax.ShapeDtypeStruct((B, ATTN_KV_HEADS, nkc, 8, HEAD_DIM), F32),
        ],
        compiler_params=_params(("parallel", "parallel")),
        name="attn_prep",
    )(u4, u4, cos, sin, q_gain, k_gain)
    out = pl.pallas_call(
        _flash_kernel,
        grid=(B, ATTN_KV_HEADS, S // tq),
        in_specs=[
            pl.BlockSpec((None, tq, 2 * HEAD_DIM), lambda b, k, i: (b, i, k)),
            pl.BlockSpec((None, None, nkc, HEAD_DIM, tk), lambda b, k, i: (b, k, 0, 0, 0)),
            pl.BlockSpec((None, None, S, 2 * HEAD_DIM), lambda b, k, i: (b, k, 0, 0)),
            pl.BlockSpec((None, None, nkc, 8, HEAD_DIM), lambda b, k, i: (b, k, 0, 0, 0)),
        ],
        out_specs=pl.BlockSpec((None, tq, 2 * HEAD_DIM), lambda b, k, i: (b, i, k)),
        out_shape=jax.ShapeDtypeStruct((B, S, GROUP_W), BF16),
        scratch_shapes=[
            pltpu.VMEM((2 * tq, HEAD_DIM), F32),
            pltpu.VMEM((2 * tq, 2 * HEAD_DIM), F32),
            pltpu.VMEM((2, 2 * tq, tk), BF16),
            pltpu.VMEM((2, 2 * tq, HEAD_DIM), F32),
        ],
        compiler_params=_params(("parallel", "parallel", "arbitrary")),
        name="flash_gqa",
    )(q_r, kt, va, kn)
    return out.reshape(T, GROUP_W)


def _combine_kernel(x_ref, ya_ref, cb_ref, cc_ref, cx_ref, ccp_ref, cxp_ref, ccn_ref, cxn_ref, mo_ref,
                    hf_ref, hb_ref, yd_ref, cw_ref, cbias_ref, gain_ref, w_ref, o_ref, y_ref, *, tiles_per_seq):
    tm = x_ref.shape[0]
    i = pl.program_id(0)
    halo = ccp_ref.shape[0]
    at_start = (i % tiles_per_seq) == 0
    at_end = (i % tiles_per_seq) == tiles_per_seq - 1
    gain = gain_ref[...]

    z = cc_ref[...].astype(F32) * cx_ref[...].astype(F32)
    z_before = ccp_ref[halo - 1:halo, :].astype(F32) * cxp_ref[halo - 1:halo, :].astype(F32)
    z_after = ccn_ref[0:1, :].astype(F32) * cxn_ref[0:1, :].astype(F32)
    z_before = jnp.where(at_start, 0.0, z_before)
    z_after = jnp.where(at_end, 0.0, z_after)
    rid = lax.broadcasted_iota(jnp.int32, z.shape, 0)
    z_m1 = jnp.where(rid == 0, z_before, pltpu.roll(z, 1, axis=0))
    z_p1 = jnp.where(rid == tm - 1, z_after, pltpu.roll(z, tm - 1, axis=0))
    conv = z_m1 * cw_ref[0:1, :] + z * cw_ref[1:2, :] + z_p1 * cw_ref[2:3, :] + cbias_ref[...]
    yb = cb_ref[...].astype(F32) * conv

    for grp in range(4):
        lanes = slice(grp * HEAD_DIM, (grp + 1) * HEAD_DIM)
        y_ref[:, lanes] = (ya_ref[:, lanes].astype(F32) * gain[:, lanes]).astype(BF16)
        gl = slice(GROUP_W + grp * HEAD_DIM, GROUP_W + (grp + 1) * HEAD_DIM)
        y_ref[:, gl] = (_head_norm(yb[:, lanes]) * gain[:, gl]).astype(BF16)
        gl = slice(2 * GROUP_W + grp * HEAD_DIM, 2 * GROUP_W + (grp + 1) * HEAD_DIM)
        yc = _head_norm(hf_ref[:, lanes].astype(F32) + hb_ref[:, lanes].astype(F32))
        y_ref[:, gl] = (jax.nn.sigmoid(mo_ref[:, lanes].astype(F32)) * yc * gain[:, gl]).astype(BF16)
        gl = slice(3 * GROUP_W + grp * HEAD_DIM, 3 * GROUP_W + (grp + 1) * HEAD_DIM)
        y_ref[:, gl] = (_head_norm(yd_ref[:, lanes].astype(F32)) * gain[:, gl]).astype(BF16)

    o_ref[...] = x_ref[...] + jnp.dot(y_ref[...], w_ref[...], preferred_element_type=F32)


def _combine_out(x2, ya, u3, hf, hb, yd, conv_w, conv_b, out_gain, w_out, S, tm=512, halo=16):
    T = x2.shape[0]
    hb_per_tile = tm // halo
    n_halo = T // halo

    def u_blk(blk):
        return pl.BlockSpec((None, tm, GROUP_W), lambda i: (blk, i, 0))

    def u_prev(blk):
        return pl.BlockSpec((None, halo, GROUP_W), lambda i: (blk, jnp.maximum(i * hb_per_tile - 1, 0), 0))

    def u_next(blk):
        return pl.BlockSpec((None, halo, GROUP_W), lambda i: (blk, jnp.minimum((i + 1) * hb_per_tile, n_halo - 1), 0))

    tok = pl.BlockSpec((tm, GROUP_W), lambda i: (i, 0))
    return pl.pallas_call(
        functools.partial(_combine_kernel, tiles_per_seq=S // tm),
        grid=(T // tm,),
        in_specs=[
            pl.BlockSpec((tm, D_MODEL), lambda i: (i, 0)),
            tok,
            u_blk(BLK_CB), u_blk(BLK_CC), u_blk(BLK_CX),
            u_prev(BLK_CC), u_prev(BLK_CX), u_next(BLK_CC), u_next(BLK_CX),
            u_blk(BLK_MO),
            tok, tok, tok,
            pl.BlockSpec((8, GROUP_W), lambda i: (0, 0)),
            pl.BlockSpec((1, GROUP_W), lambda i: (0, 0)),
            pl.BlockSpec((1, D_MODEL), lambda i: (0, 0)),
            pl.BlockSpec((D_MODEL, D_MODEL), lambda i: (0, 0)),
        ],
        out_specs=pl.BlockSpec((tm, D_MODEL), lambda i: (i, 0)),
        out_shape=jax.ShapeDtypeStruct((T, D_MODEL), F32),
        scratch_shapes=[pltpu.VMEM((tm, D_MODEL), BF16)],
        compiler_params=_params(("parallel",)),
        name="combine_out",
    )(x2, ya, u3, u3, u3, u3, u3, u3, u3, u3, hf, hb, yd, conv_w, conv_b, out_gain, w_out)


def _mem_kv_kernel(mem_ref, g_ref, w_ref, kg_ref, kt_ref, v_ref, h_ref):
    _norm_rows_to(mem_ref, g_ref, h_ref)
    kv = jnp.dot(h_ref[...], w_ref[...], preferred_element_type=F32)
    kg = kg_ref[...]
    for h in range(CA_HEADS):
        lanes = slice(h * HEAD_DIM, (h + 1) * HEAD_DIM)
        kt_ref[h] = (_head_norm(kv[:, lanes]) * kg).T.astype(BF16)
    v_ref[...] = kv[:, CA_HEADS * HEAD_DIM:].astype(BF16)


def _mem_kv(mem, g_mem, w_kv, k_gain):
    B = mem.shape[0]
    return pl.pallas_call(
        _mem_kv_kernel,
        grid=(B,),
        in_specs=[
            pl.BlockSpec((None, N_MEM, D_MODEL), lambda b: (b, 0, 0)),
            pl.BlockSpec((1, D_MODEL), lambda b: (0, 0)),
            pl.BlockSpec((D_MODEL, 2 * GROUP_W), lambda b: (0, 0)),
            pl.BlockSpec((1, HEAD_DIM), lambda b: (0, 0)),
        ],
        out_specs=[
            pl.BlockSpec((None, CA_HEADS, HEAD_DIM, N_MEM), lambda b: (b, 0, 0, 0)),
            pl.BlockSpec((None, N_MEM, GROUP_W), lambda b: (b, 0, 0)),
        ],
        out_shape=[
            jax.ShapeDtypeStruct((B, CA_HEADS, HEAD_DIM, N_MEM), BF16),
            jax.ShapeDtypeStruct((B, N_MEM, GROUP_W), BF16),
        ],
        scratch_shapes=[pltpu.VMEM((N_MEM, D_MODEL), BF16)],
        compiler_params=_params(("parallel",)),
        name="mem_kv",
    )(mem, g_mem, w_kv, k_gain)


def _cross_attn_kernel(x_ref, g_ref, wq_ref, qg_ref, kt_ref, v_ref, wo_ref, o_ref, h_ref, oc_ref):
    _norm_rows_to(x_ref, g_ref, h_ref)
    q = jnp.dot(h_ref[...], wq_ref[...], preferred_element_type=F32)
    qg = qg_ref[...]
    for h in range(CA_HEADS):
        lanes = slice(h * HEAD_DIM, (h + 1) * HEAD_DIM)
        qh = (_head_norm(q[:, lanes]) * (qg * QK_SCALE)).astype(BF16)
        s = jnp.dot(qh, kt_ref[h], preferred_element_type=F32)
        e = jnp.exp(s - jnp.max(s, axis=-1, keepdims=True))
        o = jnp.dot(e.astype(BF16), v_ref[:, lanes], preferred_element_type=F32)
        oc_ref[:, lanes] = (o * pl.reciprocal(jnp.sum(e, axis=-1, keepdims=True), approx=False)).astype(BF16)
    o_ref[...] = x_ref[...] + jnp.dot(oc_ref[...], wo_ref[...], preferred_element_type=F32)


def _cross_attn(x2, g_ca, w_q, q_gain, kt, v, w_o, B, S, tm=1024):
    T = x2.shape[0]
    x3 = x2.reshape(B, S, D_MODEL)
    out = pl.pallas_call(
        _cross_attn_kernel,
        grid=(B, S // tm),
        in_specs=[
            pl.BlockSpec((None, tm, D_MODEL), lambda b, i: (b, i, 0)),
            pl.BlockSpec((1, D_MODEL), lambda b, i: (0, 0)),
            pl.BlockSpec((D_MODEL, GROUP_W), lambda b, i: (0, 0)),
            pl.BlockSpec((1, HEAD_DIM), lambda b, i: (0, 0)),
            pl.BlockSpec((None, CA_HEADS, HEAD_DIM, N_MEM), lambda b, i: (b, 0, 0, 0)),
            pl.BlockSpec((None, N_MEM, GROUP_W), lambda b, i: (b, 0, 0)),
            pl.BlockSpec((GROUP_W, D_MODEL), lambda b, i: (0, 0)),
        ],
        out_specs=pl.BlockSpec((None, tm, D_MODEL), lambda b, i: (b, i, 0)),
        out_shape=jax.ShapeDtypeStruct((B, S, D_MODEL), F32),
        scratch_shapes=[pltpu.VMEM((tm, D_MODEL), BF16), pltpu.VMEM((tm, GROUP_W), BF16)],
        compiler_params=_params(("parallel", "parallel")),
        name="cross_attn",
    )(x3, g_ca, w_q, q_gain, kt, v, w_o)
    return out.reshape(T, D_MODEL)


def _mlp_kernel(x_ref, g_ref, wu_ref, wd_ref, o_ref, h_ref):
    @pl.when(pl.program_id(1) == 0)
    def _():
        _norm_rows_to(x_ref, g_ref, h_ref)
        o_ref[...] = x_ref[...]

    up = jnp.dot(h_ref[...], wu_ref[...], preferred_element_type=F32)
    act = jnp.square(jnp.maximum(up, 0.0)).astype(BF16)
    o_ref[...] += jnp.dot(act, wd_ref[...], preferred_element_type=F32)


def _mlp(x2, g, w_up, w_down, tm=512, tf=1024):
    T = x2.shape[0]
    return pl.pallas_call(
        _mlp_kernel,
        grid=(T // tm, D_FF // tf),
        in_specs=[
            pl.BlockSpec((tm, D_MODEL), lambda i, j: (i, 0)),
            pl.BlockSpec((1, D_MODEL), lambda i, j: (0, 0)),
            pl.BlockSpec((D_MODEL, tf), lambda i, j: (0, j)),
            pl.BlockSpec((tf, D_MODEL), lambda i, j: (j, 0)),
        ],
        out_specs=pl.BlockSpec((tm, D_MODEL), lambda i, j: (i, 0)),
        out_shape=jax.ShapeDtypeStruct((T, D_MODEL), F32),
        scratch_shapes=[pltpu.VMEM((tm, D_MODEL), BF16)],
        compiler_params=_params(("parallel", "arbitrary")),
        name="mlp",
    )(x2, g, w_up, w_down)


def _prep_weights(w_in, conv_w, i_bias, f_bias, w_out, w_ca_q, w_ca_kv, w_ca_o, w_up, w_down):
    n_main = 8 * GROUP_W
    w_main = jnp.concatenate([w_in[:, :, :n_main], w_in[:, :, n_main + 16:]], axis=2).astype(BF16)
    pad = jnp.zeros((DEPTH, D_MODEL, HEAD_DIM - 8), F32)
    w_gate = jnp.concatenate([w_in[:, :, n_main:n_main + 8], pad, w_in[:, :, n_main + 8:n_main + 16], pad],
                             axis=2).astype(BF16)
    lane_pad = jnp.zeros((DEPTH, 1, HEAD_DIM - 8), F32)
    ib = jnp.concatenate([i_bias.reshape(DEPTH, 1, 8), lane_pad], axis=2)
    fb = jnp.concatenate([f_bias.reshape(DEPTH, 1, 8), lane_pad], axis=2)
    cw = jnp.concatenate([conv_w, jnp.zeros((DEPTH, 5, GROUP_W), F32)], axis=1)
    return dict(w_main=w_main, w_gate=w_gate, ib=ib, fb=fb, cw=cw, w_out=w_out.astype(BF16),
                w_ca_q=w_ca_q.astype(BF16), w_ca_kv=w_ca_kv.astype(BF16), w_ca_o=w_ca_o.astype(BF16),
                w_up=w_up.astype(BF16), w_down=w_down.astype(BF16))


def _trunk(x, mem, pw, g_mix, conv_b, attn_q_norm, attn_k_norm, out_gain, g_ca, g_mem, ca_q_norm, ca_k_norm, g_mlp):
    B, S, _ = x.shape
    T = B * S
    x2 = x.reshape(T, D_MODEL)
    rope = _rope_tables(S)
    ftab = _fourier_tables(S)
    for l in range(DEPTH):
        u3, gates = _norm_proj(x2, g_mix[l][None], pw["w_main"][l], pw["w_gate"][l])
        ya = _fourier(u3, B, S, ftab)
        hf, hb = _mlstm(u3, gates, pw["ib"][l], pw["fb"][l], B, S)
        yd = _axial_gqa(u3, attn_q_norm[l][None], attn_k_norm[l][None], rope, B, S)
        x2 = _combine_out(x2, ya, u3, hf, hb, yd, pw["cw"][l], conv_b[l][None], out_gain[l][None], pw["w_out"][l], S)
        kt, v = _mem_kv(mem, g_mem[l][None], pw["w_ca_kv"][l], ca_k_norm[l][None])
        x2 = _cross_attn(x2, g_ca[l][None], pw["w_ca_q"][l], ca_q_norm[l][None], kt, v, pw["w_ca_o"][l], B, S)
        x2 = _mlp(x2, g_mlp[l][None], pw["w_up"][l], pw["w_down"][l])
    return x2.reshape(B, S, D_MODEL)


def kernel(x_prompt, x_sample, mem_prompt, mem_sample, g_mix, w_in, conv_w, conv_b, i_bias, f_bias, attn_q_norm,
           attn_k_norm, out_gain, w_out, g_ca, g_mem, w_ca_q, w_ca_kv, ca_q_norm, ca_k_norm, w_ca_o, g_mlp, w_up,
           w_down):
    pw = _prep_weights(w_in, conv_w, i_bias, f_bias, w_out, w_ca_q, w_ca_kv, w_ca_o, w_up, w_down)
    args = (pw, g_mix, conv_b, attn_q_norm, attn_k_norm, out_gain, g_ca, g_mem, ca_q_norm, ca_k_norm, g_mlp)
    y_prompt = _trunk(x_prompt, mem_prompt, *args)
    y_sample = _trunk(x_sample, mem_sample, *args)
    return (y_prompt, y_sample)
```

```python
import functools
import math

import jax
import jax.numpy as jnp
from jax import lax
from jax.experimental import pallas as pl
from jax.experimental.pallas import tpu as pltpu

F32 = jnp.float32
BF16 = jnp.bfloat16

D_MODEL = 2048
DEPTH = 4
HEAD_DIM = 128
GROUP_W = 4 * HEAD_DIM
N_COLBLK = 10
GATE_W = 2 * HEAD_DIM
MLSTM_HEADS = 4
MLSTM_CHUNK = 128
ATTN_KV_HEADS = 2
GRID_W = 64
ROPE_THETA = 10000.0
ROPE_FREQS = HEAD_DIM // 4
CA_HEADS = 4
N_MEM = 256
D_FF = 4 * D_MODEL
RMS_EPS = 1e-6
QK_SCALE = HEAD_DIM ** -0.5
LOG2_E = math.log2(math.e)
FRAME_MARGIN = 100.0

BLK_FOURIER, BLK_CB, BLK_CC, BLK_CX, BLK_MQ, BLK_MK, BLK_MV, BLK_MO, BLK_AQ, BLK_AKV = range(N_COLBLK)

V7X_VMEM_LIMIT = 56 * 1024 * 1024
NORM_ROWS = 256


def _params(semantics, vmem=V7X_VMEM_LIMIT):
    return pltpu.CompilerParams(dimension_semantics=semantics, vmem_limit_bytes=vmem)


def _head_norm(y):
    return y * lax.rsqrt(jnp.mean(y * y, axis=-1, keepdims=True) + RMS_EPS)


def _norm_rows_to(x_ref, g_ref, h_ref):
    tm = x_ref.shape[0]
    g = g_ref[...]

    def body(r, carry):
        rows = pl.ds(pl.multiple_of(r * NORM_ROWS, NORM_ROWS), NORM_ROWS)
        xf = x_ref[rows, :]
        ms = jnp.mean(xf * xf, axis=-1, keepdims=True)
        h_ref[rows, :] = (xf * lax.rsqrt(ms + RMS_EPS) * g).astype(BF16)
        return carry

    lax.fori_loop(0, tm // NORM_ROWS, body, 0)


def _norm_proj_kernel(x_ref, g_ref, w_ref, wg_ref, u_ref, gate_ref, h_ref):
    @pl.when(pl.program_id(1) == 0)
    def _():
        _norm_rows_to(x_ref, g_ref, h_ref)
        gate_ref[...] = jnp.dot(h_ref[...], wg_ref[...], preferred_element_type=F32)

    res = jnp.dot(h_ref[...], w_ref[...], preferred_element_type=F32)
    for k in range(u_ref.shape[0]):
        u_ref[k] = res[:, k * GROUP_W:(k + 1) * GROUP_W].astype(BF16)


def _norm_proj(x2, g, w_main, w_gate, tm=1024, nb=2):
    T = x2.shape[0]
    tn = nb * GROUP_W
    return pl.pallas_call(
        _norm_proj_kernel,
        grid=(T // tm, N_COLBLK // nb),
        in_specs=[
            pl.BlockSpec((tm, D_MODEL), lambda i, j: (i, 0)),
            pl.BlockSpec((1, D_MODEL), lambda i, j: (0, 0)),
            pl.BlockSpec((D_MODEL, tn), lambda i, j: (0, j)),
            pl.BlockSpec((D_MODEL, GATE_W), lambda i, j: (0, 0)),
        ],
        out_specs=[
            pl.BlockSpec((nb, tm, GROUP_W), lambda i, j: (j, i, 0)),
            pl.BlockSpec((tm, GATE_W), lambda i, j: (i, 0)),
        ],
        out_shape=[
            jax.ShapeDtypeStruct((N_COLBLK, T, GROUP_W), BF16),
            jax.ShapeDtypeStruct((T, GATE_W), F32),
        ],
        scratch_shapes=[pltpu.VMEM((tm, D_MODEL), BF16)],
        compiler_params=_params(("parallel", "arbitrary")),
        name="norm_proj",
    )(x2, g, w_main, w_gate)


def _fourier_tables(S):
    n2 = HEAD_DIM
    n1 = S // n2
    i1 = jnp.arange(n1, dtype=jnp.int32)
    a1 = (2.0 * math.pi / n1) * ((i1[:, None] * i1[None, :]) % n1).astype(F32)
    f1 = jnp.concatenate([jnp.cos(a1), -jnp.sin(a1)], axis=0).astype(BF16)
    i2 = jnp.arange(n2, dtype=jnp.int32)
    p = i1[:, None, None] + n1 * i2[None, :, None]
    k = (p * i2[None, None, :]) % S
    a2 = (2.0 * math.pi / S) * k.astype(F32)
    c2, s2 = jnp.cos(a2), jnp.sin(a2)
    g = jnp.concatenate([jnp.concatenate([c2, s2], axis=2),
                         jnp.concatenate([-s2, c2], axis=2)], axis=1).astype(BF16)
    ac = (2.0 * math.pi / HEAD_DIM) * ((i2[:, None] * i2[None, :]) % HEAD_DIM).astype(F32)
    cs = jnp.concatenate([jnp.cos(ac), jnp.sin(ac)], axis=0).astype(BF16)
    return f1, g, cs


def _fourier_stage1_kernel(a_ref, f1_ref, y_ref):
    n1 = a_ref.shape[0]
    res = jnp.dot(f1_ref[...], a_ref[...], preferred_element_type=F32)
    y_ref[0] = res[:n1].astype(BF16)
    y_ref[1] = res[n1:].astype(BF16)


def _fourier_stage2_kernel(y_ref, g_ref, cs_ref, o_ref, *, scale):
    n_p = y_ref.shape[1]
    cs = cs_ref[...]
    for p in range(n_p):
        ycat = jnp.concatenate([y_ref[0, p], y_ref[1, p]], axis=0)
        x = jnp.dot(g_ref[p], ycat, preferred_element_type=F32).astype(BF16)
        for grp in range(4):
            lanes = slice(grp * HEAD_DIM, (grp + 1) * HEAD_DIM)
            xg = jnp.concatenate([x[:HEAD_DIM, lanes], x[HEAD_DIM:, lanes]], axis=1)
            o = jnp.dot(xg, cs, preferred_element_type=F32) * scale
            col = p * GROUP_W + grp * HEAD_DIM
            o_ref[:, col:col + HEAD_DIM] = _head_norm(o).astype(BF16)


def _fourier(u3, B, S, tables):
    f1, g, cs = tables
    n2 = HEAD_DIM
    n1 = S // n2
    T = B * S
    ncol = min(16, n2)
    a_view = u3[BLK_FOURIER].reshape(B, n1, n2 * GROUP_W)
    y = pl.pallas_call(
        _fourier_stage1_kernel,
        grid=(B, n2 // ncol),
        in_specs=[
            pl.BlockSpec((None, n1, ncol * GROUP_W), lambda b, j: (b, 0, j)),
            pl.BlockSpec((2 * n1, n1), lambda b, j: (0, 0)),
        ],
        out_specs=pl.BlockSpec((None, 2, n1, ncol * GROUP_W), lambda b, j: (b, 0, 0, j)),
        out_shape=jax.ShapeDtypeStruct((B, 2, n1, n2 * GROUP_W), BF16),
        compiler_params=_params(("parallel", "parallel")),
        name="fourier_stage1",
    )(a_view, f1)
    y5 = y.reshape(B, 2, n1, n2, GROUP_W)
    n_p = 8
    out = pl.pallas_call(
        functools.partial(_fourier_stage2_kernel, scale=1.0 / math.sqrt(S * HEAD_DIM)),
        grid=(B, n1 // n_p),
        in_specs=[
            pl.BlockSpec((None, 2, n_p, n2, GROUP_W), lambda b, j: (b, 0, j, 0, 0)),
            pl.BlockSpec((n_p, 2 * n2, 2 * n2), lambda b, j: (j, 0, 0)),
            pl.BlockSpec((2 * HEAD_DIM, HEAD_DIM), lambda b, j: (0, 0)),
        ],
        out_specs=pl.BlockSpec((None, n2, n_p * GROUP_W), lambda b, j: (b, 0, j)),
        out_shape=jax.ShapeDtypeStruct((B, n2, n1 * GROUP_W), BF16),
        compiler_params=_params(("parallel", "parallel")),
        name="fourier_stage2",
    )(y5, g, cs)
    return out.reshape(T, GROUP_W)


def _log_sigmoid(x):
    return jnp.minimum(x, 0.0) - jnp.log1p(jnp.exp(-jnp.abs(x)))


def _mlstm_chunk(d, q, k, v, gi, gf, ib, fb, tri, mask, c_ref, n_ref, m_ref, out_ref, rows):
    li = gi + ib
    lf = _log_sigmoid(gf + fb)
    b = jnp.dot(tri, lf, precision=lax.Precision.HIGHEST, preferred_element_type=F32)
    g = jnp.sum(lf, axis=0, keepdims=True)
    w_end = g - b + li
    m_loc = jnp.max(w_end, axis=0, keepdims=True)
    e = jnp.exp(w_end - m_loc)
    a_t = (li - b).T
    m_old = m_ref[d:d + 1, :]
    m_new = jnp.maximum(g + m_old, m_loc)
    a_sc = jnp.exp(g + m_old - m_new)
    b_sc = jnp.exp(m_loc - m_new)
    for h in range(MLSTM_HEADS):
        hp = MLSTM_HEADS * d + h
        lanes = slice(h * HEAD_DIM, (h + 1) * HEAD_DIM)
        qh = q[:, lanes]
        vh = v[:, lanes]
        ks = k[:, lanes].astype(F32) * QK_SCALE
        bc = b[:, hp:hp + 1]
        dm = jnp.where(mask, bc + a_t[hp:hp + 1, :], -jnp.inf)
        inter_log = bc + m_old[:, hp:hp + 1]
        m_t = jnp.maximum(inter_log, jnp.max(dm, axis=-1, keepdims=True))
        p = jnp.exp(dm - m_t)
        inter_w = jnp.exp(inter_log - m_t)
        qk = lax.dot_general(qh, ks.astype(BF16), (((1,), (1,)), ((), ())), preferred_element_type=F32)
        a_mat = p * qk
        c_prev = c_ref[hp]
        n_prev = n_ref[hp:hp + 1, :]
        num = (jnp.dot(a_mat.astype(BF16), vh, preferred_element_type=F32)
               + inter_w * jnp.dot(qh, c_prev.astype(BF16), preferred_element_type=F32))
        den = (jnp.sum(a_mat, axis=-1, keepdims=True)
               + inter_w * jnp.sum(qh.astype(F32) * n_prev, axis=-1, keepdims=True))
        hout = num / jnp.maximum(jnp.abs(den), jnp.exp(-m_t))
        out_ref[rows, lanes] = hout.astype(out_ref.dtype)
        ke = ks * e[:, hp:hp + 1]
        c_loc = jnp.dot(ke.T.astype(BF16), vh, preferred_element_type=F32)
        n_loc = jnp.sum(ke, axis=0, keepdims=True)
        ah = a_sc[:, hp:hp + 1]
        bh = b_sc[:, hp:hp + 1]
        c_ref[hp] = ah * c_prev + bh * c_loc
        n_ref[hp:hp + 1, :] = ah * n_prev + bh * n_loc
    m_ref[d:d + 1, :] = m_new


def _mlstm_kernel(qf_ref, kf_ref, vf_ref, gf_ref, qb_ref, kb_ref, vb_ref, gb_ref, ib_ref, fb_ref,
                  hf_ref, hb_ref, c_ref, n_ref, m_ref):
    L = MLSTM_CHUNK
    n_sub = qf_ref.shape[0] // L

    @pl.when(pl.program_id(1) == 0)
    def _():
        c_ref[...] = jnp.zeros_like(c_ref)
        n_ref[...] = jnp.zeros_like(n_ref)
        m_ref[...] = jnp.zeros_like(m_ref)

    row = lax.broadcasted_iota(jnp.int32, (L, L), 0)
    col = lax.broadcasted_iota(jnp.int32, (L, L), 1)
    mask_f = col <= row
    mask_b = col >= row
    tri_f = jnp.where(mask_f, 1.0, 0.0).astype(F32)
    tri_b = jnp.where(mask_b, 1.0, 0.0).astype(F32)
    ib = ib_ref[...]
    fb = fb_ref[...]

    def body(j, carry):
        rf = pl.ds(pl.multiple_of(j * L, L), L)
        rb = pl.ds(pl.multiple_of((n_sub - 1 - j) * L, L), L)
        _mlstm_chunk(0, qf_ref[rf, :], kf_ref[rf, :], vf_ref[rf, :], gf_ref[rf, :HEAD_DIM], gf_ref[rf, HEAD_DIM:],
                     ib, fb, tri_f, mask_f, c_ref, n_ref, m_ref, hf_ref, rf)
        _mlstm_chunk(1, qb_ref[rb, :], kb_ref[rb, :], vb_ref[rb, :], gb_ref[rb, :HEAD_DIM], gb_ref[rb, HEAD_DIM:],
                     ib, fb, tri_b, mask_b, c_ref, n_ref, m_ref, hb_ref, rb)
        return carry

    lax.fori_loop(0, n_sub, body, 0)


def _mlstm(u3, gates, ib, fb, B, S, n_sub=4):
    T = B * S
    rows = n_sub * MLSTM_CHUNK
    nblk = S // rows
    u4 = u3.reshape(N_COLBLK, B, S, GROUP_W)
    g3 = gates.reshape(B, S, GATE_W)

    def fwd(blk):
        return pl.BlockSpec((None, None, rows, GROUP_W), lambda b, c: (blk, b, c, 0))

    def bwd(blk):
        return pl.BlockSpec((None, None, rows, GROUP_W), lambda b, c: (blk, b, nblk - 1 - c, 0))

    hf, hb = pl.pallas_call(
        _mlstm_kernel,
        grid=(B, nblk),
        in_specs=[
            fwd(BLK_MQ), fwd(BLK_MK), fwd(BLK_MV),
            pl.BlockSpec((None, rows, GATE_W), lambda b, c: (b, c, 0)),
            bwd(BLK_MQ), bwd(BLK_MK), bwd(BLK_MV),
            pl.BlockSpec((None, rows, GATE_W), lambda b, c: (b, nblk - 1 - c, 0)),
            pl.BlockSpec((1, HEAD_DIM), lambda b, c: (0, 0)),
            pl.BlockSpec((1, HEAD_DIM), lambda b, c: (0, 0)),
        ],
        out_specs=[
            pl.BlockSpec((None, rows, GROUP_W), lambda b, c: (b, c, 0)),
            pl.BlockSpec((None, rows, GROUP_W), lambda b, c: (b, nblk - 1 - c, 0)),
        ],
        out_shape=[jax.ShapeDtypeStruct((B, S, GROUP_W), BF16)] * 2,
        scratch_shapes=[
            pltpu.VMEM((2 * MLSTM_HEADS, HEAD_DIM, HEAD_DIM), F32),
            pltpu.VMEM((2 * MLSTM_HEADS, HEAD_DIM), F32),
            pltpu.VMEM((8, HEAD_DIM), F32),
        ],
        compiler_params=_params(("parallel", "arbitrary")),
        name="mlstm",
    )(u4, u4, u4, g3, u4, u4, u4, g3, ib, fb)
    return hf.reshape(T, GROUP_W), hb.reshape(T, GROUP_W)


def _rope_tables(S):
    rows = S // GRID_W
    row = jnp.repeat(jnp.arange(rows), GRID_W).astype(F32)
    colp = jnp.tile(jnp.arange(GRID_W), rows).astype(F32)
    inv = ROPE_THETA ** (-jnp.arange(ROPE_FREQS, dtype=F32) / ROPE_FREQS)
    ar = row[:, None] * inv
    ac = colp[:, None] * inv
    ang = jnp.concatenate([ar, ar, ac, ac], axis=1)
    first_half = (jnp.arange(HEAD_DIM) % (2 * ROPE_FREQS)) < ROPE_FREQS
    return jnp.cos(ang), jnp.where(first_half, -1.0, 1.0) * jnp.sin(ang)


def _rope(x, cos, sin_signed, first_half):
    partner = jnp.where(first_half, pltpu.roll(x, HEAD_DIM - ROPE_FREQS, axis=1), pltpu.roll(x, ROPE_FREQS, axis=1))
    return x * cos + partner * sin_signed


def _attn_prep_kernel(aq_ref, akv_ref, cos_ref, sin_ref, qg_ref, kg_ref, q_ref, kt_ref, va_ref, kn_ref):
    cos = cos_ref[...]
    sin = sin_ref[...]
    lane = lax.broadcasted_iota(jnp.int32, cos.shape, 1)
    first_half = (lane % (2 * ROPE_FREQS)) < ROPE_FREQS
    qg = qg_ref[...]
    kg = kg_ref[...]
    for h in range(4):
        lanes = slice(h * HEAD_DIM, (h + 1) * HEAD_DIM)
        qh = _head_norm(aq_ref[:, lanes].astype(F32)) * qg
        q_ref[:, lanes] = (_rope(qh, cos, sin, first_half) * (QK_SCALE * LOG2_E)).astype(BF16)
    for h in range(ATTN_KV_HEADS):
        lanes = slice(h * HEAD_DIM, (h + 1) * HEAD_DIM)
        kh = _head_norm(akv_ref[:, lanes].astype(F32)) * kg
        kb = _rope(kh, cos, sin, first_half).astype(BF16)
        kt_ref[h] = kb.astype(F32).T.astype(BF16)
        k2 = jnp.max(jnp.sum(jnp.square(kb.astype(F32)), axis=-1, keepdims=True), axis=0, keepdims=True)
        kn_ref[h] = jnp.broadcast_to(k2, kn_ref.shape[1:])
        vl = slice((ATTN_KV_HEADS + h) * HEAD_DIM, (ATTN_KV_HEADS + h + 1) * HEAD_DIM)
        va_ref[h, :, :HEAD_DIM] = akv_ref[:, vl]
        va_ref[h, :, HEAD_DIM:] = jnp.ones((akv_ref.shape[0], HEAD_DIM), BF16)


def _flash_kernel(q_ref, kt_ref, va_ref, kn_ref, o_ref, m_ref, acc_ref, p_ref, mp_ref):
    tq = q_ref.shape[0]
    nkc, _, tk = kt_ref.shape
    n_rep = tk // HEAD_DIM
    q2 = jnp.concatenate([q_ref[:, :HEAD_DIM], q_ref[:, HEAD_DIM:]], axis=0)

    def chunk(c):
        return kt_ref[c], va_ref[pl.ds(pl.multiple_of(c * tk, tk), tk), :]

    def lane_max(s):
        part = s[:, :HEAD_DIM]
        for t in range(1, n_rep):
            part = jnp.maximum(part, s[:, t * HEAD_DIM:(t + 1) * HEAD_DIM])
        return part

    qf = q2.astype(F32)
    q_norm = jnp.sqrt(jnp.sum(qf * qf, axis=-1, keepdims=True))
    k_norm = jnp.sqrt(jnp.max(jnp.max(kn_ref[...], axis=0), axis=0, keepdims=True))
    safe = jnp.max(q_norm * k_norm) <= FRAME_MARGIN
    acc_ref[...] = jnp.zeros_like(acc_ref)

    def fast_scores(c, slot):
        kt, _ = chunk(c)
        s = jnp.dot(q2, kt, preferred_element_type=F32)
        p_ref[slot] = jnp.exp2(s - jnp.tile(m_ref[...], (1, n_rep))).astype(BF16)
        mp_ref[slot] = lane_max(s)

    def fast_accumulate(c, slot):
        _, vc = chunk(c)
        m_prev = m_ref[...]
        m_new = jnp.maximum(m_prev, jnp.max(mp_ref[slot], axis=-1, keepdims=True))
        alpha = jnp.exp2(m_prev - m_new)
        acc_ref[...] = (acc_ref[...] + jnp.dot(p_ref[slot], vc, preferred_element_type=F32)) * jnp.tile(alpha, (1, 2))
        m_ref[...] = m_new

    def fast():
        m_ref[...] = jnp.zeros_like(m_ref)

        group = 4 if nkc % 4 == 0 else 2

        def body(i, carry):
            for t in range(group):
                fast_scores(group * i + t, t % 2)
                fast_accumulate(group * i + t, t % 2)
            return carry

        lax.fori_loop(0, nkc // group, body, 0)

    def exact():
        m_ref[...] = jnp.full_like(m_ref, -jnp.inf)

        def body(c, carry):
            kt, vc = chunk(c)
            s = jnp.dot(q2, kt, preferred_element_type=F32)
            m_prev = m_ref[...]
            m_new = jnp.maximum(m_prev, jnp.max(lane_max(s), axis=-1, keepdims=True))
            p = jnp.exp2(s - jnp.tile(m_new, (1, n_rep))).astype(BF16)
            acc_ref[...] = (jnp.tile(jnp.exp2(m_prev - m_new), (1, 2)) * acc_ref[...]
                            + jnp.dot(p, vc, preferred_element_type=F32))
            m_ref[...] = m_new
            return carry

        lax.fori_loop(0, nkc, body, 0)

    lax.cond(safe, fast, exact)
    o = acc_ref[:, :HEAD_DIM] / acc_ref[:, HEAD_DIM:]
    o_ref[:, :HEAD_DIM] = o[:tq].astype(o_ref.dtype)
    o_ref[:, HEAD_DIM:] = o[tq:].astype(o_ref.dtype)


def _axial_gqa(u3, q_gain, k_gain, rope, B, S, tq=512, tk=1024):
    T = B * S
    cos, sin = rope
    u4 = u3.reshape(N_COLBLK, B, S, GROUP_W)
    tk = min(tk, S // 2)
    nkc = S // tk
    assert nkc >= 2 and nkc % 2 == 0, "flash kernel pipelines kv chunks in pairs"
    q_r, kt, va, kn = pl.pallas_call(
        _attn_prep_kernel,
        grid=(B, nkc),
        in_specs=[
            pl.BlockSpec((None, None, tk, GROUP_W), lambda b, i: (BLK_AQ, b, i, 0)),
            pl.BlockSpec((None, None, tk, GROUP_W), lambda b, i: (BLK_AKV, b, i, 0)),
            pl.BlockSpec((tk, HEAD_DIM), lambda b, i: (i, 0)),
            pl.BlockSpec((tk, HEAD_DIM), lambda b, i: (i, 0)),
            pl.BlockSpec((1, HEAD_DIM), lambda b, i: (0, 0)),
            pl.BlockSpec((1, HEAD_DIM), lambda b, i: (0, 0)),
        ],
        out_specs=[
            pl.BlockSpec((None, tk, GROUP_W), lambda b, i: (b, i, 0)),
            pl.BlockSpec((None, ATTN_KV_HEADS, None, HEAD_DIM, tk), lambda b, i: (b, 0, i, 0, 0)),
            pl.BlockSpec((None, ATTN_KV_HEADS, tk, 2 * HEAD_DIM), lambda b, i: (b, 0, i, 0)),
            pl.BlockSpec((None, ATTN_KV_HEADS, None, 8, HEAD_DIM), lambda b, i: (b, 0, i, 0, 0)),
        ],
        out_shape=[
            jax.ShapeDtypeStruct((B, S, GROUP_W), BF16),
            jax.ShapeDtypeStruct((B, ATTN_KV_HEADS, nkc, HEAD_DIM, tk), BF16),
            jax.ShapeDtypeStruct((B, ATTN_KV_HEADS, S, 2 * HEAD_DIM), BF16),
            jax.ShapeDtypeStruct((B, ATTN_KV_HEADS, nkc, 8, HEAD_DIM), F32),
        ],
        compiler_params=_params(("parallel", "parallel")),
        name="attn_prep",
    )(u4, u4, cos, sin, q_gain, k_gain)
    out = pl.pallas_call(
        _flash_kernel,
        grid=(B, ATTN_KV_HEADS, S // tq),
        in_specs=[
            pl.BlockSpec((None, tq, 2 * HEAD_DIM), lambda b, k, i: (b, i, k)),
            pl.BlockSpec((None, None, nkc, HEAD_DIM, tk), lambda b, k, i: (b, k, 0, 0, 0)),
            pl.BlockSpec((None, None, S, 2 * HEAD_DIM), lambda b, k, i: (b, k, 0, 0)),
            pl.BlockSpec((None, None, nkc, 8, HEAD_DIM), lambda b, k, i: (b, k, 0, 0, 0)),
        ],
        out_specs=pl.BlockSpec((None, tq, 2 * HEAD_DIM), lambda b, k, i: (b, i, k)),
        out_shape=jax.ShapeDtypeStruct((B, S, GROUP_W), BF16),
        scratch_shapes=[
            pltpu.VMEM((2 * tq, HEAD_DIM), F32),
            pltpu.VMEM((2 * tq, 2 * HEAD_DIM), F32),
            pltpu.VMEM((2, 2 * tq, tk), BF16),
            pltpu.VMEM((2, 2 * tq, HEAD_DIM), F32),
        ],
        compiler_params=_params(("parallel", "parallel", "arbitrary")),
        name="flash_gqa",
    )(q_r, kt, va, kn)
    return out.reshape(T, GROUP_W)


def _combine_kernel(x_ref, ya_ref, cb_ref, cc_ref, cx_ref, ccp_ref, cxp_ref, ccn_ref, cxn_ref, mo_ref,
                    hf_ref, hb_ref, yd_ref, cw_ref, cbias_ref, gain_ref, w_ref, o_ref, y_ref, *, tiles_per_seq):
    tm = x_ref.shape[0]
    i = pl.program_id(0)
    halo = ccp_ref.shape[0]
    at_start = (i % tiles_per_seq) == 0
    at_end = (i % tiles_per_seq) == tiles_per_seq - 1
    gain = gain_ref[...]

    z = cc_ref[...].astype(F32) * cx_ref[...].astype(F32)
    z_before = ccp_ref[halo - 1:halo, :].astype(F32) * cxp_ref[halo - 1:halo, :].astype(F32)
    z_after = ccn_ref[0:1, :].astype(F32) * cxn_ref[0:1, :].astype(F32)
    z_before = jnp.where(at_start, 0.0, z_before)
    z_after = jnp.where(at_end, 0.0, z_after)
    rid = lax.broadcasted_iota(jnp.int32, z.shape, 0)
    z_m1 = jnp.where(rid == 0, z_before, pltpu.roll(z, 1, axis=0))
    z_p1 = jnp.where(rid == tm - 1, z_after, pltpu.roll(z, tm - 1, axis=0))
    conv = z_m1 * cw_ref[0:1, :] + z * cw_ref[1:2, :] + z_p1 * cw_ref[2:3, :] + cbias_ref[...]
    yb = cb_ref[...].astype(F32) * conv

    for grp in range(4):
        lanes = slice(grp * HEAD_DIM, (grp + 1) * HEAD_DIM)
        y_ref[:, lanes] = (ya_ref[:, lanes].astype(F32) * gain[:, lanes]).astype(BF16)
        gl = slice(GROUP_W + grp * HEAD_DIM, GROUP_W + (grp + 1) * HEAD_DIM)
        y_ref[:, gl] = (_head_norm(yb[:, lanes]) * gain[:, gl]).astype(BF16)
        gl = slice(2 * GROUP_W + grp * HEAD_DIM, 2 * GROUP_W + (grp + 1) * HEAD_DIM)
        yc = _head_norm(hf_ref[:, lanes].astype(F32) + hb_ref[:, lanes].astype(F32))
        y_ref[:, gl] = (jax.nn.sigmoid(mo_ref[:, lanes].astype(F32)) * yc * gain[:, gl]).astype(BF16)
        gl = slice(3 * GROUP_W + grp * HEAD_DIM, 3 * GROUP_W + (grp + 1) * HEAD_DIM)
        y_ref[:, gl] = (_head_norm(yd_ref[:, lanes].astype(F32)) * gain[:, gl]).astype(BF16)

    o_ref[...] = x_ref[...] + jnp.dot(y_ref[...], w_ref[...], preferred_element_type=F32)


def _combine_out(x2, ya, u3, hf, hb, yd, conv_w, conv_b, out_gain, w_out, S, tm=512, halo=16):
    T = x2.shape[0]
    hb_per_tile = tm // halo
    n_halo = T // halo

    def u_blk(blk):
        return pl.BlockSpec((None, tm, GROUP_W), lambda i: (blk, i, 0))

    def u_prev(blk):
        return pl.BlockSpec((None, halo, GROUP_W), lambda i: (blk, jnp.maximum(i * hb_per_tile - 1, 0), 0))

    def u_next(blk):
        return pl.BlockSpec((None, halo, GROUP_W), lambda i: (blk, jnp.minimum((i + 1) * hb_per_tile, n_halo - 1), 0))

    tok = pl.BlockSpec((tm, GROUP_W), lambda i: (i, 0))
    return pl.pallas_call(
        functools.partial(_combine_kernel, tiles_per_seq=S // tm),
        grid=(T // tm,),
        in_specs=[
            pl.BlockSpec((tm, D_MODEL), lambda i: (i, 0)),
            tok,
            u_blk(BLK_CB), u_blk(BLK_CC), u_blk(BLK_CX),
            u_prev(BLK_CC), u_prev(BLK_CX), u_next(BLK_CC), u_next(BLK_CX),
            u_blk(BLK_MO),
            tok, tok, tok,
            pl.BlockSpec((8, GROUP_W), lambda i: (0, 0)),
            pl.BlockSpec((1, GROUP_W), lambda i: (0, 0)),
            pl.BlockSpec((1, D_MODEL), lambda i: (0, 0)),
            pl.BlockSpec((D_MODEL, D_MODEL), lambda i: (0, 0)),
        ],
        out_specs=pl.BlockSpec((tm, D_MODEL), lambda i: (i, 0)),
        out_shape=jax.ShapeDtypeStruct((T, D_MODEL), F32),
        scratch_shapes=[pltpu.VMEM((tm, D_MODEL), BF16)],
        compiler_params=_params(("parallel",)),
        name="combine_out",
    )(x2, ya, u3, u3, u3, u3, u3, u3, u3, u3, hf, hb, yd, conv_w, conv_b, out_gain, w_out)


def _mem_kv_kernel(mem_ref, g_ref, w_ref, kg_ref, kt_ref, v_ref, h_ref):
    _norm_rows_to(mem_ref, g_ref, h_ref)
    kv = jnp.dot(h_ref[...], w_ref[...], preferred_element_type=F32)
    kg = kg_ref[...]
    for h in range(CA_HEADS):
        lanes = slice(h * HEAD_DIM, (h + 1) * HEAD_DIM)
        kt_ref[h] = (_head_norm(kv[:, lanes]) * kg).T.astype(BF16)
    v_ref[...] = kv[:, CA_HEADS * HEAD_DIM:].astype(BF16)


def _mem_kv(mem, g_mem, w_kv, k_gain):
    B = mem.shape[0]
    return pl.pallas_call(
        _mem_kv_kernel,
        grid=(B,),
        in_specs=[
            pl.BlockSpec((None, N_MEM, D_MODEL), lambda b: (b, 0, 0)),
            pl.BlockSpec((1, D_MODEL), lambda b: (0, 0)),
            pl.BlockSpec((D_MODEL, 2 * GROUP_W), lambda b: (0, 0)),
            pl.BlockSpec((1, HEAD_DIM), lambda b: (0, 0)),
        ],
        out_specs=[
            pl.BlockSpec((None, CA_HEADS, HEAD_DIM, N_MEM), lambda b: (b, 0, 0, 0)),
            pl.BlockSpec((None, N_MEM, GROUP_W), lambda b: (b, 0, 0)),
        ],
        out_shape=[
            jax.ShapeDtypeStruct((B, CA_HEADS, HEAD_DIM, N_MEM), BF16),
            jax.ShapeDtypeStruct((B, N_MEM, GROUP_W), BF16),
        ],
        scratch_shapes=[pltpu.VMEM((N_MEM, D_MODEL), BF16)],
        compiler_params=_params(("parallel",)),
        name="mem_kv",
    )(mem, g_mem, w_kv, k_gain)


def _cross_attn_kernel(x_ref, g_ref, wq_ref, qg_ref, kt_ref, v_ref, wo_ref, o_ref, h_ref, oc_ref):
    _norm_rows_to(x_ref, g_ref, h_ref)
    q = jnp.dot(h_ref[...], wq_ref[...], preferred_element_type=F32)
    qg = qg_ref[...]
    for h in range(CA_HEADS):
        lanes = slice(h * HEAD_DIM, (h + 1) * HEAD_DIM)
        qh = (_head_norm(q[:, lanes]) * (qg * QK_SCALE)).astype(BF16)
        s = jnp.dot(qh, kt_ref[h], preferred_element_type=F32)
        e = jnp.exp(s - jnp.max(s, axis=-1, keepdims=True))
        o = jnp.dot(e.astype(BF16), v_ref[:, lanes], preferred_element_type=F32)
        oc_ref[:, lanes] = (o * pl.reciprocal(jnp.sum(e, axis=-1, keepdims=True), approx=False)).astype(BF16)
    o_ref[...] = x_ref[...] + jnp.dot(oc_ref[...], wo_ref[...], preferred_element_type=F32)


def _cross_attn(x2, g_ca, w_q, q_gain, kt, v, w_o, B, S, tm=1024):
    T = x2.shape[0]
    x3 = x2.reshape(B, S, D_MODEL)
    out = pl.pallas_call(
        _cross_attn_kernel,
        grid=(B, S // tm),
        in_specs=[
            pl.BlockSpec((None, tm, D_MODEL), lambda b, i: (b, i, 0)),
            pl.BlockSpec((1, D_MODEL), lambda b, i: (0, 0)),
            pl.BlockSpec((D_MODEL, GROUP_W), lambda b, i: (0, 0)),
            pl.BlockSpec((1, HEAD_DIM), lambda b, i: (0, 0)),
            pl.BlockSpec((None, CA_HEADS, HEAD_DIM, N_MEM), lambda b, i: (b, 0, 0, 0)),
            pl.BlockSpec((None, N_MEM, GROUP_W), lambda b, i: (b, 0, 0)),
            pl.BlockSpec((GROUP_W, D_MODEL), lambda b, i: (0, 0)),
        ],
        out_specs=pl.BlockSpec((None, tm, D_MODEL), lambda b, i: (b, i, 0)),
        out_shape=jax.ShapeDtypeStruct((B, S, D_MODEL), F32),
        scratch_shapes=[pltpu.VMEM((tm, D_MODEL), BF16), pltpu.VMEM((tm, GROUP_W), BF16)],
        compiler_params=_params(("parallel", "parallel")),
        name="cross_attn",
    )(x3, g_ca, w_q, q_gain, kt, v, w_o)
    return out.reshape(T, D_MODEL)


def _mlp_kernel(x_ref, g_ref, wu_ref, wd_ref, o_ref, h_ref):
    @pl.when(pl.program_id(1) == 0)
    def _():
        _norm_rows_to(x_ref, g_ref, h_ref)
        o_ref[...] = x_ref[...]

    up = jnp.dot(h_ref[...], wu_ref[...], preferred_element_type=F32)
    act = jnp.square(jnp.maximum(up, 0.0)).astype(BF16)
    o_ref[...] += jnp.dot(act, wd_ref[...], preferred_element_type=F32)


def _mlp(x2, g, w_up, w_down, tm=512, tf=1024):
    T = x2.shape[0]
    return pl.pallas_call(
        _mlp_kernel,
        grid=(T // tm, D_FF // tf),
        in_specs=[
            pl.BlockSpec((tm, D_MODEL), lambda i, j: (i, 0)),
            pl.BlockSpec((1, D_MODEL), lambda i, j: (0, 0)),
            pl.BlockSpec((D_MODEL, tf), lambda i, j: (0, j)),
            pl.BlockSpec((tf, D_MODEL), lambda i, j: (j, 0)),
        ],
        out_specs=pl.BlockSpec((tm, D_MODEL), lambda i, j: (i, 0)),
        out_shape=jax.ShapeDtypeStruct((T, D_MODEL), F32),
        scratch_shapes=[pltpu.VMEM((tm, D_MODEL), BF16)],
        compiler_params=_params(("parallel", "arbitrary")),
        name="mlp",
    )(x2, g, w_up, w_down)


def _prep_weights(w_in, conv_w, i_bias, f_bias, w_out, w_ca_q, w_ca_kv, w_ca_o, w_up, w_down):
    n_main = 8 * GROUP_W
    w_main = jnp.concatenate([w_in[:, :, :n_main], w_in[:, :, n_main + 16:]], axis=2).astype(BF16)
    pad = jnp.zeros((DEPTH, D_MODEL, HEAD_DIM - 8), F32)
    w_gate = jnp.concatenate([w_in[:, :, n_main:n_main + 8], pad, w_in[:, :, n_main + 8:n_main + 16], pad],
                             axis=2).astype(BF16)
    lane_pad = jnp.zeros((DEPTH, 1, HEAD_DIM - 8), F32)
    ib = jnp.concatenate([i_bias.reshape(DEPTH, 1, 8), lane_pad], axis=2)
    fb = jnp.concatenate([f_bias.reshape(DEPTH, 1, 8), lane_pad], axis=2)
    cw = jnp.concatenate([conv_w, jnp.zeros((DEPTH, 5, GROUP_W), F32)], axis=1)
    return dict(w_main=w_main, w_gate=w_gate, ib=ib, fb=fb, cw=cw, w_out=w_out.astype(BF16),
                w_ca_q=w_ca_q.astype(BF16), w_ca_kv=w_ca_kv.astype(BF16), w_ca_o=w_ca_o.astype(BF16),
                w_up=w_up.astype(BF16), w_down=w_down.astype(BF16))


def _trunk(x, mem, pw, g_mix, conv_b, attn_q_norm, attn_k_norm, out_gain, g_ca, g_mem, ca_q_norm, ca_k_norm, g_mlp):
    B, S, _ = x.shape
    T = B * S
    x2 = x.reshape(T, D_MODEL)
    rope = _rope_tables(S)
    ftab = _fourier_tables(S)
    for l in range(DEPTH):
        u3, gates = _norm_proj(x2, g_mix[l][None], pw["w_main"][l], pw["w_gate"][l])
        ya = _fourier(u3, B, S, ftab)
        hf, hb = _mlstm(u3, gates, pw["ib"][l], pw["fb"][l], B, S)
        yd = _axial_gqa(u3, attn_q_norm[l][None], attn_k_norm[l][None], rope, B, S)
        x2 = _combine_out(x2, ya, u3, hf, hb, yd, pw["cw"][l], conv_b[l][None], out_gain[l][None], pw["w_out"][l], S)
        kt, v = _mem_kv(mem, g_mem[l][None], pw["w_ca_kv"][l], ca_k_norm[l][None])
        x2 = _cross_attn(x2, g_ca[l][None], pw["w_ca_q"][l], ca_q_norm[l][None], kt, v, pw["w_ca_o"][l], B, S)
        x2 = _mlp(x2, g_mlp[l][None], pw["w_up"][l], pw["w_down"][l])
    return x2.reshape(B, S, D_MODEL)


def kernel(x_prompt, x_sample, mem_prompt, mem_sample, g_mix, w_in, conv_w, conv_b, i_bias, f_bias, attn_q_norm,
           attn_k_norm, out_gain, w_out, g_ca, g_mem, w_ca_q, w_ca_kv, ca_q_norm, ca_k_norm, w_ca_o, g_mlp, w_up,
           w_down):
    pw = _prep_weights(w_in, conv_w, i_bias, f_bias, w_out, w_ca_q, w_ca_kv, w_ca_o, w_up, w_down)
    args = (pw, g_mix, conv_b, attn_q_norm, attn_k_norm, out_gain, g_ca, g_mem, ca_q_norm, ca_k_norm, g_mlp)
    y_prompt = _trunk(x_prompt, mem_prompt, *args)
    y_sample = _trunk(x_sample, mem_sample, *args)
    return (y_prompt, y_sample)
```

```python
import functools
import math

import jax
import jax.numpy as jnp
from jax import lax
from jax.experimental import pallas as pl
from jax.experimental.pallas import tpu as pltpu

F32 = jnp.float32
BF16 = jnp.bfloat16

D_MODEL = 2048
DEPTH = 4
HEAD_DIM = 128
GROUP_W = 4 * HEAD_DIM
N_COLBLK = 10
GATE_W = 2 * HEAD_DIM
MLSTM_HEADS = 4
MLSTM_CHUNK = 128
ATTN_KV_HEADS = 2
GRID_W = 64
ROPE_THETA = 10000.0
ROPE_FREQS = HEAD_DIM // 4
CA_HEADS = 4
N_MEM = 256
D_FF = 4 * D_MODEL
RMS_EPS = 1e-6
QK_SCALE = HEAD_DIM ** -0.5
LOG2_E = math.log2(math.e)
FRAME_MARGIN = 64.0

BLK_FOURIER, BLK_CB, BLK_CC, BLK_CX, BLK_MQ, BLK_MK, BLK_MV, BLK_MO, BLK_AQ, BLK_AKV = range(N_COLBLK)

V7X_VMEM_LIMIT = 56 * 1024 * 1024
NORM_ROWS = 256


def _params(semantics, vmem=V7X_VMEM_LIMIT):
    return pltpu.CompilerParams(dimension_semantics=semantics, vmem_limit_bytes=vmem)


def _head_norm(y):
    return y * lax.rsqrt(jnp.mean(y * y, axis=-1, keepdims=True) + RMS_EPS)


def _norm_rows_to(x_ref, g_ref, h_ref):
    tm = x_ref.shape[0]
    g = g_ref[...]

    def body(r, carry):
        rows = pl.ds(pl.multiple_of(r * NORM_ROWS, NORM_ROWS), NORM_ROWS)
        xf = x_ref[rows, :]
        ms = jnp.mean(xf * xf, axis=-1, keepdims=True)
        h_ref[rows, :] = (xf * lax.rsqrt(ms + RMS_EPS) * g).astype(BF16)
        return carry

    lax.fori_loop(0, tm // NORM_ROWS, body, 0)


def _norm_proj_kernel(x_ref, g_ref, w_ref, wg_ref, u_ref, gate_ref, h_ref):
    @pl.when(pl.program_id(1) == 0)
    def _():
        _norm_rows_to(x_ref, g_ref, h_ref)
        gate_ref[...] = jnp.dot(h_ref[...], wg_ref[...], preferred_element_type=F32)

    res = jnp.dot(h_ref[...], w_ref[...], preferred_element_type=F32)
    for k in range(u_ref.shape[0]):
        u_ref[k] = res[:, k * GROUP_W:(k + 1) * GROUP_W].astype(BF16)


def _norm_proj(x2, g, w_main, w_gate, tm=1024, nb=2):
    T = x2.shape[0]
    tn = nb * GROUP_W
    return pl.pallas_call(
        _norm_proj_kernel,
        grid=(T // tm, N_COLBLK // nb),
        in_specs=[
            pl.BlockSpec((tm, D_MODEL), lambda i, j: (i, 0)),
            pl.BlockSpec((1, D_MODEL), lambda i, j: (0, 0)),
            pl.BlockSpec((D_MODEL, tn), lambda i, j: (0, j)),
            pl.BlockSpec((D_MODEL, GATE_W), lambda i, j: (0, 0)),
        ],
        out_specs=[
            pl.BlockSpec((nb, tm, GROUP_W), lambda i, j: (j, i, 0)),
            pl.BlockSpec((tm, GATE_W), lambda i, j: (i, 0)),
        ],
        out_shape=[
            jax.ShapeDtypeStruct((N_COLBLK, T, GROUP_W), BF16),
            jax.ShapeDtypeStruct((T, GATE_W), F32),
        ],
        scratch_shapes=[pltpu.VMEM((tm, D_MODEL), BF16)],
        compiler_params=_params(("parallel", "arbitrary")),
        name="norm_proj",
    )(x2, g, w_main, w_gate)


def _fourier_tables(S):
    n2 = HEAD_DIM
    n1 = S // n2
    i1 = jnp.arange(n1, dtype=jnp.int32)
    a1 = (2.0 * math.pi / n1) * ((i1[:, None] * i1[None, :]) % n1).astype(F32)
    f1 = jnp.concatenate([jnp.cos(a1), -jnp.sin(a1)], axis=0).astype(BF16)
    i2 = jnp.arange(n2, dtype=jnp.int32)
    p = i1[:, None, None] + n1 * i2[None, :, None]
    k = (p * i2[None, None, :]) % S
    a2 = (2.0 * math.pi / S) * k.astype(F32)
    c2, s2 = jnp.cos(a2), jnp.sin(a2)
    g = jnp.concatenate([jnp.concatenate([c2, s2], axis=2),
                         jnp.concatenate([-s2, c2], axis=2)], axis=1).astype(BF16)
    ac = (2.0 * math.pi / HEAD_DIM) * ((i2[:, None] * i2[None, :]) % HEAD_DIM).astype(F32)
    cs = jnp.concatenate([jnp.cos(ac), jnp.sin(ac)], axis=0).astype(BF16)
    return f1, g, cs


def _fourier_stage1_kernel(a_ref, f1_ref, y_ref):
    n1 = a_ref.shape[0]
    res = jnp.dot(f1_ref[...], a_ref[...], preferred_element_type=F32)
    y_ref[0] = res[:n1].astype(BF16)
    y_ref[1] = res[n1:].astype(BF16)


def _fourier_stage2_kernel(y_ref, g_ref, cs_ref, o_ref, *, scale):
    n_p = y_ref.shape[1]
    cs = cs_ref[...]
    for p in range(n_p):
        ycat = jnp.concatenate([y_ref[0, p], y_ref[1, p]], axis=0)
        x = jnp.dot(g_ref[p], ycat, preferred_element_type=F32).astype(BF16)
        for grp in range(4):
            lanes = slice(grp * HEAD_DIM, (grp + 1) * HEAD_DIM)
            xg = jnp.concatenate([x[:HEAD_DIM, lanes], x[HEAD_DIM:, lanes]], axis=1)
            o = jnp.dot(xg, cs, preferred_element_type=F32) * scale
            col = p * GROUP_W + grp * HEAD_DIM
            o_ref[:, col:col + HEAD_DIM] = _head_norm(o).astype(BF16)


def _fourier(u3, B, S, tables):
    f1, g, cs = tables
    n2 = HEAD_DIM
    n1 = S // n2
    T = B * S
    ncol = min(16, n2)
    a_view = u3[BLK_FOURIER].reshape(B, n1, n2 * GROUP_W)
    y = pl.pallas_call(
        _fourier_stage1_kernel,
        grid=(B, n2 // ncol),
        in_specs=[
            pl.BlockSpec((None, n1, ncol * GROUP_W), lambda b, j: (b, 0, j)),
            pl.BlockSpec((2 * n1, n1), lambda b, j: (0, 0)),
        ],
        out_specs=pl.BlockSpec((None, 2, n1, ncol * GROUP_W), lambda b, j: (b, 0, 0, j)),
        out_shape=jax.ShapeDtypeStruct((B, 2, n1, n2 * GROUP_W), BF16),
        compiler_params=_params(("parallel", "parallel")),
        name="fourier_stage1",
    )(a_view, f1)
    y5 = y.reshape(B, 2, n1, n2, GROUP_W)
    n_p = 8
    out = pl.pallas_call(
        functools.partial(_fourier_stage2_kernel, scale=1.0 / math.sqrt(S * HEAD_DIM)),
        grid=(B, n1 // n_p),
        in_specs=[
            pl.BlockSpec((None, 2, n_p, n2, GROUP_W), lambda b, j: (b, 0, j, 0, 0)),
            pl.BlockSpec((n_p, 2 * n2, 2 * n2), lambda b, j: (j, 0, 0)),
            pl.BlockSpec((2 * HEAD_DIM, HEAD_DIM), lambda b, j: (0, 0)),
        ],
        out_specs=pl.BlockSpec((None, n2, n_p * GROUP_W), lambda b, j: (b, 0, j)),
        out_shape=jax.ShapeDtypeStruct((B, n2, n1 * GROUP_W), BF16),
        compiler_params=_params(("parallel", "parallel")),
        name="fourier_stage2",
    )(y5, g, cs)
    return out.reshape(T, GROUP_W)


def _log_sigmoid(x):
    return jnp.minimum(x, 0.0) - jnp.log1p(jnp.exp(-jnp.abs(x)))


def _mlstm_chunk(d, q, k, v, gi, gf, ib, fb, tri, mask, c_ref, n_ref, m_ref, out_ref, rows):
    li = gi + ib
    lf = _log_sigmoid(gf + fb)
    b = jnp.dot(tri, lf, precision=lax.Precision.HIGHEST, preferred_element_type=F32)
    g = jnp.sum(lf, axis=0, keepdims=True)
    w_end = g - b + li
    m_loc = jnp.max(w_end, axis=0, keepdims=True)
    e = jnp.exp(w_end - m_loc)
    a_t = (li - b).T
    m_old = m_ref[d:d + 1, :]
    m_new = jnp.maximum(g + m_old, m_loc)
    a_sc = jnp.exp(g + m_old - m_new)
    b_sc = jnp.exp(m_loc - m_new)
    for h in range(MLSTM_HEADS):
        hp = MLSTM_HEADS * d + h
        lanes = slice(h * HEAD_DIM, (h + 1) * HEAD_DIM)
        qh = q[:, lanes]
        vh = v[:, lanes]
        ks = k[:, lanes].astype(F32) * QK_SCALE
        bc = b[:, hp:hp + 1]
        dm = jnp.where(mask, bc + a_t[hp:hp + 1, :], -jnp.inf)
        inter_log = bc + m_old[:, hp:hp + 1]
        m_t = jnp.maximum(inter_log, jnp.max(dm, axis=-1, keepdims=True))
        p = jnp.exp(dm - m_t)
        inter_w = jnp.exp(inter_log - m_t)
        qk = lax.dot_general(qh, ks.astype(BF16), (((1,), (1,)), ((), ())), preferred_element_type=F32)
        a_mat = p * qk
        c_prev = c_ref[hp]
        n_prev = n_ref[hp:hp + 1, :]
        num = (jnp.dot(a_mat.astype(BF16), vh, preferred_element_type=F32)
               + inter_w * jnp.dot(qh, c_prev.astype(BF16), preferred_element_type=F32))
        den = (jnp.sum(a_mat, axis=-1, keepdims=True)
               + inter_w * jnp.sum(qh.astype(F32) * n_prev, axis=-1, keepdims=True))
        hout = num / jnp.maximum(jnp.abs(den), jnp.exp(-m_t))
        out_ref[rows, lanes] = hout.astype(out_ref.dtype)
        ke = ks * e[:, hp:hp + 1]
        c_loc = jnp.dot(ke.T.astype(BF16), vh, preferred_element_type=F32)
        n_loc = jnp.sum(ke, axis=0, keepdims=True)
        ah = a_sc[:, hp:hp + 1]
        bh = b_sc[:, hp:hp + 1]
        c_ref[hp] = ah * c_prev + bh * c_loc
        n_ref[hp:hp + 1, :] = ah * n_prev + bh * n_loc
    m_ref[d:d + 1, :] = m_new


def _mlstm_kernel(qf_ref, kf_ref, vf_ref, gf_ref, qb_ref, kb_ref, vb_ref, gb_ref, ib_ref, fb_ref,
                  hf_ref, hb_ref, c_ref, n_ref, m_ref):
    L = MLSTM_CHUNK
    n_sub = qf_ref.shape[0] // L

    @pl.when(pl.program_id(1) == 0)
    def _():
        c_ref[...] = jnp.zeros_like(c_ref)
        n_ref[...] = jnp.zeros_like(n_ref)
        m_ref[...] = jnp.zeros_like(m_ref)

    row = lax.broadcasted_iota(jnp.int32, (L, L), 0)
    col = lax.broadcasted_iota(jnp.int32, (L, L), 1)
    mask_f = col <= row
    mask_b = col >= row
    tri_f = jnp.where(mask_f, 1.0, 0.0).astype(F32)
    tri_b = jnp.where(mask_b, 1.0, 0.0).astype(F32)
    ib = ib_ref[...]
    fb = fb_ref[...]

    def body(j, carry):
        rf = pl.ds(pl.multiple_of(j * L, L), L)
        rb = pl.ds(pl.multiple_of((n_sub - 1 - j) * L, L), L)
        _mlstm_chunk(0, qf_ref[rf, :], kf_ref[rf, :], vf_ref[rf, :], gf_ref[rf, :HEAD_DIM], gf_ref[rf, HEAD_DIM:],
                     ib, fb, tri_f, mask_f, c_ref, n_ref, m_ref, hf_ref, rf)
        _mlstm_chunk(1, qb_ref[rb, :], kb_ref[rb, :], vb_ref[rb, :], gb_ref[rb, :HEAD_DIM], gb_ref[rb, HEAD_DIM:],
                     ib, fb, tri_b, mask_b, c_ref, n_ref, m_ref, hb_ref, rb)
        return carry

    lax.fori_loop(0, n_sub, body, 0)


def _mlstm(u3, gates, ib, fb, B, S, n_sub=4):
    T = B * S
    rows = n_sub * MLSTM_CHUNK
    nblk = S // rows
    u4 = u3.reshape(N_COLBLK, B, S, GROUP_W)
    g3 = gates.reshape(B, S, GATE_W)

    def fwd(blk):
        return pl.BlockSpec((None, None, rows, GROUP_W), lambda b, c: (blk, b, c, 0))

    def bwd(blk):
        return pl.BlockSpec((None, None, rows, GROUP_W), lambda b, c: (blk, b, nblk - 1 - c, 0))

    hf, hb = pl.pallas_call(
        _mlstm_kernel,
        grid=(B, nblk),
        in_specs=[
            fwd(BLK_MQ), fwd(BLK_MK), fwd(BLK_MV),
            pl.BlockSpec((None, rows, GATE_W), lambda b, c: (b, c, 0)),
            bwd(BLK_MQ), bwd(BLK_MK), bwd(BLK_MV),
            pl.BlockSpec((None, rows, GATE_W), lambda b, c: (b, nblk - 1 - c, 0)),
            pl.BlockSpec((1, HEAD_DIM), lambda b, c: (0, 0)),
            pl.BlockSpec((1, HEAD_DIM), lambda b, c: (0, 0)),
        ],
        out_specs=[
            pl.BlockSpec((None, rows, GROUP_W), lambda b, c: (b, c, 0)),
            pl.BlockSpec((None, rows, GROUP_W), lambda b, c: (b, nblk - 1 - c, 0)),
        ],
        out_shape=[jax.ShapeDtypeStruct((B, S, GROUP_W), BF16)] * 2,
        scratch_shapes=[
            pltpu.VMEM((2 * MLSTM_HEADS, HEAD_DIM, HEAD_DIM), F32),
            pltpu.VMEM((2 * MLSTM_HEADS, HEAD_DIM), F32),
            pltpu.VMEM((8, HEAD_DIM), F32),
        ],
        compiler_params=_params(("parallel", "arbitrary")),
        name="mlstm",
    )(u4, u4, u4, g3, u4, u4, u4, g3, ib, fb)
    return hf.reshape(T, GROUP_W), hb.reshape(T, GROUP_W)


def _rope_tables(S):
    rows = S // GRID_W
    row = jnp.repeat(jnp.arange(rows), GRID_W).astype(F32)
    colp = jnp.tile(jnp.arange(GRID_W), rows).astype(F32)
    inv = ROPE_THETA ** (-jnp.arange(ROPE_FREQS, dtype=F32) / ROPE_FREQS)
    ar = row[:, None] * inv
    ac = colp[:, None] * inv
    ang = jnp.concatenate([ar, ar, ac, ac], axis=1)
    first_half = (jnp.arange(HEAD_DIM) % (2 * ROPE_FREQS)) < ROPE_FREQS
    return jnp.cos(ang), jnp.where(first_half, -1.0, 1.0) * jnp.sin(ang)


def _rope(x, cos, sin_signed, first_half):
    partner = jnp.where(first_half, pltpu.roll(x, HEAD_DIM - ROPE_FREQS, axis=1), pltpu.roll(x, ROPE_FREQS, axis=1))
    return x * cos + partner * sin_signed


def _attn_prep_kernel(aq_ref, akv_ref, cos_ref, sin_ref, qg_ref, kg_ref, q_ref, kt_ref, va_ref, kn_ref):
    cos = cos_ref[...]
    sin = sin_ref[...]
    lane = lax.broadcasted_iota(jnp.int32, cos.shape, 1)
    first_half = (lane % (2 * ROPE_FREQS)) < ROPE_FREQS
    qg = qg_ref[...]
    kg = kg_ref[...]
    for h in range(4):
        lanes = slice(h * HEAD_DIM, (h + 1) * HEAD_DIM)
        qh = _head_norm(aq_ref[:, lanes].astype(F32)) * qg
        q_ref[:, lanes] = (_rope(qh, cos, sin, first_half) * (QK_SCALE * LOG2_E)).astype(BF16)
    for h in range(ATTN_KV_HEADS):
        lanes = slice(h * HEAD_DIM, (h + 1) * HEAD_DIM)
        kh = _head_norm(akv_ref[:, lanes].astype(F32)) * kg
        kb = _rope(kh, cos, sin, first_half).astype(BF16)
        kt_ref[h] = kb.astype(F32).T.astype(BF16)
        k2 = jnp.max(jnp.sum(jnp.square(kb.astype(F32)), axis=-1, keepdims=True), axis=0, keepdims=True)
        kn_ref[h] = jnp.broadcast_to(k2, kn_ref.shape[1:])
        vl = slice((ATTN_KV_HEADS + h) * HEAD_DIM, (ATTN_KV_HEADS + h + 1) * HEAD_DIM)
        va_ref[h, :, :HEAD_DIM] = akv_ref[:, vl]
        va_ref[h, :, HEAD_DIM:] = jnp.ones((akv_ref.shape[0], HEAD_DIM), BF16)


def _flash_kernel(q_ref, kt_ref, va_ref, kn_ref, o_ref, m_ref, acc_ref, p_ref, mp_ref):
    tq = q_ref.shape[0]
    nkc, _, tk = kt_ref.shape
    n_rep = tk // HEAD_DIM
    q2 = jnp.concatenate([q_ref[:, :HEAD_DIM], q_ref[:, HEAD_DIM:]], axis=0)

    def chunk(c):
        return kt_ref[c], va_ref[pl.ds(pl.multiple_of(c * tk, tk), tk), :]

    def lane_max(s):
        part = s[:, :HEAD_DIM]
        for t in range(1, n_rep):
            part = jnp.maximum(part, s[:, t * HEAD_DIM:(t + 1) * HEAD_DIM])
        return part

    qf = q2.astype(F32)
    q_norm = jnp.sqrt(jnp.sum(qf * qf, axis=-1, keepdims=True))
    k_norm = jnp.sqrt(jnp.max(jnp.max(kn_ref[...], axis=0), axis=0, keepdims=True))
    safe = jnp.max(q_norm * k_norm) <= FRAME_MARGIN
    acc_ref[...] = jnp.zeros_like(acc_ref)

    def fast_scores(c, slot):
        kt, _ = chunk(c)
        s = jnp.dot(q2, kt, preferred_element_type=F32)
        p_ref[slot] = jnp.exp2(s - jnp.tile(m_ref[...], (1, n_rep))).astype(BF16)
        mp_ref[slot] = lane_max(s)

    def fast_accumulate(c, slot):
        _, vc = chunk(c)
        m_prev = m_ref[...]
        m_new = jnp.maximum(m_prev, jnp.max(mp_ref[slot], axis=-1, keepdims=True))
        alpha = jnp.exp2(m_prev - m_new)
        acc_ref[...] = (acc_ref[...] + jnp.dot(p_ref[slot], vc, preferred_element_type=F32)) * jnp.tile(alpha, (1, 2))
        m_ref[...] = m_new

    def fast():
        m_ref[...] = jnp.zeros_like(m_ref)

        group = 8 if nkc % 8 == 0 else 2

        def body(i, carry):
            for t in range(group):
                fast_scores(group * i + t, t % 2)
                fast_accumulate(group * i + t, t % 2)
            return carry

        lax.fori_loop(0, nkc // group, body, 0)

    def exact():
        m_ref[...] = jnp.full_like(m_ref, -jnp.inf)

        def body(c, carry):
            kt, vc = chunk(c)
            s = jnp.dot(q2, kt, preferred_element_type=F32)
            m_prev = m_ref[...]
            m_new = jnp.maximum(m_prev, jnp.max(lane_max(s), axis=-1, keepdims=True))
            p = jnp.exp2(s - jnp.tile(m_new, (1, n_rep))).astype(BF16)
            acc_ref[...] = (jnp.tile(jnp.exp2(m_prev - m_new), (1, 2)) * acc_ref[...]
                            + jnp.dot(p, vc, preferred_element_type=F32))
            m_ref[...] = m_new
            return carry

        lax.fori_loop(0, nkc, body, 0)

    lax.cond(safe, fast, exact)
    o = acc_ref[:, :HEAD_DIM] / acc_ref[:, HEAD_DIM:]
    o_ref[:, :HEAD_DIM] = o[:tq].astype(o_ref.dtype)
    o_ref[:, HEAD_DIM:] = o[tq:].astype(o_ref.dtype)


def _axial_gqa(u3, q_gain, k_gain, rope, B, S, tq=512, tk=1024):
    T = B * S
    cos, sin = rope
    u4 = u3.reshape(N_COLBLK, B, S, GROUP_W)
    tk = min(tk, S // 2)
    nkc = S // tk
    assert nkc >= 2 and nkc % 2 == 0, "flash kernel pipelines kv chunks in pairs"
    q_r, kt, va, kn = pl.pallas_call(
        _attn_prep_kernel,
        grid=(B, nkc),
        in_specs=[
            pl.BlockSpec((None, None, tk, GROUP_W), lambda b, i: (BLK_AQ, b, i, 0)),
            pl.BlockSpec((None, None, tk, GROUP_W), lambda b, i: (BLK_AKV, b, i, 0)),
            pl.BlockSpec((tk, HEAD_DIM), lambda b, i: (i, 0)),
            pl.BlockSpec((tk, HEAD_DIM), lambda b, i: (i, 0)),
            pl.BlockSpec((1, HEAD_DIM), lambda b, i: (0, 0)),
            pl.BlockSpec((1, HEAD_DIM), lambda b, i: (0, 0)),
        ],
        out_specs=[
            pl.BlockSpec((None, tk, GROUP_W), lambda b, i: (b, i, 0)),
            pl.BlockSpec((None, ATTN_KV_HEADS, None, HEAD_DIM, tk), lambda b, i: (b, 0, i, 0, 0)),
            pl.BlockSpec((None, ATTN_KV_HEADS, tk, 2 * HEAD_DIM), lambda b, i: (b, 0, i, 0)),
            pl.BlockSpec((None, ATTN_KV_HEADS, None, 8, HEAD_DIM), lambda b, i: (b, 0, i, 0, 0)),
        ],
        out_shape=[
            jax.ShapeDtypeStruct((B, S, GROUP_W), BF16),
            jax.ShapeDtypeStruct((B, ATTN_KV_HEADS, nkc, HEAD_DIM, tk), BF16),
            jax.ShapeDtypeStruct((B, ATTN_KV_HEADS, S, 2 * HEAD_DIM), BF16),
            jax.ShapeDtypeStruct((B, ATTN_KV_HEADS, nkc, 8, HEAD_DIM), F32),
        ],
        compiler_params=_params(("parallel", "parallel")),
        name="attn_prep",
    )(u4, u4, cos, sin, q_gain, k_gain)
    out = pl.pallas_call(
        _flash_kernel,
        grid=(B, ATTN_KV_HEADS, S // tq),
        in_specs=[
            pl.BlockSpec((None, tq, 2 * HEAD_DIM), lambda b, k, i: (b, i, k)),
            pl.BlockSpec((None, None, nkc, HEAD_DIM, tk), lambda b, k, i: (b, k, 0, 0, 0)),
            pl.BlockSpec((None, None, S, 2 * HEAD_DIM), lambda b, k, i: (b, k, 0, 0)),
            pl.BlockSpec((None, None, nkc, 8, HEAD_DIM), lambda b, k, i: (b, k, 0, 0, 0)),
        ],
        out_specs=pl.BlockSpec((None, tq, 2 * HEAD_DIM), lambda b, k, i: (b, i, k)),
        out_shape=jax.ShapeDtypeStruct((B, S, GROUP_W), BF16),
        scratch_shapes=[
            pltpu.VMEM((2 * tq, HEAD_DIM), F32),
            pltpu.VMEM((2 * tq, 2 * HEAD_DIM), F32),
            pltpu.VMEM((2, 2 * tq, tk), BF16),
            pltpu.VMEM((2, 2 * tq, HEAD_DIM), F32),
        ],
        compiler_params=_params(("parallel", "parallel", "arbitrary")),
        name="flash_gqa",
    )(q_r, kt, va, kn)
    return out.reshape(T, GROUP_W)


def _combine_kernel(x_ref, ya_ref, cb_ref, cc_ref, cx_ref, ccp_ref, cxp_ref, ccn_ref, cxn_ref, mo_ref,
                    hf_ref, hb_ref, yd_ref, cw_ref, cbias_ref, gain_ref, w_ref, o_ref, y_ref, *, tiles_per_seq):
    tm = x_ref.shape[0]
    i = pl.program_id(0)
    halo = ccp_ref.shape[0]
    at_start = (i % tiles_per_seq) == 0
    at_end = (i % tiles_per_seq) == tiles_per_seq - 1
    gain = gain_ref[...]

    z = cc_ref[...].astype(F32) * cx_ref[...].astype(F32)
    z_before = ccp_ref[halo - 1:halo, :].astype(F32) * cxp_ref[halo - 1:halo, :].astype(F32)
    z_after = ccn_ref[0:1, :].astype(F32) * cxn_ref[0:1, :].astype(F32)
    z_before = jnp.where(at_start, 0.0, z_before)
    z_after = jnp.where(at_end, 0.0, z_after)
    rid = lax.broadcasted_iota(jnp.int32, z.shape, 0)
    z_m1 = jnp.where(rid == 0, z_before, pltpu.roll(z, 1, axis=0))
    z_p1 = jnp.where(rid == tm - 1, z_after, pltpu.roll(z, tm - 1, axis=0))
    conv = z_m1 * cw_ref[0:1, :] + z * cw_ref[1:2, :] + z_p1 * cw_ref[2:3, :] + cbias_ref[...]
    yb = cb_ref[...].astype(F32) * conv

    for grp in range(4):
        lanes = slice(grp * HEAD_DIM, (grp + 1) * HEAD_DIM)
        y_ref[:, lanes] = (ya_ref[:, lanes].astype(F32) * gain[:, lanes]).astype(BF16)
        gl = slice(GROUP_W + grp * HEAD_DIM, GROUP_W + (grp + 1) * HEAD_DIM)
        y_ref[:, gl] = (_head_norm(yb[:, lanes]) * gain[:, gl]).astype(BF16)
        gl = slice(2 * GROUP_W + grp * HEAD_DIM, 2 * GROUP_W + (grp + 1) * HEAD_DIM)
        yc = _head_norm(hf_ref[:, lanes].astype(F32) + hb_ref[:, lanes].astype(F32))
        y_ref[:, gl] = (jax.nn.sigmoid(mo_ref[:, lanes].astype(F32)) * yc * gain[:, gl]).astype(BF16)
        gl = slice(3 * GROUP_W + grp * HEAD_DIM, 3 * GROUP_W + (grp + 1) * HEAD_DIM)
        y_ref[:, gl] = (_head_norm(yd_ref[:, lanes].astype(F32)) * gain[:, gl]).astype(BF16)

    o_ref[...] = x_ref[...] + jnp.dot(y_ref[...], w_ref[...], preferred_element_type=F32)


def _combine_out(x2, ya, u3, hf, hb, yd, conv_w, conv_b, out_gain, w_out, S, tm=512, halo=16):
    T = x2.shape[0]
    hb_per_tile = tm // halo
    n_halo = T // halo

    def u_blk(blk):
        return pl.BlockSpec((None, tm, GROUP_W), lambda i: (blk, i, 0))

    def u_prev(blk):
        return pl.BlockSpec((None, halo, GROUP_W), lambda i: (blk, jnp.maximum(i * hb_per_tile - 1, 0), 0))

    def u_next(blk):
        return pl.BlockSpec((None, halo, GROUP_W), lambda i: (blk, jnp.minimum((i + 1) * hb_per_tile, n_halo - 1), 0))

    tok = pl.BlockSpec((tm, GROUP_W), lambda i: (i, 0))
    return pl.pallas_call(
        functools.partial(_combine_kernel, tiles_per_seq=S // tm),
        grid=(T // tm,),
        in_specs=[
            pl.BlockSpec((tm, D_MODEL), lambda i: (i, 0)),
            tok,
            u_blk(BLK_CB), u_blk(BLK_CC), u_blk(BLK_CX),
            u_prev(BLK_CC), u_prev(BLK_CX), u_next(BLK_CC), u_next(BLK_CX),
            u_blk(BLK_MO),
            tok, tok, tok,
            pl.BlockSpec((8, GROUP_W), lambda i: (0, 0)),
            pl.BlockSpec((1, GROUP_W), lambda i: (0, 0)),
            pl.BlockSpec((1, D_MODEL), lambda i: (0, 0)),
            pl.BlockSpec((D_MODEL, D_MODEL), lambda i: (0, 0)),
        ],
        out_specs=pl.BlockSpec((tm, D_MODEL), lambda i: (i, 0)),
        out_shape=jax.ShapeDtypeStruct((T, D_MODEL), F32),
        scratch_shapes=[pltpu.VMEM((tm, D_MODEL), BF16)],
        compiler_params=_params(("parallel",)),
        name="combine_out",
    )(x2, ya, u3, u3, u3, u3, u3, u3, u3, u3, hf, hb, yd, conv_w, conv_b, out_gain, w_out)


def _mem_kv_kernel(mem_ref, g_ref, w_ref, kg_ref, kt_ref, v_ref, h_ref):
    _norm_rows_to(mem_ref, g_ref, h_ref)
    kv = jnp.dot(h_ref[...], w_ref[...], preferred_element_type=F32)
    kg = kg_ref[...]
    for h in range(CA_HEADS):
        lanes = slice(h * HEAD_DIM, (h + 1) * HEAD_DIM)
        kt_ref[h] = (_head_norm(kv[:, lanes]) * kg).T.astype(BF16)
    v_ref[...] = kv[:, CA_HEADS * HEAD_DIM:].astype(BF16)


def _mem_kv(mem, g_mem, w_kv, k_gain):
    B = mem.shape[0]
    return pl.pallas_call(
        _mem_kv_kernel,
        grid=(B,),
        in_specs=[
            pl.BlockSpec((None, N_MEM, D_MODEL), lambda b: (b, 0, 0)),
            pl.BlockSpec((1, D_MODEL), lambda b: (0, 0)),
            pl.BlockSpec((D_MODEL, 2 * GROUP_W), lambda b: (0, 0)),
            pl.BlockSpec((1, HEAD_DIM), lambda b: (0, 0)),
        ],
        out_specs=[
            pl.BlockSpec((None, CA_HEADS, HEAD_DIM, N_MEM), lambda b: (b, 0, 0, 0)),
            pl.BlockSpec((None, N_MEM, GROUP_W), lambda b: (b, 0, 0)),
        ],
        out_shape=[
            jax.ShapeDtypeStruct((B, CA_HEADS, HEAD_DIM, N_MEM), BF16),
            jax.ShapeDtypeStruct((B, N_MEM, GROUP_W), BF16),
        ],
        scratch_shapes=[pltpu.VMEM((N_MEM, D_MODEL), BF16)],
        compiler_params=_params(("parallel",)),
        name="mem_kv",
    )(mem, g_mem, w_kv, k_gain)


def _cross_attn_kernel(x_ref, g_ref, wq_ref, qg_ref, kt_ref, v_ref, wo_ref, o_ref, h_ref, oc_ref):
    _norm_rows_to(x_ref, g_ref, h_ref)
    q = jnp.dot(h_ref[...], wq_ref[...], preferred_element_type=F32)
    qg = qg_ref[...]
    for h in range(CA_HEADS):
        lanes = slice(h * HEAD_DIM, (h + 1) * HEAD_DIM)
        qh = (_head_norm(q[:, lanes]) * (qg * QK_SCALE)).astype(BF16)
        s = jnp.dot(qh, kt_ref[h], preferred_element_type=F32)
        e = jnp.exp(s - jnp.max(s, axis=-1, keepdims=True))
        o = jnp.dot(e.astype(BF16), v_ref[:, lanes], preferred_element_type=F32)
        oc_ref[:, lanes] = (o * pl.reciprocal(jnp.sum(e, axis=-1, keepdims=True), approx=False)).astype(BF16)
    o_ref[...] = x_ref[...] + jnp.dot(oc_ref[...], wo_ref[...], preferred_element_type=F32)


def _cross_attn(x2, g_ca, w_q, q_gain, kt, v, w_o, B, S, tm=1024):
    T = x2.shape[0]
    x3 = x2.reshape(B, S, D_MODEL)
    out = pl.pallas_call(
        _cross_attn_kernel,
        grid=(B, S // tm),
        in_specs=[
            pl.BlockSpec((None, tm, D_MODEL), lambda b, i: (b, i, 0)),
            pl.BlockSpec((1, D_MODEL), lambda b, i: (0, 0)),
            pl.BlockSpec((D_MODEL, GROUP_W), lambda b, i: (0, 0)),
            pl.BlockSpec((1, HEAD_DIM), lambda b, i: (0, 0)),
            pl.BlockSpec((None, CA_HEADS, HEAD_DIM, N_MEM), lambda b, i: (b, 0, 0, 0)),
            pl.BlockSpec((None, N_MEM, GROUP_W), lambda b, i: (b, 0, 0)),
            pl.BlockSpec((GROUP_W, D_MODEL), lambda b, i: (0, 0)),
        ],
        out_specs=pl.BlockSpec((None, tm, D_MODEL), lambda b, i: (b, i, 0)),
        out_shape=jax.ShapeDtypeStruct((B, S, D_MODEL), F32),
        scratch_shapes=[pltpu.VMEM((tm, D_MODEL), BF16), pltpu.VMEM((tm, GROUP_W), BF16)],
        compiler_params=_params(("parallel", "parallel")),
        name="cross_attn",
    )(x3, g_ca, w_q, q_gain, kt, v, w_o)
    return out.reshape(T, D_MODEL)


def _mlp_kernel(x_ref, g_ref, wu_ref, wd_ref, o_ref, h_ref):
    @pl.when(pl.program_id(1) == 0)
    def _():
        _norm_rows_to(x_ref, g_ref, h_ref)
        o_ref[...] = x_ref[...]

    up = jnp.dot(h_ref[...], wu_ref[...], preferred_element_type=F32)
    act = jnp.square(jnp.maximum(up, 0.0)).astype(BF16)
    o_ref[...] += jnp.dot(act, wd_ref[...], preferred_element_type=F32)


def _mlp(x2, g, w_up, w_down, tm=512, tf=1024):
    T = x2.shape[0]
    return pl.pallas_call(
        _mlp_kernel,
        grid=(T // tm, D_FF // tf),
        in_specs=[
            pl.BlockSpec((tm, D_MODEL), lambda i, j: (i, 0)),
            pl.BlockSpec((1, D_MODEL), lambda i, j: (0, 0)),
            pl.BlockSpec((D_MODEL, tf), lambda i, j: (0, j)),
            pl.BlockSpec((tf, D_MODEL), lambda i, j: (j, 0)),
        ],
        out_specs=pl.BlockSpec((tm, D_MODEL), lambda i, j: (i, 0)),
        out_shape=jax.ShapeDtypeStruct((T, D_MODEL), F32),
        scratch_shapes=[pltpu.VMEM((tm, D_MODEL), BF16)],
        compiler_params=_params(("parallel", "arbitrary")),
        name="mlp",
    )(x2, g, w_up, w_down)


def _prep_weights(w_in, conv_w, i_bias, f_bias, w_out, w_ca_q, w_ca_kv, w_ca_o, w_up, w_down):
    n_main = 8 * GROUP_W
    w_main = jnp.concatenate([w_in[:, :, :n_main], w_in[:, :, n_main + 16:]], axis=2).astype(BF16)
    pad = jnp.zeros((DEPTH, D_MODEL, HEAD_DIM - 8), F32)
    w_gate = jnp.concatenate([w_in[:, :, n_main:n_main + 8], pad, w_in[:, :, n_main + 8:n_main + 16], pad],
                             axis=2).astype(BF16)
    lane_pad = jnp.zeros((DEPTH, 1, HEAD_DIM - 8), F32)
    ib = jnp.concatenate([i_bias.reshape(DEPTH, 1, 8), lane_pad], axis=2)
    fb = jnp.concatenate([f_bias.reshape(DEPTH, 1, 8), lane_pad], axis=2)
    cw = jnp.concatenate([conv_w, jnp.zeros((DEPTH, 5, GROUP_W), F32)], axis=1)
    return dict(w_main=w_main, w_gate=w_gate, ib=ib, fb=fb, cw=cw, w_out=w_out.astype(BF16),
                w_ca_q=w_ca_q.astype(BF16), w_ca_kv=w_ca_kv.astype(BF16), w_ca_o=w_ca_o.astype(BF16),
                w_up=w_up.astype(BF16), w_down=w_down.astype(BF16))


def _trunk(x, mem, pw, g_mix, conv_b, attn_q_norm, attn_k_norm, out_gain, g_ca, g_mem, ca_q_norm, ca_k_norm, g_mlp):
    B, S, _ = x.shape
    T = B * S
    x2 = x.reshape(T, D_MODEL)
    rope = _rope_tables(S)
    ftab = _fourier_tables(S)
    for l in range(DEPTH):
        u3, gates = _norm_proj(x2, g_mix[l][None], pw["w_main"][l], pw["w_gate"][l])
        ya = _fourier(u3, B, S, ftab)
        hf, hb = _mlstm(u3, gates, pw["ib"][l], pw["fb"][l], B, S)
        yd = _axial_gqa(u3, attn_q_norm[l][None], attn_k_norm[l][None], rope, B, S)
        x2 = _combine_out(x2, ya, u3, hf, hb, yd, pw["cw"][l], conv_b[l][None], out_gain[l][None], pw["w_out"][l], S)
        kt, v = _mem_kv(mem, g_mem[l][None], pw["w_ca_kv"][l], ca_k_norm[l][None])
        x2 = _cross_attn(x2, g_ca[l][None], pw["w_ca_q"][l], ca_q_norm[l][None], kt, v, pw["w_ca_o"][l], B, S)
        x2 = _mlp(x2, g_mlp[l][None], pw["w_up"][l], pw["w_down"][l])
    return x2.reshape(B, S, D_MODEL)


def kernel(x_prompt, x_sample, mem_prompt, mem_sample, g_mix, w_in, conv_w, conv_b, i_bias, f_bias, attn_q_norm,
           attn_k_norm, out_gain, w_out, g_ca, g_mem, w_ca_q, w_ca_kv, ca_q_norm, ca_k_norm, w_ca_o, g_mlp, w_up,
           w_down):
    pw = _prep_weights(w_in, conv_w, i_bias, f_bias, w_out, w_ca_q, w_ca_kv, w_ca_o, w_up, w_down)
    args = (pw, g_mix, conv_b, attn_q_norm, attn_k_norm, out_gain, g_ca, g_mem, ca_q_norm, ca_k_norm, g_mlp)
    y_prompt = _trunk(x_prompt, mem_prompt, *args)
    y_sample = _trunk(x_sample, mem_sample, *args)
    return (y_prompt, y_sample)
```

```python
import functools
import math

import jax
import jax.numpy as jnp
from jax import lax
from jax.experimental import pallas as pl
from jax.experimental.pallas import tpu as pltpu

F32 = jnp.float32
BF16 = jnp.bfloat16

D_MODEL = 2048
DEPTH = 4
HEAD_DIM = 128
GROUP_W = 4 * HEAD_DIM
N_COLBLK = 10
GATE_W = 2 * HEAD_DIM
MLSTM_HEADS = 4
MLSTM_CHUNK = 128
ATTN_KV_HEADS = 2
GRID_W = 64
ROPE_THETA = 10000.0
ROPE_FREQS = HEAD_DIM // 4
CA_HEADS = 4
N_MEM = 256
D_FF = 4 * D_MODEL
RMS_EPS = 1e-6
QK_SCALE = HEAD_DIM ** -0.5
LOG2_E = math.log2(math.e)
FRAME_MARGIN = 64.0

BLK_FOURIER, BLK_CB, BLK_CC, BLK_CX, BLK_MQ, BLK_MK, BLK_MV, BLK_MO, BLK_AQ, BLK_AKV = range(N_COLBLK)

V7X_VMEM_LIMIT = 56 * 1024 * 1024
NORM_ROWS = 256


def _params(semantics, vmem=V7X_VMEM_LIMIT):
    return pltpu.CompilerParams(dimension_semantics=semantics, vmem_limit_bytes=vmem)


def _head_norm(y):
    return y * lax.rsqrt(jnp.mean(y * y, axis=-1, keepdims=True) + RMS_EPS)


def _norm_rows_to(x_ref, g_ref, h_ref):
    tm = x_ref.shape[0]
    g = g_ref[...]

    def body(r, carry):
        rows = pl.ds(pl.multiple_of(r * NORM_ROWS, NORM_ROWS), NORM_ROWS)
        xf = x_ref[rows, :]
        ms = jnp.mean(xf * xf, axis=-1, keepdims=True)
        h_ref[rows, :] = (xf * lax.rsqrt(ms + RMS_EPS) * g).astype(BF16)
        return carry

    lax.fori_loop(0, tm // NORM_ROWS, body, 0)


def _norm_proj_kernel(x_ref, g_ref, w_ref, wg_ref, u_ref, gate_ref, h_ref):
    @pl.when(pl.program_id(1) == 0)
    def _():
        _norm_rows_to(x_ref, g_ref, h_ref)
        gate_ref[...] = jnp.dot(h_ref[...], wg_ref[...], preferred_element_type=F32)

    res = jnp.dot(h_ref[...], w_ref[...], preferred_element_type=F32)
    for k in range(u_ref.shape[0]):
        u_ref[k] = res[:, k * GROUP_W:(k + 1) * GROUP_W].astype(BF16)


def _norm_proj(x2, g, w_main, w_gate, tm=1024, nb=2):
    T = x2.shape[0]
    tn = nb * GROUP_W
    return pl.pallas_call(
        _norm_proj_kernel,
        grid=(T // tm, N_COLBLK // nb),
        in_specs=[
            pl.BlockSpec((tm, D_MODEL), lambda i, j: (i, 0)),
            pl.BlockSpec((1, D_MODEL), lambda i, j: (0, 0)),
            pl.BlockSpec((D_MODEL, tn), lambda i, j: (0, j)),
            pl.BlockSpec((D_MODEL, GATE_W), lambda i, j: (0, 0)),
        ],
        out_specs=[
            pl.BlockSpec((nb, tm, GROUP_W), lambda i, j: (j, i, 0)),
            pl.BlockSpec((tm, GATE_W), lambda i, j: (i, 0)),
        ],
        out_shape=[
            jax.ShapeDtypeStruct((N_COLBLK, T, GROUP_W), BF16),
            jax.ShapeDtypeStruct((T, GATE_W), F32),
        ],
        scratch_shapes=[pltpu.VMEM((tm, D_MODEL), BF16)],
        compiler_params=_params(("parallel", "arbitrary")),
        name="norm_proj",
    )(x2, g, w_main, w_gate)


def _fourier_tables(S):
    n2 = HEAD_DIM
    n1 = S // n2
    i1 = jnp.arange(n1, dtype=jnp.int32)
    a1 = (2.0 * math.pi / n1) * ((i1[:, None] * i1[None, :]) % n1).astype(F32)
    f1 = jnp.concatenate([jnp.cos(a1), -jnp.sin(a1)], axis=0).astype(BF16)
    i2 = jnp.arange(n2, dtype=jnp.int32)
    p = i1[:, None, None] + n1 * i2[None, :, None]
    k = (p * i2[None, None, :]) % S
    a2 = (2.0 * math.pi / S) * k.astype(F32)
    c2, s2 = jnp.cos(a2), jnp.sin(a2)
    g = jnp.concatenate([jnp.concatenate([c2, s2], axis=2),
                         jnp.concatenate([-s2, c2], axis=2)], axis=1).astype(BF16)
    ac = (2.0 * math.pi / HEAD_DIM) * ((i2[:, None] * i2[None, :]) % HEAD_DIM).astype(F32)
    cs = jnp.concatenate([jnp.cos(ac), jnp.sin(ac)], axis=0).astype(BF16)
    return f1, g, cs


def _fourier_stage1_kernel(a_ref, f1_ref, y_ref):
    n1 = a_ref.shape[0]
    res = jnp.dot(f1_ref[...], a_ref[...], preferred_element_type=F32)
    y_ref[0] = res[:n1].astype(BF16)
    y_ref[1] = res[n1:].astype(BF16)


def _fourier_stage2_kernel(y_ref, g_ref, cs_ref, o_ref, *, scale):
    n_p = y_ref.shape[1]
    cs = cs_ref[...]
    for p in range(n_p):
        ycat = jnp.concatenate([y_ref[0, p], y_ref[1, p]], axis=0)
        x = jnp.dot(g_ref[p], ycat, preferred_element_type=F32).astype(BF16)
        for grp in range(4):
            lanes = slice(grp * HEAD_DIM, (grp + 1) * HEAD_DIM)
            xg = jnp.concatenate([x[:HEAD_DIM, lanes], x[HEAD_DIM:, lanes]], axis=1)
            o = jnp.dot(xg, cs, preferred_element_type=F32) * scale
            col = p * GROUP_W + grp * HEAD_DIM
            o_ref[:, col:col + HEAD_DIM] = _head_norm(o).astype(BF16)


def _fourier(u3, B, S, tables):
    f1, g, cs = tables
    n2 = HEAD_DIM
    n1 = S // n2
    T = B * S
    ncol = min(16, n2)
    a_view = u3[BLK_FOURIER].reshape(B, n1, n2 * GROUP_W)
    y = pl.pallas_call(
        _fourier_stage1_kernel,
        grid=(B, n2 // ncol),
        in_specs=[
            pl.BlockSpec((None, n1, ncol * GROUP_W), lambda b, j: (b, 0, j)),
            pl.BlockSpec((2 * n1, n1), lambda b, j: (0, 0)),
        ],
        out_specs=pl.BlockSpec((None, 2, n1, ncol * GROUP_W), lambda b, j: (b, 0, 0, j)),
        out_shape=jax.ShapeDtypeStruct((B, 2, n1, n2 * GROUP_W), BF16),
        compiler_params=_params(("parallel", "parallel")),
        name="fourier_stage1",
    )(a_view, f1)
    y5 = y.reshape(B, 2, n1, n2, GROUP_W)
    n_p = 8
    out = pl.pallas_call(
        functools.partial(_fourier_stage2_kernel, scale=1.0 / math.sqrt(S * HEAD_DIM)),
        grid=(B, n1 // n_p),
        in_specs=[
            pl.BlockSpec((None, 2, n_p, n2, GROUP_W), lambda b, j: (b, 0, j, 0, 0)),
            pl.BlockSpec((n_p, 2 * n2, 2 * n2), lambda b, j: (j, 0, 0)),
            pl.BlockSpec((2 * HEAD_DIM, HEAD_DIM), lambda b, j: (0, 0)),
        ],
        out_specs=pl.BlockSpec((None, n2, n_p * GROUP_W), lambda b, j: (b, 0, j)),
        out_shape=jax.ShapeDtypeStruct((B, n2, n1 * GROUP_W), BF16),
        compiler_params=_params(("parallel", "parallel")),
        name="fourier_stage2",
    )(y5, g, cs)
    return out.reshape(T, GROUP_W)


def _log_sigmoid(x):
    return jnp.minimum(x, 0.0) - jnp.log1p(jnp.exp(-jnp.abs(x)))


def _mlstm_chunk(d, q, k, v, gi, gf, ib, fb, tri, mask, c_ref, n_ref, m_ref, out_ref, rows):
    li = gi + ib
    lf = _log_sigmoid(gf + fb)
    b = jnp.dot(tri, lf, precision=lax.Precision.HIGHEST, preferred_element_type=F32)
    g = jnp.sum(lf, axis=0, keepdims=True)
    w_end = g - b + li
    m_loc = jnp.max(w_end, axis=0, keepdims=True)
    e = jnp.exp(w_end - m_loc)
    a_t = (li - b).T
    m_old = m_ref[d:d + 1, :]
    m_new = jnp.maximum(g + m_old, m_loc)
    a_sc = jnp.exp(g + m_old - m_new)
    b_sc = jnp.exp(m_loc - m_new)
    lane = lax.broadcasted_iota(jnp.int32, (1, HEAD_DIM), 1)

    def pick(x, hp):
        return jnp.sum(jnp.where(lane == hp, x, 0.0), axis=-1, keepdims=True)

    for h in range(MLSTM_HEADS):
        hp = MLSTM_HEADS * d + h
        lanes = slice(h * HEAD_DIM, (h + 1) * HEAD_DIM)
        qh = q[:, lanes]
        vh = v[:, lanes]
        ks = k[:, lanes].astype(F32) * QK_SCALE
        bc = pick(b, hp)
        dm = jnp.where(mask, bc + a_t[hp:hp + 1, :], -jnp.inf)
        inter_log = bc + pick(m_old, hp)
        m_t = jnp.maximum(inter_log, jnp.max(dm, axis=-1, keepdims=True))
        p = jnp.exp(dm - m_t)
        inter_w = jnp.exp(inter_log - m_t)
        qk = lax.dot_general(qh, ks.astype(BF16), (((1,), (1,)), ((), ())), preferred_element_type=F32)
        a_mat = p * qk
        c_prev = c_ref[hp]
        n_prev = n_ref[hp:hp + 1, :]
        num = (jnp.dot(a_mat.astype(BF16), vh, preferred_element_type=F32)
               + inter_w * jnp.dot(qh, c_prev.astype(BF16), preferred_element_type=F32))
        den = (jnp.sum(a_mat, axis=-1, keepdims=True)
               + inter_w * jnp.sum(qh.astype(F32) * n_prev, axis=-1, keepdims=True))
        hout = num / jnp.maximum(jnp.abs(den), jnp.exp(-m_t))
        out_ref[rows, lanes] = hout.astype(out_ref.dtype)
        ke = ks * pick(e, hp)
        c_loc = jnp.dot(ke.T.astype(BF16), vh, preferred_element_type=F32)
        n_loc = jnp.sum(ke, axis=0, keepdims=True)
        ah = pick(a_sc, hp)
        bh = pick(b_sc, hp)
        c_ref[hp] = ah * c_prev + bh * c_loc
        n_ref[hp:hp + 1, :] = ah * n_prev + bh * n_loc
    m_ref[d:d + 1, :] = m_new


def _mlstm_kernel(qf_ref, kf_ref, vf_ref, gf_ref, qb_ref, kb_ref, vb_ref, gb_ref, ib_ref, fb_ref,
                  hf_ref, hb_ref, c_ref, n_ref, m_ref):
    L = MLSTM_CHUNK
    n_sub = qf_ref.shape[0] // L

    @pl.when(pl.program_id(1) == 0)
    def _():
        c_ref[...] = jnp.zeros_like(c_ref)
        n_ref[...] = jnp.zeros_like(n_ref)
        m_ref[...] = jnp.zeros_like(m_ref)

    row = lax.broadcasted_iota(jnp.int32, (L, L), 0)
    col = lax.broadcasted_iota(jnp.int32, (L, L), 1)
    mask_f = col <= row
    mask_b = col >= row
    tri_f = jnp.where(mask_f, 1.0, 0.0).astype(F32)
    tri_b = jnp.where(mask_b, 1.0, 0.0).astype(F32)
    ib = ib_ref[...]
    fb = fb_ref[...]

    def body(j, carry):
        rf = pl.ds(pl.multiple_of(j * L, L), L)
        rb = pl.ds(pl.multiple_of((n_sub - 1 - j) * L, L), L)
        _mlstm_chunk(0, qf_ref[rf, :], kf_ref[rf, :], vf_ref[rf, :], gf_ref[rf, :HEAD_DIM], gf_ref[rf, HEAD_DIM:],
                     ib, fb, tri_f, mask_f, c_ref, n_ref, m_ref, hf_ref, rf)
        _mlstm_chunk(1, qb_ref[rb, :], kb_ref[rb, :], vb_ref[rb, :], gb_ref[rb, :HEAD_DIM], gb_ref[rb, HEAD_DIM:],
                     ib, fb, tri_b, mask_b, c_ref, n_ref, m_ref, hb_ref, rb)
        return carry

    lax.fori_loop(0, n_sub, body, 0)


def _mlstm(u3, gates, ib, fb, B, S, n_sub=4):
    T = B * S
    rows = n_sub * MLSTM_CHUNK
    nblk = S // rows
    u4 = u3.reshape(N_COLBLK, B, S, GROUP_W)
    g3 = gates.reshape(B, S, GATE_W)

    def fwd(blk):
        return pl.BlockSpec((None, None, rows, GROUP_W), lambda b, c: (blk, b, c, 0))

    def bwd(blk):
        return pl.BlockSpec((None, None, rows, GROUP_W), lambda b, c: (blk, b, nblk - 1 - c, 0))

    hf, hb = pl.pallas_call(
        _mlstm_kernel,
        grid=(B, nblk),
        in_specs=[
            fwd(BLK_MQ), fwd(BLK_MK), fwd(BLK_MV),
            pl.BlockSpec((None, rows, GATE_W), lambda b, c: (b, c, 0)),
            bwd(BLK_MQ), bwd(BLK_MK), bwd(BLK_MV),
            pl.BlockSpec((None, rows, GATE_W), lambda b, c: (b, nblk - 1 - c, 0)),
            pl.BlockSpec((1, HEAD_DIM), lambda b, c: (0, 0)),
            pl.BlockSpec((1, HEAD_DIM), lambda b, c: (0, 0)),
        ],
        out_specs=[
            pl.BlockSpec((None, rows, GROUP_W), lambda b, c: (b, c, 0)),
            pl.BlockSpec((None, rows, GROUP_W), lambda b, c: (b, nblk - 1 - c, 0)),
        ],
        out_shape=[jax.ShapeDtypeStruct((B, S, GROUP_W), BF16)] * 2,
        scratch_shapes=[
            pltpu.VMEM((2 * MLSTM_HEADS, HEAD_DIM, HEAD_DIM), F32),
            pltpu.VMEM((2 * MLSTM_HEADS, HEAD_DIM), F32),
            pltpu.VMEM((8, HEAD_DIM), F32),
        ],
        compiler_params=_params(("parallel", "arbitrary")),
        name="mlstm",
    )(u4, u4, u4, g3, u4, u4, u4, g3, ib, fb)
    return hf.reshape(T, GROUP_W), hb.reshape(T, GROUP_W)


def _rope_tables(S):
    rows = S // GRID_W
    row = jnp.repeat(jnp.arange(rows), GRID_W).astype(F32)
    colp = jnp.tile(jnp.arange(GRID_W), rows).astype(F32)
    inv = ROPE_THETA ** (-jnp.arange(ROPE_FREQS, dtype=F32) / ROPE_FREQS)
    ar = row[:, None] * inv
    ac = colp[:, None] * inv
    ang = jnp.concatenate([ar, ar, ac, ac], axis=1)
    first_half = (jnp.arange(HEAD_DIM) % (2 * ROPE_FREQS)) < ROPE_FREQS
    return jnp.cos(ang), jnp.where(first_half, -1.0, 1.0) * jnp.sin(ang)


def _rope(x, cos, sin_signed, first_half):
    partner = jnp.where(first_half, pltpu.roll(x, HEAD_DIM - ROPE_FREQS, axis=1), pltpu.roll(x, ROPE_FREQS, axis=1))
    return x * cos + partner * sin_signed


def _attn_prep_kernel(aq_ref, akv_ref, cos_ref, sin_ref, qg_ref, kg_ref, q_ref, kt_ref, va_ref, kn_ref):
    cos = cos_ref[...]
    sin = sin_ref[...]
    lane = lax.broadcasted_iota(jnp.int32, cos.shape, 1)
    first_half = (lane % (2 * ROPE_FREQS)) < ROPE_FREQS
    qg = qg_ref[...]
    kg = kg_ref[...]
    for h in range(4):
        lanes = slice(h * HEAD_DIM, (h + 1) * HEAD_DIM)
        qh = _head_norm(aq_ref[:, lanes].astype(F32)) * qg
        q_ref[:, lanes] = (_rope(qh, cos, sin, first_half) * (QK_SCALE * LOG2_E)).astype(BF16)
    for h in range(ATTN_KV_HEADS):
        lanes = slice(h * HEAD_DIM, (h + 1) * HEAD_DIM)
        kh = _head_norm(akv_ref[:, lanes].astype(F32)) * kg
        kb = _rope(kh, cos, sin, first_half).astype(BF16)
        kt_ref[h] = kb.astype(F32).T.astype(BF16)
        k2 = jnp.max(jnp.sum(jnp.square(kb.astype(F32)), axis=-1, keepdims=True), axis=0, keepdims=True)
        kn_ref[h] = jnp.broadcast_to(k2, kn_ref.shape[1:])
        vl = slice((ATTN_KV_HEADS + h) * HEAD_DIM, (ATTN_KV_HEADS + h + 1) * HEAD_DIM)
        va_ref[h, :, :HEAD_DIM] = akv_ref[:, vl]
        va_ref[h, :, HEAD_DIM:] = jnp.ones((akv_ref.shape[0], HEAD_DIM), BF16)


def _flash_kernel(q_ref, kt_ref, va_ref, kn_ref, o_ref, m_ref, acc_ref, p_ref, mp_ref):
    tq = q_ref.shape[0]
    nkc, _, tk = kt_ref.shape
    n_rep = tk // HEAD_DIM
    q2 = jnp.concatenate([q_ref[:, :HEAD_DIM], q_ref[:, HEAD_DIM:]], axis=0)

    def chunk(c):
        return kt_ref[c], va_ref[pl.ds(pl.multiple_of(c * tk, tk), tk), :]

    def lane_max(s):
        part = s[:, :HEAD_DIM]
        for t in range(1, n_rep):
            part = jnp.maximum(part, s[:, t * HEAD_DIM:(t + 1) * HEAD_DIM])
        return part

    qf = q2.astype(F32)
    q_norm = jnp.sqrt(jnp.sum(qf * qf, axis=-1, keepdims=True))
    k_norm = jnp.sqrt(jnp.max(jnp.max(kn_ref[...], axis=0), axis=0, keepdims=True))
    safe = jnp.max(q_norm * k_norm) <= FRAME_MARGIN
    acc_ref[...] = jnp.zeros_like(acc_ref)

    def fast_scores(c, slot):
        kt, _ = chunk(c)
        s = jnp.dot(q2, kt, preferred_element_type=F32)
        p_ref[slot] = jnp.exp2(s - jnp.tile(m_ref[...], (1, n_rep))).astype(BF16)
        mp_ref[slot] = lane_max(s)

    def fast_accumulate(c, slot):
        _, vc = chunk(c)
        m_prev = m_ref[...]
        m_new = jnp.maximum(m_prev, jnp.max(mp_ref[slot], axis=-1, keepdims=True))
        alpha = jnp.exp2(m_prev - m_new)
        acc_ref[...] = (acc_ref[...] + jnp.dot(p_ref[slot], vc, preferred_element_type=F32)) * jnp.tile(alpha, (1, 2))
        m_ref[...] = m_new

    def fast():
        m_ref[...] = jnp.zeros_like(m_ref)

        group = 8 if nkc % 8 == 0 else 2

        def body(i, carry):
            for t in range(group):
                fast_scores(group * i + t, t % 2)
                fast_accumulate(group * i + t, t % 2)
            return carry

        lax.fori_loop(0, nkc // group, body, 0)

    def exact():
        m_ref[...] = jnp.full_like(m_ref, -jnp.inf)

        def body(c, carry):
            kt, vc = chunk(c)
            s = jnp.dot(q2, kt, preferred_element_type=F32)
            m_prev = m_ref[...]
            m_new = jnp.maximum(m_prev, jnp.max(lane_max(s), axis=-1, keepdims=True))
            p = jnp.exp2(s - jnp.tile(m_new, (1, n_rep))).astype(BF16)
            acc_ref[...] = (jnp.tile(jnp.exp2(m_prev - m_new), (1, 2)) * acc_ref[...]
                            + jnp.dot(p, vc, preferred_element_type=F32))
            m_ref[...] = m_new
            return carry

        lax.fori_loop(0, nkc, body, 0)

    lax.cond(safe, fast, exact)
    o = acc_ref[:, :HEAD_DIM] / acc_ref[:, HEAD_DIM:]
    o_ref[:, :HEAD_DIM] = o[:tq].astype(o_ref.dtype)
    o_ref[:, HEAD_DIM:] = o[tq:].astype(o_ref.dtype)


def _axial_gqa(u3, q_gain, k_gain, rope, B, S, tq=512, tk=1024):
    T = B * S
    cos, sin = rope
    u4 = u3.reshape(N_COLBLK, B, S, GROUP_W)
    tk = min(tk, S // 2)
    nkc = S // tk
    assert nkc >= 2 and nkc % 2 == 0, "flash kernel pipelines kv chunks in pairs"
    q_r, kt, va, kn = pl.pallas_call(
        _attn_prep_kernel,
        grid=(B, nkc),
        in_specs=[
            pl.BlockSpec((None, None, tk, GROUP_W), lambda b, i: (BLK_AQ, b, i, 0)),
            pl.BlockSpec((None, None, tk, GROUP_W), lambda b, i: (BLK_AKV, b, i, 0)),
            pl.BlockSpec((tk, HEAD_DIM), lambda b, i: (i, 0)),
            pl.BlockSpec((tk, HEAD_DIM), lambda b, i: (i, 0)),
            pl.BlockSpec((1, HEAD_DIM), lambda b, i: (0, 0)),
            pl.BlockSpec((1, HEAD_DIM), lambda b, i: (0, 0)),
        ],
        out_specs=[
            pl.BlockSpec((None, tk, GROUP_W), lambda b, i: (b, i, 0)),
            pl.BlockSpec((None, ATTN_KV_HEADS, None, HEAD_DIM, tk), lambda b, i: (b, 0, i, 0, 0)),
            pl.BlockSpec((None, ATTN_KV_HEADS, tk, 2 * HEAD_DIM), lambda b, i: (b, 0, i, 0)),
            pl.BlockSpec((None, ATTN_KV_HEADS, None, 8, HEAD_DIM), lambda b, i: (b, 0, i, 0, 0)),
        ],
        out_shape=[
            jax.ShapeDtypeStruct((B, S, GROUP_W), BF16),
            jax.ShapeDtypeStruct((B, ATTN_KV_HEADS, nkc, HEAD_DIM, tk), BF16),
            jax.ShapeDtypeStruct((B, ATTN_KV_HEADS, S, 2 * HEAD_DIM), BF16),
            jax.ShapeDtypeStruct((B, ATTN_KV_HEADS, nkc, 8, HEAD_DIM), F32),
        ],
        compiler_params=_params(("parallel", "parallel")),
        name="attn_prep",
    )(u4, u4, cos, sin, q_gain, k_gain)
    out = pl.pallas_call(
        _flash_kernel,
        grid=(B, ATTN_KV_HEADS, S // tq),
        in_specs=[
            pl.BlockSpec((None, tq, 2 * HEAD_DIM), lambda b, k, i: (b, i, k)),
            pl.BlockSpec((None, None, nkc, HEAD_DIM, tk), lambda b, k, i: (b, k, 0, 0, 0)),
            pl.BlockSpec((None, None, S, 2 * HEAD_DIM), lambda b, k, i: (b, k, 0, 0)),
            pl.BlockSpec((None, None, nkc, 8, HEAD_DIM), lambda b, k, i: (b, k, 0, 0, 0)),
        ],
        out_specs=pl.BlockSpec((None, tq, 2 * HEAD_DIM), lambda b, k, i: (b, i, k)),
        out_shape=jax.ShapeDtypeStruct((B, S, GROUP_W), BF16),
        scratch_shapes=[
            pltpu.VMEM((2 * tq, HEAD_DIM), F32),
            pltpu.VMEM((2 * tq, 2 * HEAD_DIM), F32),
            pltpu.VMEM((2, 2 * tq, tk), BF16),
            pltpu.VMEM((2, 2 * tq, HEAD_DIM), F32),
        ],
        compiler_params=_params(("parallel", "parallel", "arbitrary")),
        name="flash_gqa",
    )(q_r, kt, va, kn)
    return out.reshape(T, GROUP_W)


def _combine_kernel(x_ref, ya_ref, cb_ref, cc_ref, cx_ref, ccp_ref, cxp_ref, ccn_ref, cxn_ref, mo_ref,
                    hf_ref, hb_ref, yd_ref, cw_ref, cbias_ref, gain_ref, w_ref, o_ref, y_ref, *, tiles_per_seq):
    tm = x_ref.shape[0]
    i = pl.program_id(0)
    halo = ccp_ref.shape[0]
    at_start = (i % tiles_per_seq) == 0
    at_end = (i % tiles_per_seq) == tiles_per_seq - 1
    gain = gain_ref[...]

    z = cc_ref[...].astype(F32) * cx_ref[...].astype(F32)
    z_before = ccp_ref[halo - 1:halo, :].astype(F32) * cxp_ref[halo - 1:halo, :].astype(F32)
    z_after = ccn_ref[0:1, :].astype(F32) * cxn_ref[0:1, :].astype(F32)
    z_before = jnp.where(at_start, 0.0, z_before)
    z_after = jnp.where(at_end, 0.0, z_after)
    rid = lax.broadcasted_iota(jnp.int32, z.shape, 0)
    z_m1 = jnp.where(rid == 0, z_before, pltpu.roll(z, 1, axis=0))
    z_p1 = jnp.where(rid == tm - 1, z_after, pltpu.roll(z, tm - 1, axis=0))
    conv = z_m1 * cw_ref[0:1, :] + z * cw_ref[1:2, :] + z_p1 * cw_ref[2:3, :] + cbias_ref[...]
    yb = cb_ref[...].astype(F32) * conv

    for grp in range(4):
        lanes = slice(grp * HEAD_DIM, (grp + 1) * HEAD_DIM)
        y_ref[:, lanes] = (ya_ref[:, lanes].astype(F32) * gain[:, lanes]).astype(BF16)
        gl = slice(GROUP_W + grp * HEAD_DIM, GROUP_W + (grp + 1) * HEAD_DIM)
        y_ref[:, gl] = (_head_norm(yb[:, lanes]) * gain[:, gl]).astype(BF16)
        gl = slice(2 * GROUP_W + grp * HEAD_DIM, 2 * GROUP_W + (grp + 1) * HEAD_DIM)
        yc = _head_norm(hf_ref[:, lanes].astype(F32) + hb_ref[:, lanes].astype(F32))
        y_ref[:, gl] = (jax.nn.sigmoid(mo_ref[:, lanes].astype(F32)) * yc * gain[:, gl]).astype(BF16)
        gl = slice(3 * GROUP_W + grp * HEAD_DIM, 3 * GROUP_W + (grp + 1) * HEAD_DIM)
        y_ref[:, gl] = (_head_norm(yd_ref[:, lanes].astype(F32)) * gain[:, gl]).astype(BF16)

    o_ref[...] = x_ref[...] + jnp.dot(y_ref[...], w_ref[...], preferred_element_type=F32)


def _combine_out(x2, ya, u3, hf, hb, yd, conv_w, conv_b, out_gain, w_out, S, tm=512, halo=16):
    T = x2.shape[0]
    hb_per_tile = tm // halo
    n_halo = T // halo

    def u_blk(blk):
        return pl.BlockSpec((None, tm, GROUP_W), lambda i: (blk, i, 0))

    def u_prev(blk):
        return pl.BlockSpec((None, halo, GROUP_W), lambda i: (blk, jnp.maximum(i * hb_per_tile - 1, 0), 0))

    def u_next(blk):
        return pl.BlockSpec((None, halo, GROUP_W), lambda i: (blk, jnp.minimum((i + 1) * hb_per_tile, n_halo - 1), 0))

    tok = pl.BlockSpec((tm, GROUP_W), lambda i: (i, 0))
    return pl.pallas_call(
        functools.partial(_combine_kernel, tiles_per_seq=S // tm),
        grid=(T // tm,),
        in_specs=[
            pl.BlockSpec((tm, D_MODEL), lambda i: (i, 0)),
            tok,
            u_blk(BLK_CB), u_blk(BLK_CC), u_blk(BLK_CX),
            u_prev(BLK_CC), u_prev(BLK_CX), u_next(BLK_CC), u_next(BLK_CX),
            u_blk(BLK_MO),
            tok, tok, tok,
            pl.BlockSpec((8, GROUP_W), lambda i: (0, 0)),
            pl.BlockSpec((1, GROUP_W), lambda i: (0, 0)),
            pl.BlockSpec((1, D_MODEL), lambda i: (0, 0)),
            pl.BlockSpec((D_MODEL, D_MODEL), lambda i: (0, 0)),
        ],
        out_specs=pl.BlockSpec((tm, D_MODEL), lambda i: (i, 0)),
        out_shape=jax.ShapeDtypeStruct((T, D_MODEL), F32),
        scratch_shapes=[pltpu.VMEM((tm, D_MODEL), BF16)],
        compiler_params=_params(("parallel",)),
        name="combine_out",
    )(x2, ya, u3, u3, u3, u3, u3, u3, u3, u3, hf, hb, yd, conv_w, conv_b, out_gain, w_out)


def _mem_kv_kernel(mem_ref, g_ref, w_ref, kg_ref, kt_ref, v_ref, h_ref):
    _norm_rows_to(mem_ref, g_ref, h_ref)
    kv = jnp.dot(h_ref[...], w_ref[...], preferred_element_type=F32)
    kg = kg_ref[...]
    for h in range(CA_HEADS):
        lanes = slice(h * HEAD_DIM, (h + 1) * HEAD_DIM)
        kt_ref[h] = (_head_norm(kv[:, lanes]) * kg).T.astype(BF16)
    v_ref[...] = kv[:, CA_HEADS * HEAD_DIM:].astype(BF16)


def _mem_kv(mem, g_mem, w_kv, k_gain):
    B = mem.shape[0]
    return pl.pallas_call(
        _mem_kv_kernel,
        grid=(B,),
        in_specs=[
            pl.BlockSpec((None, N_MEM, D_MODEL), lambda b: (b, 0, 0)),
            pl.BlockSpec((1, D_MODEL), lambda b: (0, 0)),
            pl.BlockSpec((D_MODEL, 2 * GROUP_W), lambda b: (0, 0)),
            pl.BlockSpec((1, HEAD_DIM), lambda b: (0, 0)),
        ],
        out_specs=[
            pl.BlockSpec((None, CA_HEADS, HEAD_DIM, N_MEM), lambda b: (b, 0, 0, 0)),
            pl.BlockSpec((None, N_MEM, GROUP_W), lambda b: (b, 0, 0)),
        ],
        out_shape=[
            jax.ShapeDtypeStruct((B, CA_HEADS, HEAD_DIM, N_MEM), BF16),
            jax.ShapeDtypeStruct((B, N_MEM, GROUP_W), BF16),
        ],
        scratch_shapes=[pltpu.VMEM((N_MEM, D_MODEL), BF16)],
        compiler_params=_params(("parallel",)),
        name="mem_kv",
    )(mem, g_mem, w_kv, k_gain)


def _cross_attn_kernel(x_ref, g_ref, wq_ref, qg_ref, kt_ref, v_ref, wo_ref, o_ref, h_ref, oc_ref):
    _norm_rows_to(x_ref, g_ref, h_ref)
    q = jnp.dot(h_ref[...], wq_ref[...], preferred_element_type=F32)
    qg = qg_ref[...]
    for h in range(CA_HEADS):
        lanes = slice(h * HEAD_DIM, (h + 1) * HEAD_DIM)
        qh = (_head_norm(q[:, lanes]) * (qg * QK_SCALE)).astype(BF16)
        s = jnp.dot(qh, kt_ref[h], preferred_element_type=F32)
        e = jnp.exp(s - jnp.max(s, axis=-1, keepdims=True))
        o = jnp.dot(e.astype(BF16), v_ref[:, lanes], preferred_element_type=F32)
        oc_ref[:, lanes] = (o * pl.reciprocal(jnp.sum(e, axis=-1, keepdims=True), approx=False)).astype(BF16)
    o_ref[...] = x_ref[...] + jnp.dot(oc_ref[...], wo_ref[...], preferred_element_type=F32)


def _cross_attn(x2, g_ca, w_q, q_gain, kt, v, w_o, B, S, tm=1024):
    T = x2.shape[0]
    x3 = x2.reshape(B, S, D_MODEL)
    out = pl.pallas_call(
        _cross_attn_kernel,
        grid=(B, S // tm),
        in_specs=[
            pl.BlockSpec((None, tm, D_MODEL), lambda b, i: (b, i, 0)),
            pl.BlockSpec((1, D_MODEL), lambda b, i: (0, 0)),
            pl.BlockSpec((D_MODEL, GROUP_W), lambda b, i: (0, 0)),
            pl.BlockSpec((1, HEAD_DIM), lambda b, i: (0, 0)),
            pl.BlockSpec((None, CA_HEADS, HEAD_DIM, N_MEM), lambda b, i: (b, 0, 0, 0)),
            pl.BlockSpec((None, N_MEM, GROUP_W), lambda b, i: (b, 0, 0)),
            pl.BlockSpec((GROUP_W, D_MODEL), lambda b, i: (0, 0)),
        ],
        out_specs=pl.BlockSpec((None, tm, D_MODEL), lambda b, i: (b, i, 0)),
        out_shape=jax.ShapeDtypeStruct((B, S, D_MODEL), F32),
        scratch_shapes=[pltpu.VMEM((tm, D_MODEL), BF16), pltpu.VMEM((tm, GROUP_W), BF16)],
        compiler_params=_params(("parallel", "parallel")),
        name="cross_attn",
    )(x3, g_ca, w_q, q_gain, kt, v, w_o)
    return out.reshape(T, D_MODEL)


def _mlp_kernel(x_ref, g_ref, wu_ref, wd_ref, o_ref, h_ref):
    @pl.when(pl.program_id(1) == 0)
    def _():
        _norm_rows_to(x_ref, g_ref, h_ref)
        o_ref[...] = x_ref[...]

    up = jnp.dot(h_ref[...], wu_ref[...], preferred_element_type=F32)
    act = jnp.square(jnp.maximum(up, 0.0)).astype(BF16)
    o_ref[...] += jnp.dot(act, wd_ref[...], preferred_element_type=F32)


def _mlp(x2, g, w_up, w_down, tm=512, tf=1024):
    T = x2.shape[0]
    return pl.pallas_call(
        _mlp_kernel,
        grid=(T // tm, D_FF // tf),
        in_specs=[
            pl.BlockSpec((tm, D_MODEL), lambda i, j: (i, 0)),
            pl.BlockSpec((1, D_MODEL), lambda i, j: (0, 0)),
            pl.BlockSpec((D_MODEL, tf), lambda i, j: (0, j)),
            pl.BlockSpec((tf, D_MODEL), lambda i, j: (j, 0)),
        ],
        out_specs=pl.BlockSpec((tm, D_MODEL), lambda i, j: (i, 0)),
        out_shape=jax.ShapeDtypeStruct((T, D_MODEL), F32),
        scratch_shapes=[pltpu.VMEM((tm, D_MODEL), BF16)],
        compiler_params=_params(("parallel", "arbitrary")),
        name="mlp",
    )(x2, g, w_up, w_down)


def _prep_weights(w_in, conv_w, i_bias, f_bias, w_out, w_ca_q, w_ca_kv, w_ca_o, w_up, w_down):
    n_main = 8 * GROUP_W
    w_main = jnp.concatenate([w_in[:, :, :n_main], w_in[:, :, n_main + 16:]], axis=2).astype(BF16)
    pad = jnp.zeros((DEPTH, D_MODEL, HEAD_DIM - 8), F32)
    w_gate = jnp.concatenate([w_in[:, :, n_main:n_main + 8], pad, w_in[:, :, n_main + 8:n_main + 16], pad],
                             axis=2).astype(BF16)
    lane_pad = jnp.zeros((DEPTH, 1, HEAD_DIM - 8), F32)
    ib = jnp.concatenate([i_bias.reshape(DEPTH, 1, 8), lane_pad], axis=2)
    fb = jnp.concatenate([f_bias.reshape(DEPTH, 1, 8), lane_pad], axis=2)
    cw = jnp.concatenate([conv_w, jnp.zeros((DEPTH, 5, GROUP_W), F32)], axis=1)
    return dict(w_main=w_main, w_gate=w_gate, ib=ib, fb=fb, cw=cw, w_out=w_out.astype(BF16),
                w_ca_q=w_ca_q.astype(BF16), w_ca_kv=w_ca_kv.astype(BF16), w_ca_o=w_ca_o.astype(BF16),
                w_up=w_up.astype(BF16), w_down=w_down.astype(BF16))


def _trunk(x, mem, pw, g_mix, conv_b, attn_q_norm, attn_k_norm, out_gain, g_ca, g_mem, ca_q_norm, ca_k_norm, g_mlp):
    B, S, _ = x.shape
    T = B * S
    x2 = x.reshape(T, D_MODEL)
    rope = _rope_tables(S)
    ftab = _fourier_tables(S)
    for l in range(DEPTH):
        u3, gates = _norm_proj(x2, g_mix[l][None], pw["w_main"][l], pw["w_gate"][l])
        ya = _fourier(u3, B, S, ftab)
        hf, hb = _mlstm(u3, gates, pw["ib"][l], pw["fb"][l], B, S)
        yd = _axial_gqa(u3, attn_q_norm[l][None], attn_k_norm[l][None], rope, B, S)
        x2 = _combine_out(x2, ya, u3, hf, hb, yd, pw["cw"][l], conv_b[l][None], out_gain[l][None], pw["w_out"][l], S)
        kt, v = _mem_kv(mem, g_mem[l][None], pw["w_ca_kv"][l], ca_k_norm[l][None])
        x2 = _cross_attn(x2, g_ca[l][None], pw["w_ca_q"][l], ca_q_norm[l][None], kt, v, pw["w_ca_o"][l], B, S)
        x2 = _mlp(x2, g_mlp[l][None], pw["w_up"][l], pw["w_down"][l])
    return x2.reshape(B, S, D_MODEL)


def kernel(x_prompt, x_sample, mem_prompt, mem_sample, g_mix, w_in, conv_w, conv_b, i_bias, f_bias, attn_q_norm,
           attn_k_norm, out_gain, w_out, g_ca, g_mem, w_ca_q, w_ca_kv, ca_q_norm, ca_k_norm, w_ca_o, g_mlp, w_up,
           w_down):
    pw = _prep_weights(w_in, conv_w, i_bias, f_bias, w_out, w_ca_q, w_ca_kv, w_ca_o, w_up, w_down)
    args = (pw, g_mix, conv_b, attn_q_norm, attn_k_norm, out_gain, g_ca, g_mem, ca_q_norm, ca_k_norm, g_mlp)
    y_prompt = _trunk(x_prompt, mem_prompt, *args)
    y_sample = _trunk(x_sample, mem_sample, *args)
    return (y_prompt, y_sample)
```

```python
import functools
import math

import jax
import jax.numpy as jnp
from jax import lax
from jax.experimental import pallas as pl
from jax.experimental.pallas import tpu as pltpu

F32 = jnp.float32
BF16 = jnp.bfloat16

D_MODEL = 2048
DEPTH = 4
HEAD_DIM = 128
GROUP_W = 4 * HEAD_DIM
N_COLBLK = 10
GATE_W = 2 * HEAD_DIM
MLSTM_HEADS = 4
MLSTM_CHUNK = 128
ATTN_KV_HEADS = 2
GRID_W = 64
ROPE_THETA = 10000.0
ROPE_FREQS = HEAD_DIM // 4
CA_HEADS = 4
N_MEM = 256
D_FF = 4 * D_MODEL
RMS_EPS = 1e-6
QK_SCALE = HEAD_DIM ** -0.5
LOG2_E = math.log2(math.e)
FRAME_MARGIN = 64.0

BLK_FOURIER, BLK_CB, BLK_CC, BLK_CX, BLK_MQ, BLK_MK, BLK_MV, BLK_MO, BLK_AQ, BLK_AKV = range(N_COLBLK)

V7X_VMEM_LIMIT = 56 * 1024 * 1024
NORM_ROWS = 256


def _params(semantics, vmem=V7X_VMEM_LIMIT):
    return pltpu.CompilerParams(dimension_semantics=semantics, vmem_limit_bytes=vmem)


def _head_norm(y):
    return y * lax.rsqrt(jnp.mean(y * y, axis=-1, keepdims=True) + RMS_EPS)


def _norm_rows_to(x_ref, g_ref, h_ref):
    tm = x_ref.shape[0]
    g = g_ref[...]

    def body(r, carry):
        rows = pl.ds(pl.multiple_of(r * NORM_ROWS, NORM_ROWS), NORM_ROWS)
        xf = x_ref[rows, :]
        ms = jnp.mean(xf * xf, axis=-1, keepdims=True)
        h_ref[rows, :] = (xf * lax.rsqrt(ms + RMS_EPS) * g).astype(BF16)
        return carry

    lax.fori_loop(0, tm // NORM_ROWS, body, 0)


def _norm_proj_kernel(x_ref, g_ref, w_ref, wg_ref, u_ref, gate_ref, h_ref):
    @pl.when(pl.program_id(1) == 0)
    def _():
        _norm_rows_to(x_ref, g_ref, h_ref)
        gate_ref[...] = jnp.dot(h_ref[...], wg_ref[...], preferred_element_type=F32)

    res = jnp.dot(h_ref[...], w_ref[...], preferred_element_type=F32)
    for k in range(u_ref.shape[0]):
        u_ref[k] = res[:, k * GROUP_W:(k + 1) * GROUP_W].astype(BF16)


def _norm_proj(x2, g, w_main, w_gate, tm=1024, nb=2):
    T = x2.shape[0]
    tn = nb * GROUP_W
    return pl.pallas_call(
        _norm_proj_kernel,
        grid=(T // tm, N_COLBLK // nb),
        in_specs=[
            pl.BlockSpec((tm, D_MODEL), lambda i, j: (i, 0)),
            pl.BlockSpec((1, D_MODEL), lambda i, j: (0, 0)),
            pl.BlockSpec((D_MODEL, tn), lambda i, j: (0, j)),
            pl.BlockSpec((D_MODEL, GATE_W), lambda i, j: (0, 0)),
        ],
        out_specs=[
            pl.BlockSpec((nb, tm, GROUP_W), lambda i, j: (j, i, 0)),
            pl.BlockSpec((tm, GATE_W), lambda i, j: (i, 0)),
        ],
        out_shape=[
            jax.ShapeDtypeStruct((N_COLBLK, T, GROUP_W), BF16),
            jax.ShapeDtypeStruct((T, GATE_W), F32),
        ],
        scratch_shapes=[pltpu.VMEM((tm, D_MODEL), BF16)],
        compiler_params=_params(("parallel", "arbitrary")),
        name="norm_proj",
    )(x2, g, w_main, w_gate)


def _fourier_tables(S):
    n2 = HEAD_DIM
    n1 = S // n2
    i1 = jnp.arange(n1, dtype=jnp.int32)
    a1 = (2.0 * math.pi / n1) * ((i1[:, None] * i1[None, :]) % n1).astype(F32)
    f1 = jnp.concatenate([jnp.cos(a1), -jnp.sin(a1)], axis=0).astype(BF16)
    i2 = jnp.arange(n2, dtype=jnp.int32)
    p = i1[:, None, None] + n1 * i2[None, :, None]
    k = (p * i2[None, None, :]) % S
    a2 = (2.0 * math.pi / S) * k.astype(F32)
    c2, s2 = jnp.cos(a2), jnp.sin(a2)
    g = jnp.concatenate([jnp.concatenate([c2, s2], axis=2),
                         jnp.concatenate([-s2, c2], axis=2)], axis=1).astype(BF16)
    ac = (2.0 * math.pi / HEAD_DIM) * ((i2[:, None] * i2[None, :]) % HEAD_DIM).astype(F32)
    cs = jnp.concatenate([jnp.cos(ac), jnp.sin(ac)], axis=0).astype(BF16)
    return f1, g, cs


def _fourier_stage1_kernel(a_ref, f1_ref, y_ref):
    n1 = a_ref.shape[0]
    res = jnp.dot(f1_ref[...], a_ref[...], preferred_element_type=F32)
    y_ref[0] = res[:n1].astype(BF16)
    y_ref[1] = res[n1:].astype(BF16)


def _fourier_stage2_kernel(y_ref, g_ref, cs_ref, o_ref, *, scale):
    n_p = y_ref.shape[1]
    cs = cs_ref[...]
    for p in range(n_p):
        ycat = jnp.concatenate([y_ref[0, p], y_ref[1, p]], axis=0)
        x = jnp.dot(g_ref[p], ycat, preferred_element_type=F32).astype(BF16)
        for grp in range(4):
            lanes = slice(grp * HEAD_DIM, (grp + 1) * HEAD_DIM)
            xg = jnp.concatenate([x[:HEAD_DIM, lanes], x[HEAD_DIM:, lanes]], axis=1)
            o = jnp.dot(xg, cs, preferred_element_type=F32) * scale
            col = p * GROUP_W + grp * HEAD_DIM
            o_ref[:, col:col + HEAD_DIM] = _head_norm(o).astype(BF16)


def _fourier(u3, B, S, tables):
    f1, g, cs = tables
    n2 = HEAD_DIM
    n1 = S // n2
    T = B * S
    ncol = min(16, n2)
    a_view = u3[BLK_FOURIER].reshape(B, n1, n2 * GROUP_W)
    y = pl.pallas_call(
        _fourier_stage1_kernel,
        grid=(B, n2 // ncol),
        in_specs=[
            pl.BlockSpec((None, n1, ncol * GROUP_W), lambda b, j: (b, 0, j)),
            pl.BlockSpec((2 * n1, n1), lambda b, j: (0, 0)),
        ],
        out_specs=pl.BlockSpec((None, 2, n1, ncol * GROUP_W), lambda b, j: (b, 0, 0, j)),
        out_shape=jax.ShapeDtypeStruct((B, 2, n1, n2 * GROUP_W), BF16),
        compiler_params=_params(("parallel", "parallel")),
        name="fourier_stage1",
    )(a_view, f1)
    y5 = y.reshape(B, 2, n1, n2, GROUP_W)
    n_p = 8
    out = pl.pallas_call(
        functools.partial(_fourier_stage2_kernel, scale=1.0 / math.sqrt(S * HEAD_DIM)),
        grid=(B, n1 // n_p),
        in_specs=[
            pl.BlockSpec((None, 2, n_p, n2, GROUP_W), lambda b, j: (b, 0, j, 0, 0)),
            pl.BlockSpec((n_p, 2 * n2, 2 * n2), lambda b, j: (j, 0, 0)),
            pl.BlockSpec((2 * HEAD_DIM, HEAD_DIM), lambda b, j: (0, 0)),
        ],
        out_specs=pl.BlockSpec((None, n2, n_p * GROUP_W), lambda b, j: (b, 0, j)),
        out_shape=jax.ShapeDtypeStruct((B, n2, n1 * GROUP_W), BF16),
        compiler_params=_params(("parallel", "parallel")),
        name="fourier_stage2",
    )(y5, g, cs)
    return out.reshape(T, GROUP_W)


def _log_sigmoid(x):
    return jnp.minimum(x, 0.0) - jnp.log1p(jnp.exp(-jnp.abs(x)))


def _mlstm_chunk(d, q, k, v, gi, gf, ib, fb, tri, mask, c_ref, n_ref, m_ref, out_ref, rows):
    li = gi + ib
    lf = _log_sigmoid(gf + fb)
    b = jnp.dot(tri, lf, precision=lax.Precision.HIGHEST, preferred_element_type=F32)
    g = jnp.sum(lf, axis=0, keepdims=True)
    w_end = g - b + li
    m_loc = jnp.max(w_end, axis=0, keepdims=True)
    e_t = jnp.exp(w_end - m_loc).T
    a_t = (li - b).T
    m_old = m_ref[d:d + 1, :]
    m_new = jnp.maximum(g + m_old, m_loc)
    a_sc = jnp.exp(g + m_old - m_new)
    b_sc = jnp.exp(m_loc - m_new)
    lane = lax.broadcasted_iota(jnp.int32, (1, HEAD_DIM), 1)

    def pick(x, hp):
        return jnp.sum(jnp.where(lane == hp, x, 0.0), axis=-1, keepdims=True)

    for h in range(MLSTM_HEADS):
        hp = MLSTM_HEADS * d + h
        lanes = slice(h * HEAD_DIM, (h + 1) * HEAD_DIM)
        qh = q[:, lanes]
        vh = v[:, lanes]
        ks = k[:, lanes].astype(F32) * QK_SCALE
        bc = pick(b, hp)
        dm = jnp.where(mask, bc + a_t[hp:hp + 1, :], -jnp.inf)
        inter_log = bc + pick(m_old, hp)
        m_t = jnp.maximum(inter_log, jnp.max(dm, axis=-1, keepdims=True))
        p = jnp.exp(dm - m_t)
        inter_w = jnp.exp(inter_log - m_t)
        qk = lax.dot_general(qh, ks.astype(BF16), (((1,), (1,)), ((), ())), preferred_element_type=F32)
        a_mat = p * qk
        c_prev = c_ref[hp]
        n_prev = n_ref[hp:hp + 1, :]
        num = (jnp.dot(a_mat.astype(BF16), vh, preferred_element_type=F32)
               + inter_w * jnp.dot(qh, c_prev.astype(BF16), preferred_element_type=F32))
        den = (jnp.sum(a_mat, axis=-1, keepdims=True)
               + inter_w * jnp.sum(qh.astype(F32) * n_prev, axis=-1, keepdims=True))
        hout = num / jnp.maximum(jnp.abs(den), jnp.exp(-m_t))
        out_ref[rows, lanes] = hout.astype(out_ref.dtype)
        ke_t = ks.T * e_t[hp:hp + 1, :]
        c_loc = jnp.dot(ke_t.astype(BF16), vh, preferred_element_type=F32)
        n_loc = jnp.dot(e_t[:8, :], ks, precision=lax.Precision.HIGHEST,
                        preferred_element_type=F32)[hp:hp + 1, :]
        ah = pick(a_sc, hp)
        bh = pick(b_sc, hp)
        c_ref[hp] = ah * c_prev + bh * c_loc
        n_ref[hp:hp + 1, :] = ah * n_prev + bh * n_loc
    m_ref[d:d + 1, :] = m_new


def _mlstm_kernel(qf_ref, kf_ref, vf_ref, gf_ref, qb_ref, kb_ref, vb_ref, gb_ref, ib_ref, fb_ref,
                  hf_ref, hb_ref, c_ref, n_ref, m_ref):
    L = MLSTM_CHUNK
    n_sub = qf_ref.shape[0] // L

    @pl.when(pl.program_id(1) == 0)
    def _():
        c_ref[...] = jnp.zeros_like(c_ref)
        n_ref[...] = jnp.zeros_like(n_ref)
        m_ref[...] = jnp.zeros_like(m_ref)

    row = lax.broadcasted_iota(jnp.int32, (L, L), 0)
    col = lax.broadcasted_iota(jnp.int32, (L, L), 1)
    mask_f = col <= row
    mask_b = col >= row
    tri_f = jnp.where(mask_f, 1.0, 0.0).astype(F32)
    tri_b = jnp.where(mask_b, 1.0, 0.0).astype(F32)
    ib = ib_ref[...]
    fb = fb_ref[...]

    def body(j, carry):
        rf = pl.ds(pl.multiple_of(j * L, L), L)
        rb = pl.ds(pl.multiple_of((n_sub - 1 - j) * L, L), L)
        _mlstm_chunk(0, qf_ref[rf, :], kf_ref[rf, :], vf_ref[rf, :], gf_ref[rf, :HEAD_DIM], gf_ref[rf, HEAD_DIM:],
                     ib, fb, tri_f, mask_f, c_ref, n_ref, m_ref, hf_ref, rf)
        _mlstm_chunk(1, qb_ref[rb, :], kb_ref[rb, :], vb_ref[rb, :], gb_ref[rb, :HEAD_DIM], gb_ref[rb, HEAD_DIM:],
                     ib, fb, tri_b, mask_b, c_ref, n_ref, m_ref, hb_ref, rb)
        return carry

    lax.fori_loop(0, n_sub, body, 0)


def _mlstm(u3, gates, ib, fb, B, S, n_sub=4):
    T = B * S
    rows = n_sub * MLSTM_CHUNK
    nblk = S // rows
    u4 = u3.reshape(N_COLBLK, B, S, GROUP_W)
    g3 = gates.reshape(B, S, GATE_W)

    def fwd(blk):
        return pl.BlockSpec((None, None, rows, GROUP_W), lambda b, c: (blk, b, c, 0))

    def bwd(blk):
        return pl.BlockSpec((None, None, rows, GROUP_W), lambda b, c: (blk, b, nblk - 1 - c, 0))

    hf, hb = pl.pallas_call(
        _mlstm_kernel,
        grid=(B, nblk),
        in_specs=[
            fwd(BLK_MQ), fwd(BLK_MK), fwd(BLK_MV),
            pl.BlockSpec((None, rows, GATE_W), lambda b, c: (b, c, 0)),
            bwd(BLK_MQ), bwd(BLK_MK), bwd(BLK_MV),
            pl.BlockSpec((None, rows, GATE_W), lambda b, c: (b, nblk - 1 - c, 0)),
            pl.BlockSpec((1, HEAD_DIM), lambda b, c: (0, 0)),
            pl.BlockSpec((1, HEAD_DIM), lambda b, c: (0, 0)),
        ],
        out_specs=[
            pl.BlockSpec((None, rows, GROUP_W), lambda b, c: (b, c, 0)),
            pl.BlockSpec((None, rows, GROUP_W), lambda b, c: (b, nblk - 1 - c, 0)),
        ],
        out_shape=[jax.ShapeDtypeStruct((B, S, GROUP_W), BF16)] * 2,
        scratch_shapes=[
            pltpu.VMEM((2 * MLSTM_HEADS, HEAD_DIM, HEAD_DIM), F32),
            pltpu.VMEM((2 * MLSTM_HEADS, HEAD_DIM), F32),
            pltpu.VMEM((8, HEAD_DIM), F32),
        ],
        compiler_params=_params(("parallel", "arbitrary")),
        name="mlstm",
    )(u4, u4, u4, g3, u4, u4, u4, g3, ib, fb)
    return hf.reshape(T, GROUP_W), hb.reshape(T, GROUP_W)


def _rope_tables(S):
    rows = S // GRID_W
    row = jnp.repeat(jnp.arange(rows), GRID_W).astype(F32)
    colp = jnp.tile(jnp.arange(GRID_W), rows).astype(F32)
    inv = ROPE_THETA ** (-jnp.arange(ROPE_FREQS, dtype=F32) / ROPE_FREQS)
    ar = row[:, None] * inv
    ac = colp[:, None] * inv
    ang = jnp.concatenate([ar, ar, ac, ac], axis=1)
    first_half = (jnp.arange(HEAD_DIM) % (2 * ROPE_FREQS)) < ROPE_FREQS
    return jnp.cos(ang), jnp.where(first_half, -1.0, 1.0) * jnp.sin(ang)


def _rope(x, cos, sin_signed, first_half):
    partner = jnp.where(first_half, pltpu.roll(x, HEAD_DIM - ROPE_FREQS, axis=1), pltpu.roll(x, ROPE_FREQS, axis=1))
    return x * cos + partner * sin_signed


def _attn_prep_kernel(aq_ref, akv_ref, cos_ref, sin_ref, qg_ref, kg_ref, q_ref, kt_ref, va_ref, kn_ref, qn_ref):
    cos = cos_ref[...]
    sin = sin_ref[...]
    lane = lax.broadcasted_iota(jnp.int32, cos.shape, 1)
    first_half = (lane % (2 * ROPE_FREQS)) < ROPE_FREQS
    qg = qg_ref[...]
    kg = kg_ref[...]
    for h in range(4):
        lanes = slice(h * HEAD_DIM, (h + 1) * HEAD_DIM)
        qh = _head_norm(aq_ref[:, lanes].astype(F32)) * qg
        qb = (_rope(qh, cos, sin, first_half) * (QK_SCALE * LOG2_E)).astype(BF16)
        q_ref[:, lanes] = qb
        q2 = jnp.max(jnp.sum(jnp.square(qb.astype(F32)), axis=-1, keepdims=True), axis=0, keepdims=True)
        qn_ref[h] = jnp.broadcast_to(q2, qn_ref.shape[1:])
    for h in range(ATTN_KV_HEADS):
        lanes = slice(h * HEAD_DIM, (h + 1) * HEAD_DIM)
        kh = _head_norm(akv_ref[:, lanes].astype(F32)) * kg
        kb = _rope(kh, cos, sin, first_half).astype(BF16)
        kt_ref[h] = kb.astype(F32).T.astype(BF16)
        k2 = jnp.max(jnp.sum(jnp.square(kb.astype(F32)), axis=-1, keepdims=True), axis=0, keepdims=True)
        kn_ref[h] = jnp.broadcast_to(k2, kn_ref.shape[1:])
        vl = slice((ATTN_KV_HEADS + h) * HEAD_DIM, (ATTN_KV_HEADS + h + 1) * HEAD_DIM)
        va_ref[h, :, :HEAD_DIM] = akv_ref[:, vl]
        va_ref[h, :, HEAD_DIM:] = jnp.ones((akv_ref.shape[0], HEAD_DIM), BF16)


def _flash_kernel(q_ref, kt_ref, va_ref, kn_ref, qn_ref, o_ref, m_ref, acc_ref, p_ref, mp_ref):
    tq = q_ref.shape[0]
    nkc, _, tk = kt_ref.shape
    n_rep = tk // HEAD_DIM
    q2 = jnp.concatenate([q_ref[:, :HEAD_DIM], q_ref[:, HEAD_DIM:]], axis=0)

    def chunk(c):
        return kt_ref[c], va_ref[pl.ds(pl.multiple_of(c * tk, tk), tk), :]

    def lane_max(s):
        part = s[:, :HEAD_DIM]
        for t in range(1, n_rep):
            part = jnp.maximum(part, s[:, t * HEAD_DIM:(t + 1) * HEAD_DIM])
        return part

    q_sq = jnp.max(jnp.max(qn_ref[...], axis=0), axis=0, keepdims=True)
    k_sq = jnp.max(jnp.max(kn_ref[...], axis=0), axis=0, keepdims=True)
    safe = jnp.max(jnp.sqrt(q_sq * k_sq)) <= FRAME_MARGIN
    acc_ref[...] = jnp.zeros_like(acc_ref)

    def fast_scores(c, slot):
        kt, _ = chunk(c)
        s = jnp.dot(q2, kt, preferred_element_type=F32)
        p_ref[slot] = jnp.exp2(s - jnp.tile(m_ref[...], (1, n_rep))).astype(BF16)
        mp_ref[slot] = lane_max(s)

    def fast_accumulate(c, slot):
        _, vc = chunk(c)
        m_prev = m_ref[...]
        m_new = jnp.maximum(m_prev, jnp.max(mp_ref[slot], axis=-1, keepdims=True))
        alpha = jnp.exp2(m_prev - m_new)
        acc_ref[...] = (acc_ref[...] + jnp.dot(p_ref[slot], vc, preferred_element_type=F32)) * jnp.tile(alpha, (1, 2))
        m_ref[...] = m_new

    def fast():
        m_ref[...] = jnp.zeros_like(m_ref)

        group = 8 if nkc % 8 == 0 else 2

        def body(i, carry):
            for t in range(group):
                fast_scores(group * i + t, t % 2)
                fast_accumulate(group * i + t, t % 2)
            return carry

        lax.fori_loop(0, nkc // group, body, 0)

    def exact():
        m_ref[...] = jnp.full_like(m_ref, -jnp.inf)

        def body(c, carry):
            kt, vc = chunk(c)
            s = jnp.dot(q2, kt, preferred_element_type=F32)
            m_prev = m_ref[...]
            m_new = jnp.maximum(m_prev, jnp.max(lane_max(s), axis=-1, keepdims=True))
            p = jnp.exp2(s - jnp.tile(m_new, (1, n_rep))).astype(BF16)
            acc_ref[...] = (jnp.tile(jnp.exp2(m_prev - m_new), (1, 2)) * acc_ref[...]
                            + jnp.dot(p, vc, preferred_element_type=F32))
            m_ref[...] = m_new
            return carry

        lax.fori_loop(0, nkc, body, 0)

    lax.cond(safe, fast, exact)
    o = acc_ref[:, :HEAD_DIM] / acc_ref[:, HEAD_DIM:]
    o_ref[:, :HEAD_DIM] = o[:tq].astype(o_ref.dtype)
    o_ref[:, HEAD_DIM:] = o[tq:].astype(o_ref.dtype)


def _axial_gqa(u3, q_gain, k_gain, rope, B, S, tq=512, tk=1024):
    T = B * S
    cos, sin = rope
    u4 = u3.reshape(N_COLBLK, B, S, GROUP_W)
    tk = min(tk, S // 2)
    nkc = S // tk
    assert nkc >= 2 and nkc % 2 == 0, "flash kernel pipelines kv chunks in pairs"
    q_r, kt, va, kn, qn = pl.pallas_call(
        _attn_prep_kernel,
        grid=(B, nkc),
        in_specs=[
            pl.BlockSpec((None, None, tk, GROUP_W), lambda b, i: (BLK_AQ, b, i, 0)),
            pl.BlockSpec((None, None, tk, GROUP_W), lambda b, i: (BLK_AKV, b, i, 0)),
            pl.BlockSpec((tk, HEAD_DIM), lambda b, i: (i, 0)),
            pl.BlockSpec((tk, HEAD_DIM), lambda b, i: (i, 0)),
            pl.BlockSpec((1, HEAD_DIM), lambda b, i: (0, 0)),
            pl.BlockSpec((1, HEAD_DIM), lambda b, i: (0, 0)),
        ],
        out_specs=[
            pl.BlockSpec((None, tk, GROUP_W), lambda b, i: (b, i, 0)),
            pl.BlockSpec((None, ATTN_KV_HEADS, None, HEAD_DIM, tk), lambda b, i: (b, 0, i, 0, 0)),
            pl.BlockSpec((None, ATTN_KV_HEADS, tk, 2 * HEAD_DIM), lambda b, i: (b, 0, i, 0)),
            pl.BlockSpec((None, ATTN_KV_HEADS, None, 8, HEAD_DIM), lambda b, i: (b, 0, i, 0, 0)),
            pl.BlockSpec((None, 4, None, 8, HEAD_DIM), lambda b, i: (b, 0, i, 0, 0)),
        ],
        out_shape=[
            jax.ShapeDtypeStruct((B, S, GROUP_W), BF16),
            jax.ShapeDtypeStruct((B, ATTN_KV_HEADS, nkc, HEAD_DIM, tk), BF16),
            jax.ShapeDtypeStruct((B, ATTN_KV_HEADS, S, 2 * HEAD_DIM), BF16),
            jax.ShapeDtypeStruct((B, ATTN_KV_HEADS, nkc, 8, HEAD_DIM), F32),
            jax.ShapeDtypeStruct((B, 4, nkc, 8, HEAD_DIM), F32),
        ],
        compiler_params=_params(("parallel", "parallel")),
        name="attn_prep",
    )(u4, u4, cos, sin, q_gain, k_gain)
    out = pl.pallas_call(
        _flash_kernel,
        grid=(B, ATTN_KV_HEADS, S // tq),
        in_specs=[
            pl.BlockSpec((None, tq, 2 * HEAD_DIM), lambda b, k, i: (b, i, k)),
            pl.BlockSpec((None, None, nkc, HEAD_DIM, tk), lambda b, k, i: (b, k, 0, 0, 0)),
            pl.BlockSpec((None, None, S, 2 * HEAD_DIM), lambda b, k, i: (b, k, 0, 0)),
            pl.BlockSpec((None, None, nkc, 8, HEAD_DIM), lambda b, k, i: (b, k, 0, 0, 0)),
            pl.BlockSpec((None, 2, None, 8, HEAD_DIM), lambda b, k, i: (b, k, (i * tq) // tk, 0, 0)),
        ],
        out_specs=pl.BlockSpec((None, tq, 2 * HEAD_DIM), lambda b, k, i: (b, i, k)),
        out_shape=jax.ShapeDtypeStruct((B, S, GROUP_W), BF16),
        scratch_shapes=[
            pltpu.VMEM((2 * tq, HEAD_DIM), F32),
            pltpu.VMEM((2 * tq, 2 * HEAD_DIM), F32),
            pltpu.VMEM((2, 2 * tq, tk), BF16),
            pltpu.VMEM((2, 2 * tq, HEAD_DIM), F32),
        ],
        compiler_params=_params(("parallel", "parallel", "arbitrary")),
        name="flash_gqa",
    )(q_r, kt, va, kn, qn)
    return out.reshape(T, GROUP_W)


def _combine_kernel(x_ref, ya_ref, cb_ref, cc_ref, cx_ref, ccp_ref, cxp_ref, ccn_ref, cxn_ref, mo_ref,
                    hf_ref, hb_ref, yd_ref, cw_ref, cbias_ref, gain_ref, w_ref, o_ref, y_ref, *, tiles_per_seq):
    tm = x_ref.shape[0]
    i = pl.program_id(0)
    halo = ccp_ref.shape[0]
    at_start = (i % tiles_per_seq) == 0
    at_end = (i % tiles_per_seq) == tiles_per_seq - 1
    gain = gain_ref[...]

    z = cc_ref[...].astype(F32) * cx_ref[...].astype(F32)
    z_before = ccp_ref[halo - 1:halo, :].astype(F32) * cxp_ref[halo - 1:halo, :].astype(F32)
    z_after = ccn_ref[0:1, :].astype(F32) * cxn_ref[0:1, :].astype(F32)
    z_before = jnp.where(at_start, 0.0, z_before)
    z_after = jnp.where(at_end, 0.0, z_after)
    rid = lax.broadcasted_iota(jnp.int32, z.shape, 0)
    z_m1 = jnp.where(rid == 0, z_before, pltpu.roll(z, 1, axis=0))
    z_p1 = jnp.where(rid == tm - 1, z_after, pltpu.roll(z, tm - 1, axis=0))
    conv = z_m1 * cw_ref[0:1, :] + z * cw_ref[1:2, :] + z_p1 * cw_ref[2:3, :] + cbias_ref[...]
    yb = cb_ref[...].astype(F32) * conv

    for grp in range(4):
        lanes = slice(grp * HEAD_DIM, (grp + 1) * HEAD_DIM)
        y_ref[:, lanes] = (ya_ref[:, lanes].astype(F32) * gain[:, lanes]).astype(BF16)
        gl = slice(GROUP_W + grp * HEAD_DIM, GROUP_W + (grp + 1) * HEAD_DIM)
        y_ref[:, gl] = (_head_norm(yb[:, lanes]) * gain[:, gl]).astype(BF16)
        gl = slice(2 * GROUP_W + grp * HEAD_DIM, 2 * GROUP_W + (grp + 1) * HEAD_DIM)
        yc = _head_norm(hf_ref[:, lanes].astype(F32) + hb_ref[:, lanes].astype(F32))
        y_ref[:, gl] = (jax.nn.sigmoid(mo_ref[:, lanes].astype(F32)) * yc * gain[:, gl]).astype(BF16)
        gl = slice(3 * GROUP_W + grp * HEAD_DIM, 3 * GROUP_W + (grp + 1) * HEAD_DIM)
        y_ref[:, gl] = (_head_norm(yd_ref[:, lanes].astype(F32)) * gain[:, gl]).astype(BF16)

    o_ref[...] = x_ref[...] + jnp.dot(y_ref[...], w_ref[...], preferred_element_type=F32)


def _combine_out(x2, ya, u3, hf, hb, yd, conv_w, conv_b, out_gain, w_out, S, tm=512, halo=16):
    T = x2.shape[0]
    hb_per_tile = tm // halo
    n_halo = T // halo

    def u_blk(blk):
        return pl.BlockSpec((None, tm, GROUP_W), lambda i: (blk, i, 0))

    def u_prev(blk):
        return pl.BlockSpec((None, halo, GROUP_W), lambda i: (blk, jnp.maximum(i * hb_per_tile - 1, 0), 0))

    def u_next(blk):
        return pl.BlockSpec((None, halo, GROUP_W), lambda i: (blk, jnp.minimum((i + 1) * hb_per_tile, n_halo - 1), 0))

    tok = pl.BlockSpec((tm, GROUP_W), lambda i: (i, 0))
    return pl.pallas_call(
        functools.partial(_combine_kernel, tiles_per_seq=S // tm),
        grid=(T // tm,),
        in_specs=[
            pl.BlockSpec((tm, D_MODEL), lambda i: (i, 0)),
            tok,
            u_blk(BLK_CB), u_blk(BLK_CC), u_blk(BLK_CX),
            u_prev(BLK_CC), u_prev(BLK_CX), u_next(BLK_CC), u_next(BLK_CX),
            u_blk(BLK_MO),
            tok, tok, tok,
            pl.BlockSpec((8, GROUP_W), lambda i: (0, 0)),
            pl.BlockSpec((1, GROUP_W), lambda i: (0, 0)),
            pl.BlockSpec((1, D_MODEL), lambda i: (0, 0)),
            pl.BlockSpec((D_MODEL, D_MODEL), lambda i: (0, 0)),
        ],
        out_specs=pl.BlockSpec((tm, D_MODEL), lambda i: (i, 0)),
        out_shape=jax.ShapeDtypeStruct((T, D_MODEL), F32),
        scratch_shapes=[pltpu.VMEM((tm, D_MODEL), BF16)],
        compiler_params=_params(("parallel",)),
        name="combine_out",
    )(x2, ya, u3, u3, u3, u3, u3, u3, u3, u3, hf, hb, yd, conv_w, conv_b, out_gain, w_out)


def _mem_kv_kernel(mem_ref, g_ref, w_ref, kg_ref, kt_ref, v_ref, h_ref):
    _norm_rows_to(mem_ref, g_ref, h_ref)
    kv = jnp.dot(h_ref[...], w_ref[...], preferred_element_type=F32)
    kg = kg_ref[...]
    for h in range(CA_HEADS):
        lanes = slice(h * HEAD_DIM, (h + 1) * HEAD_DIM)
        kt_ref[h] = (_head_norm(kv[:, lanes]) * kg).T.astype(BF16)
    v_ref[...] = kv[:, CA_HEADS * HEAD_DIM:].astype(BF16)


def _mem_kv(mem, g_mem, w_kv, k_gain):
    B = mem.shape[0]
    return pl.pallas_call(
        _mem_kv_kernel,
        grid=(B,),
        in_specs=[
            pl.BlockSpec((None, N_MEM, D_MODEL), lambda b: (b, 0, 0)),
            pl.BlockSpec((1, D_MODEL), lambda b: (0, 0)),
            pl.BlockSpec((D_MODEL, 2 * GROUP_W), lambda b: (0, 0)),
            pl.BlockSpec((1, HEAD_DIM), lambda b: (0, 0)),
        ],
        out_specs=[
            pl.BlockSpec((None, CA_HEADS, HEAD_DIM, N_MEM), lambda b: (b, 0, 0, 0)),
            pl.BlockSpec((None, N_MEM, GROUP_W), lambda b: (b, 0, 0)),
        ],
        out_shape=[
            jax.ShapeDtypeStruct((B, CA_HEADS, HEAD_DIM, N_MEM), BF16),
            jax.ShapeDtypeStruct((B, N_MEM, GROUP_W), BF16),
        ],
        scratch_shapes=[pltpu.VMEM((N_MEM, D_MODEL), BF16)],
        compiler_params=_params(("parallel",)),
        name="mem_kv",
    )(mem, g_mem, w_kv, k_gain)


def _cross_attn_kernel(x_ref, g_ref, wq_ref, qg_ref, kt_ref, v_ref, wo_ref, o_ref, h_ref, oc_ref):
    _norm_rows_to(x_ref, g_ref, h_ref)
    q = jnp.dot(h_ref[...], wq_ref[...], preferred_element_type=F32)
    qg = qg_ref[...]
    for h in range(CA_HEADS):
        lanes = slice(h * HEAD_DIM, (h + 1) * HEAD_DIM)
        qh = (_head_norm(q[:, lanes]) * (qg * QK_SCALE)).astype(BF16)
        s = jnp.dot(qh, kt_ref[h], preferred_element_type=F32)
        e = jnp.exp(s - jnp.max(s, axis=-1, keepdims=True))
        o = jnp.dot(e.astype(BF16), v_ref[:, lanes], preferred_element_type=F32)
        oc_ref[:, lanes] = (o * pl.reciprocal(jnp.sum(e, axis=-1, keepdims=True), approx=False)).astype(BF16)
    o_ref[...] = x_ref[...] + jnp.dot(oc_ref[...], wo_ref[...], preferred_element_type=F32)


def _cross_attn(x2, g_ca, w_q, q_gain, kt, v, w_o, B, S, tm=1024):
    T = x2.shape[0]
    x3 = x2.reshape(B, S, D_MODEL)
    out = pl.pallas_call(
        _cross_attn_kernel,
        grid=(B, S // tm),
        in_specs=[
            pl.BlockSpec((None, tm, D_MODEL), lambda b, i: (b, i, 0)),
            pl.BlockSpec((1, D_MODEL), lambda b, i: (0, 0)),
            pl.BlockSpec((D_MODEL, GROUP_W), lambda b, i: (0, 0)),
            pl.BlockSpec((1, HEAD_DIM), lambda b, i: (0, 0)),
            pl.BlockSpec((None, CA_HEADS, HEAD_DIM, N_MEM), lambda b, i: (b, 0, 0, 0)),
            pl.BlockSpec((None, N_MEM, GROUP_W), lambda b, i: (b, 0, 0)),
            pl.BlockSpec((GROUP_W, D_MODEL), lambda b, i: (0, 0)),
        ],
        out_specs=pl.BlockSpec((None, tm, D_MODEL), lambda b, i: (b, i, 0)),
        out_shape=jax.ShapeDtypeStruct((B, S, D_MODEL), F32),
        scratch_shapes=[pltpu.VMEM((tm, D_MODEL), BF16), pltpu.VMEM((tm, GROUP_W), BF16)],
        compiler_params=_params(("parallel", "parallel")),
        name="cross_attn",
    )(x3, g_ca, w_q, q_gain, kt, v, w_o)
    return out.reshape(T, D_MODEL)


def _mlp_kernel(x_ref, g_ref, wu_ref, wd_ref, o_ref, h_ref):
    @pl.when(pl.program_id(1) == 0)
    def _():
        _norm_rows_to(x_ref, g_ref, h_ref)
        o_ref[...] = x_ref[...]

    up = jnp.dot(h_ref[...], wu_ref[...], preferred_element_type=F32)
    act = jnp.square(jnp.maximum(up, 0.0)).astype(BF16)
    o_ref[...] += jnp.dot(act, wd_ref[...], preferred_element_type=F32)


def _mlp(x2, g, w_up, w_down, tm=512, tf=1024):
    T = x2.shape[0]
    return pl.pallas_call(
        _mlp_kernel,
        grid=(T // tm, D_FF // tf),
        in_specs=[
            pl.BlockSpec((tm, D_MODEL), lambda i, j: (i, 0)),
            pl.BlockSpec((1, D_MODEL), lambda i, j: (0, 0)),
            pl.BlockSpec((D_MODEL, tf), lambda i, j: (0, j)),
            pl.BlockSpec((tf, D_MODEL), lambda i, j: (j, 0)),
        ],
        out_specs=pl.BlockSpec((tm, D_MODEL), lambda i, j: (i, 0)),
        out_shape=jax.ShapeDtypeStruct((T, D_MODEL), F32),
        scratch_shapes=[pltpu.VMEM((tm, D_MODEL), BF16)],
        compiler_params=_params(("parallel", "arbitrary")),
        name="mlp",
    )(x2, g, w_up, w_down)


def _prep_weights(w_in, conv_w, i_bias, f_bias, w_out, w_ca_q, w_ca_kv, w_ca_o, w_up, w_down):
    n_main = 8 * GROUP_W
    w_main = jnp.concatenate([w_in[:, :, :n_main], w_in[:, :, n_main + 16:]], axis=2).astype(BF16)
    pad = jnp.zeros((DEPTH, D_MODEL, HEAD_DIM - 8), F32)
    w_gate = jnp.concatenate([w_in[:, :, n_main:n_main + 8], pad, w_in[:, :, n_main + 8:n_main + 16], pad],
                             axis=2).astype(BF16)
    lane_pad = jnp.zeros((DEPTH, 1, HEAD_DIM - 8), F32)
    ib = jnp.concatenate([i_bias.reshape(DEPTH, 1, 8), lane_pad], axis=2)
    fb = jnp.concatenate([f_bias.reshape(DEPTH, 1, 8), lane_pad], axis=2)
    cw = jnp.concatenate([conv_w, jnp.zeros((DEPTH, 5, GROUP_W), F32)], axis=1)
    return dict(w_main=w_main, w_gate=w_gate, ib=ib, fb=fb, cw=cw, w_out=w_out.astype(BF16),
                w_ca_q=w_ca_q.astype(BF16), w_ca_kv=w_ca_kv.astype(BF16), w_ca_o=w_ca_o.astype(BF16),
                w_up=w_up.astype(BF16), w_down=w_down.astype(BF16))


def _trunk(x, mem, pw, g_mix, conv_b, attn_q_norm, attn_k_norm, out_gain, g_ca, g_mem, ca_q_norm, ca_k_norm, g_mlp):
    B, S, _ = x.shape
    T = B * S
    x2 = x.reshape(T, D_MODEL)
    rope = _rope_tables(S)
    ftab = _fourier_tables(S)
    for l in range(DEPTH):
        u3, gates = _norm_proj(x2, g_mix[l][None], pw["w_main"][l], pw["w_gate"][l])
        ya = _fourier(u3, B, S, ftab)
        hf, hb = _mlstm(u3, gates, pw["ib"][l], pw["fb"][l], B, S)
        yd = _axial_gqa(u3, attn_q_norm[l][None], attn_k_norm[l][None], rope, B, S)
        x2 = _combine_out(x2, ya, u3, hf, hb, yd, pw["cw"][l], conv_b[l][None], out_gain[l][None], pw["w_out"][l], S)
        kt, v = _mem_kv(mem, g_mem[l][None], pw["w_ca_kv"][l], ca_k_norm[l][None])
        x2 = _cross_attn(x2, g_ca[l][None], pw["w_ca_q"][l], ca_q_norm[l][None], kt, v, pw["w_ca_o"][l], B, S)
        x2 = _mlp(x2, g_mlp[l][None], pw["w_up"][l], pw["w_down"][l])
    return x2.reshape(B, S, D_MODEL)


def kernel(x_prompt, x_sample, mem_prompt, mem_sample, g_mix, w_in, conv_w, conv_b, i_bias, f_bias, attn_q_norm,
           attn_k_norm, out_gain, w_out, g_ca, g_mem, w_ca_q, w_ca_kv, ca_q_norm, ca_k_norm, w_ca_o, g_mlp, w_up,
           w_down):
    pw = _prep_weights(w_in, conv_w, i_bias, f_bias, w_out, w_ca_q, w_ca_kv, w_ca_o, w_up, w_down)
    args = (pw, g_mix, conv_b, attn_q_norm, attn_k_norm, out_gain, g_ca, g_mem, ca_q_norm, ca_k_norm, g_mlp)
    y_prompt = _trunk(x_prompt, mem_prompt, *args)
    y_sample = _trunk(x_sample, mem_sample, *args)
    return (y_prompt, y_sample)
```

```python
import functools
import math

import jax
import jax.numpy as jnp
from jax import lax
from jax.experimental import pallas as pl
from jax.experimental.pallas import tpu as pltpu

F32 = jnp.float32
BF16 = jnp.bfloat16

D_MODEL = 2048
DEPTH = 4
HEAD_DIM = 128
GROUP_W = 4 * HEAD_DIM
N_COLBLK = 10
GATE_W = 2 * HEAD_DIM
MLSTM_HEADS = 4
MLSTM_CHUNK = 128
ATTN_KV_HEADS = 2
GRID_W = 64
ROPE_THETA = 10000.0
ROPE_FREQS = HEAD_DIM // 4
CA_HEADS = 4
N_MEM = 256
D_FF = 4 * D_MODEL
RMS_EPS = 1e-6
QK_SCALE = HEAD_DIM ** -0.5
LOG2_E = math.log2(math.e)
FRAME_MARGIN = 64.0

BLK_FOURIER, BLK_CB, BLK_CC, BLK_CX, BLK_MQ, BLK_MK, BLK_MV, BLK_MO, BLK_AQ, BLK_AKV = range(N_COLBLK)

V7X_VMEM_LIMIT = 56 * 1024 * 1024
NORM_ROWS = 256


def _params(semantics, vmem=V7X_VMEM_LIMIT):
    return pltpu.CompilerParams(dimension_semantics=semantics, vmem_limit_bytes=vmem)


def _head_norm(y):
    return y * lax.rsqrt(jnp.mean(y * y, axis=-1, keepdims=True) + RMS_EPS)


def _norm_rows_to(x_ref, g_ref, h_ref):
    tm = x_ref.shape[0]
    g = g_ref[...]

    def body(r, carry):
        rows = pl.ds(pl.multiple_of(r * NORM_ROWS, NORM_ROWS), NORM_ROWS)
        xf = x_ref[rows, :]
        ms = jnp.mean(xf * xf, axis=-1, keepdims=True)
        h_ref[rows, :] = (xf * lax.rsqrt(ms + RMS_EPS) * g).astype(BF16)
        return carry

    lax.fori_loop(0, tm // NORM_ROWS, body, 0)


def _norm_proj_kernel(x_ref, g_ref, w_ref, wg_ref, u_ref, gate_ref, h_ref):
    @pl.when(pl.program_id(1) == 0)
    def _():
        _norm_rows_to(x_ref, g_ref, h_ref)
        gate_ref[...] = jnp.dot(h_ref[...], wg_ref[...], preferred_element_type=F32)

    res = jnp.dot(h_ref[...], w_ref[...], preferred_element_type=F32)
    for k in range(u_ref.shape[0]):
        u_ref[k] = res[:, k * GROUP_W:(k + 1) * GROUP_W].astype(BF16)


def _norm_proj(x2, g, w_main, w_gate, tm=1024, nb=2):
    T = x2.shape[0]
    tn = nb * GROUP_W
    return pl.pallas_call(
        _norm_proj_kernel,
        grid=(T // tm, N_COLBLK // nb),
        in_specs=[
            pl.BlockSpec((tm, D_MODEL), lambda i, j: (i, 0)),
            pl.BlockSpec((1, D_MODEL), lambda i, j: (0, 0)),
            pl.BlockSpec((D_MODEL, tn), lambda i, j: (0, j)),
            pl.BlockSpec((D_MODEL, GATE_W), lambda i, j: (0, 0)),
        ],
        out_specs=[
            pl.BlockSpec((nb, tm, GROUP_W), lambda i, j: (j, i, 0)),
            pl.BlockSpec((tm, GATE_W), lambda i, j: (i, 0)),
        ],
        out_shape=[
            jax.ShapeDtypeStruct((N_COLBLK, T, GROUP_W), BF16),
            jax.ShapeDtypeStruct((T, GATE_W), F32),
        ],
        scratch_shapes=[pltpu.VMEM((tm, D_MODEL), BF16)],
        compiler_params=_params(("parallel", "arbitrary")),
        name="norm_proj",
    )(x2, g, w_main, w_gate)


def _fourier_tables(S):
    n2 = HEAD_DIM
    n1 = S // n2
    i1 = jnp.arange(n1, dtype=jnp.int32)
    a1 = (2.0 * math.pi / n1) * ((i1[:, None] * i1[None, :]) % n1).astype(F32)
    f1 = jnp.concatenate([jnp.cos(a1), -jnp.sin(a1)], axis=0).astype(BF16)
    i2 = jnp.arange(n2, dtype=jnp.int32)
    p = i1[:, None, None] + n1 * i2[None, :, None]
    k = (p * i2[None, None, :]) % S
    a2 = (2.0 * math.pi / S) * k.astype(F32)
    c2, s2 = jnp.cos(a2), jnp.sin(a2)
    g = jnp.concatenate([jnp.concatenate([c2, s2], axis=2),
                         jnp.concatenate([-s2, c2], axis=2)], axis=1).astype(BF16)
    ac = (2.0 * math.pi / HEAD_DIM) * ((i2[:, None] * i2[None, :]) % HEAD_DIM).astype(F32)
    cs = jnp.concatenate([jnp.cos(ac), jnp.sin(ac)], axis=0).astype(BF16)
    return f1, g, cs


def _fourier_stage1_kernel(a_ref, f1_ref, y_ref):
    n1 = a_ref.shape[0]
    res = jnp.dot(f1_ref[...], a_ref[...], preferred_element_type=F32)
    y_ref[0] = res[:n1].astype(BF16)
    y_ref[1] = res[n1:].astype(BF16)


def _fourier_stage2_kernel(y_ref, g_ref, cs_ref, o_ref, *, scale):
    n_p = y_ref.shape[1]
    cs = cs_ref[...]
    for p in range(n_p):
        ycat = jnp.concatenate([y_ref[0, p], y_ref[1, p]], axis=0)
        x = jnp.dot(g_ref[p], ycat, preferred_element_type=F32).astype(BF16)
        for grp in range(4):
            lanes = slice(grp * HEAD_DIM, (grp + 1) * HEAD_DIM)
            xg = jnp.concatenate([x[:HEAD_DIM, lanes], x[HEAD_DIM:, lanes]], axis=1)
            o = jnp.dot(xg, cs, preferred_element_type=F32) * scale
            col = p * GROUP_W + grp * HEAD_DIM
            o_ref[:, col:col + HEAD_DIM] = _head_norm(o).astype(BF16)


def _fourier(u3, B, S, tables):
    f1, g, cs = tables
    n2 = HEAD_DIM
    n1 = S // n2
    T = B * S
    ncol = min(16, n2)
    a_view = u3[BLK_FOURIER].reshape(B, n1, n2 * GROUP_W)
    y = pl.pallas_call(
        _fourier_stage1_kernel,
        grid=(B, n2 // ncol),
        in_specs=[
            pl.BlockSpec((None, n1, ncol * GROUP_W), lambda b, j: (b, 0, j)),
            pl.BlockSpec((2 * n1, n1), lambda b, j: (0, 0)),
        ],
        out_specs=pl.BlockSpec((None, 2, n1, ncol * GROUP_W), lambda b, j: (b, 0, 0, j)),
        out_shape=jax.ShapeDtypeStruct((B, 2, n1, n2 * GROUP_W), BF16),
        compiler_params=_params(("parallel", "parallel")),
        name="fourier_stage1",
    )(a_view, f1)
    y5 = y.reshape(B, 2, n1, n2, GROUP_W)
    n_p = 8
    out = pl.pallas_call(
        functools.partial(_fourier_stage2_kernel, scale=1.0 / math.sqrt(S * HEAD_DIM)),
        grid=(B, n1 // n_p),
        in_specs=[
            pl.BlockSpec((None, 2, n_p, n2, GROUP_W), lambda b, j: (b, 0, j, 0, 0)),
            pl.BlockSpec((n_p, 2 * n2, 2 * n2), lambda b, j: (j, 0, 0)),
            pl.BlockSpec((2 * HEAD_DIM, HEAD_DIM), lambda b, j: (0, 0)),
        ],
        out_specs=pl.BlockSpec((None, n2, n_p * GROUP_W), lambda b, j: (b, 0, j)),
        out_shape=jax.ShapeDtypeStruct((B, n2, n1 * GROUP_W), BF16),
        compiler_params=_params(("parallel", "parallel")),
        name="fourier_stage2",
    )(y5, g, cs)
    return out.reshape(T, GROUP_W)


def _log_sigmoid(x):
    return jnp.minimum(x, 0.0) - jnp.log1p(jnp.exp(-jnp.abs(x)))


def _split3(x):
    hi = x.astype(BF16)
    r1 = x - hi.astype(F32)
    mid = r1.astype(BF16)
    return hi, mid, (r1 - mid.astype(F32)).astype(BF16)


def _mlstm_chunk(d, q, k, v, gi, gf, ib, fb, tri, mask, c_ref, n_ref, m_ref, out_ref, rows):
    li = gi + ib
    lf = _log_sigmoid(gf + fb)
    b = sum(jnp.dot(tri, part, preferred_element_type=F32) for part in _split3(lf))
    g = jnp.sum(lf, axis=0, keepdims=True)
    w_end = g - b + li
    m_loc = jnp.max(w_end, axis=0, keepdims=True)
    e_t = jnp.exp(w_end - m_loc).T * QK_SCALE
    e_parts = _split3(e_t[:16, :])
    a_t = (li - b).T
    m_old = m_ref[d:d + 1, :]
    m_new = jnp.maximum(g + m_old, m_loc)
    a_sc = jnp.exp(g + m_old - m_new)
    b_sc = jnp.exp(m_loc - m_new)
    lane = lax.broadcasted_iota(jnp.int32, (1, HEAD_DIM), 1)

    def pick(x, hp):
        return jnp.sum(jnp.where(lane == hp, x, 0.0), axis=-1, keepdims=True)

    for h in range(MLSTM_HEADS):
        hp = MLSTM_HEADS * d + h
        lanes = slice(h * HEAD_DIM, (h + 1) * HEAD_DIM)
        qh = q[:, lanes]
        vh = v[:, lanes]
        kh = k[:, lanes]
        bc = pick(b, hp)
        dm = jnp.where(mask, bc + a_t[hp:hp + 1, :], -jnp.inf)
        inter_log = bc + pick(m_old, hp)
        m_t = jnp.maximum(inter_log, jnp.max(dm, axis=-1, keepdims=True))
        p = jnp.exp(dm - m_t)
        inter_w = jnp.exp(inter_log - m_t)
        qk = lax.dot_general(qh, kh, (((1,), (1,)), ((), ())), preferred_element_type=F32)
        a_mat = p * (qk * QK_SCALE)
        c_prev = c_ref[hp]
        n_prev = n_ref[hp:hp + 1, :]
        num = (jnp.dot(a_mat.astype(BF16), vh, preferred_element_type=F32)
               + inter_w * jnp.dot(qh, c_prev.astype(BF16), preferred_element_type=F32))
        den = (jnp.sum(a_mat, axis=-1, keepdims=True)
               + inter_w * jnp.sum(qh.astype(F32) * n_prev, axis=-1, keepdims=True))
        hout = num / jnp.maximum(jnp.abs(den), jnp.exp(-m_t))
        out_ref[rows, lanes] = hout.astype(out_ref.dtype)
        ke_t = kh.astype(F32).T * e_t[hp:hp + 1, :]
        c_loc = jnp.dot(ke_t.astype(BF16), vh, preferred_element_type=F32)
        n_loc = sum(jnp.dot(part, kh, preferred_element_type=F32) for part in e_parts)[hp:hp + 1, :]
        ah = pick(a_sc, hp)
        bh = pick(b_sc, hp)
        c_ref[hp] = ah * c_prev + bh * c_loc
        n_ref[hp:hp + 1, :] = ah * n_prev + bh * n_loc
    m_ref[d:d + 1, :] = m_new


def _mlstm_kernel(qf_ref, kf_ref, vf_ref, gf_ref, qb_ref, kb_ref, vb_ref, gb_ref, ib_ref, fb_ref,
                  hf_ref, hb_ref, c_ref, n_ref, m_ref):
    L = MLSTM_CHUNK
    n_sub = qf_ref.shape[0] // L

    @pl.when(pl.program_id(1) == 0)
    def _():
        c_ref[...] = jnp.zeros_like(c_ref)
        n_ref[...] = jnp.zeros_like(n_ref)
        m_ref[...] = jnp.zeros_like(m_ref)

    row = lax.broadcasted_iota(jnp.int32, (L, L), 0)
    col = lax.broadcasted_iota(jnp.int32, (L, L), 1)
    mask_f = col <= row
    mask_b = col >= row
    tri_f = jnp.where(mask_f, 1.0, 0.0).astype(BF16)
    tri_b = jnp.where(mask_b, 1.0, 0.0).astype(BF16)
    ib = ib_ref[...]
    fb = fb_ref[...]

    def body(j, carry):
        rf = pl.ds(pl.multiple_of(j * L, L), L)
        rb = pl.ds(pl.multiple_of((n_sub - 1 - j) * L, L), L)
        _mlstm_chunk(0, qf_ref[rf, :], kf_ref[rf, :], vf_ref[rf, :], gf_ref[rf, :HEAD_DIM], gf_ref[rf, HEAD_DIM:],
                     ib, fb, tri_f, mask_f, c_ref, n_ref, m_ref, hf_ref, rf)
        _mlstm_chunk(1, qb_ref[rb, :], kb_ref[rb, :], vb_ref[rb, :], gb_ref[rb, :HEAD_DIM], gb_ref[rb, HEAD_DIM:],
                     ib, fb, tri_b, mask_b, c_ref, n_ref, m_ref, hb_ref, rb)
        return carry

    lax.fori_loop(0, n_sub, body, 0)


def _mlstm(u3, gates, ib, fb, B, S, n_sub=4):
    T = B * S
    rows = n_sub * MLSTM_CHUNK
    nblk = S // rows
    u4 = u3.reshape(N_COLBLK, B, S, GROUP_W)
    g3 = gates.reshape(B, S, GATE_W)

    def fwd(blk):
        return pl.BlockSpec((None, None, rows, GROUP_W), lambda b, c: (blk, b, c, 0))

    def bwd(blk):
        return pl.BlockSpec((None, None, rows, GROUP_W), lambda b, c: (blk, b, nblk - 1 - c, 0))

    hf, hb = pl.pallas_call(
        _mlstm_kernel,
        grid=(B, nblk),
        in_specs=[
            fwd(BLK_MQ), fwd(BLK_MK), fwd(BLK_MV),
            pl.BlockSpec((None, rows, GATE_W), lambda b, c: (b, c, 0)),
            bwd(BLK_MQ), bwd(BLK_MK), bwd(BLK_MV),
            pl.BlockSpec((None, rows, GATE_W), lambda b, c: (b, nblk - 1 - c, 0)),
            pl.BlockSpec((1, HEAD_DIM), lambda b, c: (0, 0)),
            pl.BlockSpec((1, HEAD_DIM), lambda b, c: (0, 0)),
        ],
        out_specs=[
            pl.BlockSpec((None, rows, GROUP_W), lambda b, c: (b, c, 0)),
            pl.BlockSpec((None, rows, GROUP_W), lambda b, c: (b, nblk - 1 - c, 0)),
        ],
        out_shape=[jax.ShapeDtypeStruct((B, S, GROUP_W), BF16)] * 2,
        scratch_shapes=[
            pltpu.VMEM((2 * MLSTM_HEADS, HEAD_DIM, HEAD_DIM), F32),
            pltpu.VMEM((2 * MLSTM_HEADS, HEAD_DIM), F32),
            pltpu.VMEM((8, HEAD_DIM), F32),
        ],
        compiler_params=_params(("parallel", "arbitrary")),
        name="mlstm",
    )(u4, u4, u4, g3, u4, u4, u4, g3, ib, fb)
    return hf.reshape(T, GROUP_W), hb.reshape(T, GROUP_W)


def _rope_tables(S):
    rows = S // GRID_W
    row = jnp.repeat(jnp.arange(rows), GRID_W).astype(F32)
    colp = jnp.tile(jnp.arange(GRID_W), rows).astype(F32)
    inv = ROPE_THETA ** (-jnp.arange(ROPE_FREQS, dtype=F32) / ROPE_FREQS)
    ar = row[:, None] * inv
    ac = colp[:, None] * inv
    ang = jnp.concatenate([ar, ar, ac, ac], axis=1)
    first_half = (jnp.arange(HEAD_DIM) % (2 * ROPE_FREQS)) < ROPE_FREQS
    return jnp.cos(ang), jnp.where(first_half, -1.0, 1.0) * jnp.sin(ang)


def _rope(x, cos, sin_signed, first_half):
    partner = jnp.where(first_half, pltpu.roll(x, HEAD_DIM - ROPE_FREQS, axis=1), pltpu.roll(x, ROPE_FREQS, axis=1))
    return x * cos + partner * sin_signed


def _attn_prep_kernel(aq_ref, akv_ref, cos_ref, sin_ref, qg_ref, kg_ref, q_ref, kt_ref, va_ref, kn_ref, qn_ref):
    cos = cos_ref[...]
    sin = sin_ref[...]
    lane = lax.broadcasted_iota(jnp.int32, cos.shape, 1)
    first_half = (lane % (2 * ROPE_FREQS)) < ROPE_FREQS
    qg = qg_ref[...]
    kg = kg_ref[...]
    for h in range(4):
        lanes = slice(h * HEAD_DIM, (h + 1) * HEAD_DIM)
        qh = _head_norm(aq_ref[:, lanes].astype(F32)) * qg
        qb = (_rope(qh, cos, sin, first_half) * (QK_SCALE * LOG2_E)).astype(BF16)
        q_ref[:, lanes] = qb
        q2 = jnp.max(jnp.sum(jnp.square(qb.astype(F32)), axis=-1, keepdims=True), axis=0, keepdims=True)
        qn_ref[h] = jnp.broadcast_to(q2, qn_ref.shape[1:])
    for h in range(ATTN_KV_HEADS):
        lanes = slice(h * HEAD_DIM, (h + 1) * HEAD_DIM)
        kh = _head_norm(akv_ref[:, lanes].astype(F32)) * kg
        kb = _rope(kh, cos, sin, first_half).astype(BF16)
        kt_ref[h] = kb.astype(F32).T.astype(BF16)
        k2 = jnp.max(jnp.sum(jnp.square(kb.astype(F32)), axis=-1, keepdims=True), axis=0, keepdims=True)
        kn_ref[h] = jnp.broadcast_to(k2, kn_ref.shape[1:])
        vl = slice((ATTN_KV_HEADS + h) * HEAD_DIM, (ATTN_KV_HEADS + h + 1) * HEAD_DIM)
        va_ref[h, :, :HEAD_DIM] = akv_ref[:, vl]
        va_ref[h, :, HEAD_DIM:] = jnp.ones((akv_ref.shape[0], HEAD_DIM), BF16)


def _flash_kernel(q_ref, kt_ref, va_ref, kn_ref, qn_ref, o_ref, m_ref, acc_ref, p_ref, mp_ref):
    tq = q_ref.shape[0]
    nkc, _, tk = kt_ref.shape
    n_rep = tk // HEAD_DIM
    q2 = jnp.concatenate([q_ref[:, :HEAD_DIM], q_ref[:, HEAD_DIM:]], axis=0)

    def chunk(c):
        return kt_ref[c], va_ref[pl.ds(pl.multiple_of(c * tk, tk), tk), :]

    def lane_max(s):
        part = s[:, :HEAD_DIM]
        for t in range(1, n_rep):
            part = jnp.maximum(part, s[:, t * HEAD_DIM:(t + 1) * HEAD_DIM])
        return part

    q_sq = jnp.max(jnp.max(qn_ref[...], axis=0), axis=0, keepdims=True)
    k_sq = jnp.max(jnp.max(kn_ref[...], axis=0), axis=0, keepdims=True)
    safe = jnp.max(jnp.sqrt(q_sq * k_sq)) <= FRAME_MARGIN
    acc_ref[...] = jnp.zeros_like(acc_ref)

    def fast_scores(c, slot):
        kt, _ = chunk(c)
        s = jnp.dot(q2, kt, preferred_element_type=F32)
        p_ref[slot] = jnp.exp2(s - jnp.tile(m_ref[...], (1, n_rep))).astype(BF16)
        mp_ref[slot] = lane_max(s)

    def fast_accumulate(c, slot):
        _, vc = chunk(c)
        m_prev = m_ref[...]
        m_new = jnp.maximum(m_prev, jnp.max(mp_ref[slot], axis=-1, keepdims=True))
        alpha = jnp.exp2(m_prev - m_new)
        acc_ref[...] = (acc_ref[...] + jnp.dot(p_ref[slot], vc, preferred_element_type=F32)) * jnp.tile(alpha, (1, 2))
        m_ref[...] = m_new

    def fast():
        m_ref[...] = jnp.zeros_like(m_ref)

        group = 8 if nkc % 8 == 0 else 2

        def body(i, carry):
            for t in range(group):
                fast_scores(group * i + t, t % 2)
                fast_accumulate(group * i + t, t % 2)
            return carry

        lax.fori_loop(0, nkc // group, body, 0)

    def exact():
        m_ref[...] = jnp.full_like(m_ref, -jnp.inf)

        def body(c, carry):
            kt, vc = chunk(c)
            s = jnp.dot(q2, kt, preferred_element_type=F32)
            m_prev = m_ref[...]
            m_new = jnp.maximum(m_prev, jnp.max(lane_max(s), axis=-1, keepdims=True))
            p = jnp.exp2(s - jnp.tile(m_new, (1, n_rep))).astype(BF16)
            acc_ref[...] = (jnp.tile(jnp.exp2(m_prev - m_new), (1, 2)) * acc_ref[...]
                            + jnp.dot(p, vc, preferred_element_type=F32))
            m_ref[...] = m_new
            return carry

        lax.fori_loop(0, nkc, body, 0)

    lax.cond(safe, fast, exact)
    o = acc_ref[:, :HEAD_DIM] / acc_ref[:, HEAD_DIM:]
    o_ref[:, :HEAD_DIM] = o[:tq].astype(o_ref.dtype)
    o_ref[:, HEAD_DIM:] = o[tq:].astype(o_ref.dtype)


def _axial_gqa(u3, q_gain, k_gain, rope, B, S, tq=512, tk=1024):
    T = B * S
    cos, sin = rope
    u4 = u3.reshape(N_COLBLK, B, S, GROUP_W)
    tk = min(tk, S // 2)
    nkc = S // tk
    assert nkc >= 2 and nkc % 2 == 0, "flash kernel pipelines kv chunks in pairs"
    q_r, kt, va, kn, qn = pl.pallas_call(
        _attn_prep_kernel,
        grid=(B, nkc),
        in_specs=[
            pl.BlockSpec((None, None, tk, GROUP_W), lambda b, i: (BLK_AQ, b, i, 0)),
            pl.BlockSpec((None, None, tk, GROUP_W), lambda b, i: (BLK_AKV, b, i, 0)),
            pl.BlockSpec((tk, HEAD_DIM), lambda b, i: (i, 0)),
            pl.BlockSpec((tk, HEAD_DIM), lambda b, i: (i, 0)),
            pl.BlockSpec((1, HEAD_DIM), lambda b, i: (0, 0)),
            pl.BlockSpec((1, HEAD_DIM), lambda b, i: (0, 0)),
        ],
        out_specs=[
            pl.BlockSpec((None, tk, GROUP_W), lambda b, i: (b, i, 0)),
            pl.BlockSpec((None, ATTN_KV_HEADS, None, HEAD_DIM, tk), lambda b, i: (b, 0, i, 0, 0)),
            pl.BlockSpec((None, ATTN_KV_HEADS, tk, 2 * HEAD_DIM), lambda b, i: (b, 0, i, 0)),
            pl.BlockSpec((None, ATTN_KV_HEADS, None, 8, HEAD_DIM), lambda b, i: (b, 0, i, 0, 0)),
            pl.BlockSpec((None, 4, None, 8, HEAD_DIM), lambda b, i: (b, 0, i, 0, 0)),
        ],
        out_shape=[
            jax.ShapeDtypeStruct((B, S, GROUP_W), BF16),
            jax.ShapeDtypeStruct((B, ATTN_KV_HEADS, nkc, HEAD_DIM, tk), BF16),
            jax.ShapeDtypeStruct((B, ATTN_KV_HEADS, S, 2 * HEAD_DIM), BF16),
            jax.ShapeDtypeStruct((B, ATTN_KV_HEADS, nkc, 8, HEAD_DIM), F32),
            jax.ShapeDtypeStruct((B, 4, nkc, 8, HEAD_DIM), F32),
        ],
        compiler_params=_params(("parallel", "parallel")),
        name="attn_prep",
    )(u4, u4, cos, sin, q_gain, k_gain)
    out = pl.pallas_call(
        _flash_kernel,
        grid=(B, ATTN_KV_HEADS, S // tq),
        in_specs=[
            pl.BlockSpec((None, tq, 2 * HEAD_DIM), lambda b, k, i: (b, i, k)),
            pl.BlockSpec((None, None, nkc, HEAD_DIM, tk), lambda b, k, i: (b, k, 0, 0, 0)),
            pl.BlockSpec((None, None, S, 2 * HEAD_DIM), lambda b, k, i: (b, k, 0, 0)),
            pl.BlockSpec((None, None, nkc, 8, HEAD_DIM), lambda b, k, i: (b, k, 0, 0, 0)),
            pl.BlockSpec((None, 2, None, 8, HEAD_DIM), lambda b, k, i: (b, k, (i * tq) // tk, 0, 0)),
        ],
        out_specs=pl.BlockSpec((None, tq, 2 * HEAD_DIM), lambda b, k, i: (b, i, k)),
        out_shape=jax.ShapeDtypeStruct((B, S, GROUP_W), BF16),
        scratch_shapes=[
            pltpu.VMEM((2 * tq, HEAD_DIM), F32),
            pltpu.VMEM((2 * tq, 2 * HEAD_DIM), F32),
            pltpu.VMEM((2, 2 * tq, tk), BF16),
            pltpu.VMEM((2, 2 * tq, HEAD_DIM), F32),
        ],
        compiler_params=_params(("parallel", "parallel", "arbitrary")),
        name="flash_gqa",
    )(q_r, kt, va, kn, qn)
    return out.reshape(T, GROUP_W)


def _combine_kernel(x_ref, ya_ref, cb_ref, cc_ref, cx_ref, ccp_ref, cxp_ref, ccn_ref, cxn_ref, mo_ref,
                    hf_ref, hb_ref, yd_ref, cw_ref, cbias_ref, gain_ref, w_ref, o_ref, y_ref, *, tiles_per_seq):
    tm = x_ref.shape[0]
    i = pl.program_id(0)
    halo = ccp_ref.shape[0]
    at_start = (i % tiles_per_seq) == 0
    at_end = (i % tiles_per_seq) == tiles_per_seq - 1
    gain = gain_ref[...]

    z = cc_ref[...].astype(F32) * cx_ref[...].astype(F32)
    z_before = ccp_ref[halo - 1:halo, :].astype(F32) * cxp_ref[halo - 1:halo, :].astype(F32)
    z_after = ccn_ref[0:1, :].astype(F32) * cxn_ref[0:1, :].astype(F32)
    z_before = jnp.where(at_start, 0.0, z_before)
    z_after = jnp.where(at_end, 0.0, z_after)
    rid = lax.broadcasted_iota(jnp.int32, z.shape, 0)
    z_m1 = jnp.where(rid == 0, z_before, pltpu.roll(z, 1, axis=0))
    z_p1 = jnp.where(rid == tm - 1, z_after, pltpu.roll(z, tm - 1, axis=0))
    conv = z_m1 * cw_ref[0:1, :] + z * cw_ref[1:2, :] + z_p1 * cw_ref[2:3, :] + cbias_ref[...]
    yb = cb_ref[...].astype(F32) * conv

    for grp in range(4):
        lanes = slice(grp * HEAD_DIM, (grp + 1) * HEAD_DIM)
        y_ref[:, lanes] = (ya_ref[:, lanes].astype(F32) * gain[:, lanes]).astype(BF16)
        gl = slice(GROUP_W + grp * HEAD_DIM, GROUP_W + (grp + 1) * HEAD_DIM)
        y_ref[:, gl] = (_head_norm(yb[:, lanes]) * gain[:, gl]).astype(BF16)
        gl = slice(2 * GROUP_W + grp * HEAD_DIM, 2 * GROUP_W + (grp + 1) * HEAD_DIM)
        yc = _head_norm(hf_ref[:, lanes].astype(F32) + hb_ref[:, lanes].astype(F32))
        y_ref[:, gl] = (jax.nn.sigmoid(mo_ref[:, lanes].astype(F32)) * yc * gain[:, gl]).astype(BF16)
        gl = slice(3 * GROUP_W + grp * HEAD_DIM, 3 * GROUP_W + (grp + 1) * HEAD_DIM)
        y_ref[:, gl] = (_head_norm(yd_ref[:, lanes].astype(F32)) * gain[:, gl]).astype(BF16)

    o_ref[...] = x_ref[...] + jnp.dot(y_ref[...], w_ref[...], preferred_element_type=F32)


def _combine_out(x2, ya, u3, hf, hb, yd, conv_w, conv_b, out_gain, w_out, S, tm=512, halo=16):
    T = x2.shape[0]
    hb_per_tile = tm // halo
    n_halo = T // halo

    def u_blk(blk):
        return pl.BlockSpec((None, tm, GROUP_W), lambda i: (blk, i, 0))

    def u_prev(blk):
        return pl.BlockSpec((None, halo, GROUP_W), lambda i: (blk, jnp.maximum(i * hb_per_tile - 1, 0), 0))

    def u_next(blk):
        return pl.BlockSpec((None, halo, GROUP_W), lambda i: (blk, jnp.minimum((i + 1) * hb_per_tile, n_halo - 1), 0))

    tok = pl.BlockSpec((tm, GROUP_W), lambda i: (i, 0))
    return pl.pallas_call(
        functools.partial(_combine_kernel, tiles_per_seq=S // tm),
        grid=(T // tm,),
        in_specs=[
            pl.BlockSpec((tm, D_MODEL), lambda i: (i, 0)),
            tok,
            u_blk(BLK_CB), u_blk(BLK_CC), u_blk(BLK_CX),
            u_prev(BLK_CC), u_prev(BLK_CX), u_next(BLK_CC), u_next(BLK_CX),
            u_blk(BLK_MO),
            tok, tok, tok,
            pl.BlockSpec((8, GROUP_W), lambda i: (0, 0)),
            pl.BlockSpec((1, GROUP_W), lambda i: (0, 0)),
            pl.BlockSpec((1, D_MODEL), lambda i: (0, 0)),
            pl.BlockSpec((D_MODEL, D_MODEL), lambda i: (0, 0)),
        ],
        out_specs=pl.BlockSpec((tm, D_MODEL), lambda i: (i, 0)),
        out_shape=jax.ShapeDtypeStruct((T, D_MODEL), F32),
        scratch_shapes=[pltpu.VMEM((tm, D_MODEL), BF16)],
        compiler_params=_params(("parallel",)),
        name="combine_out",
    )(x2, ya, u3, u3, u3, u3, u3, u3, u3, u3, hf, hb, yd, conv_w, conv_b, out_gain, w_out)


def _mem_kv_kernel(mem_ref, g_ref, w_ref, kg_ref, kt_ref, v_ref, h_ref):
    _norm_rows_to(mem_ref, g_ref, h_ref)
    kv = jnp.dot(h_ref[...], w_ref[...], preferred_element_type=F32)
    kg = kg_ref[...]
    for h in range(CA_HEADS):
        lanes = slice(h * HEAD_DIM, (h + 1) * HEAD_DIM)
        kt_ref[h] = (_head_norm(kv[:, lanes]) * kg).T.astype(BF16)
    v_ref[...] = kv[:, CA_HEADS * HEAD_DIM:].astype(BF16)


def _mem_kv(mem, g_mem, w_kv, k_gain):
    B = mem.shape[0]
    return pl.pallas_call(
        _mem_kv_kernel,
        grid=(B,),
        in_specs=[
            pl.BlockSpec((None, N_MEM, D_MODEL), lambda b: (b, 0, 0)),
            pl.BlockSpec((1, D_MODEL), lambda b: (0, 0)),
            pl.BlockSpec((D_MODEL, 2 * GROUP_W), lambda b: (0, 0)),
            pl.BlockSpec((1, HEAD_DIM), lambda b: (0, 0)),
        ],
        out_specs=[
            pl.BlockSpec((None, CA_HEADS, HEAD_DIM, N_MEM), lambda b: (b, 0, 0, 0)),
            pl.BlockSpec((None, N_MEM, GROUP_W), lambda b: (b, 0, 0)),
        ],
        out_shape=[
            jax.ShapeDtypeStruct((B, CA_HEADS, HEAD_DIM, N_MEM), BF16),
            jax.ShapeDtypeStruct((B, N_MEM, GROUP_W), BF16),
        ],
        scratch_shapes=[pltpu.VMEM((N_MEM, D_MODEL), BF16)],
        compiler_params=_params(("parallel",)),
        name="mem_kv",
    )(mem, g_mem, w_kv, k_gain)


def _cross_attn_kernel(x_ref, g_ref, wq_ref, qg_ref, kt_ref, v_ref, wo_ref, o_ref, h_ref, oc_ref):
    _norm_rows_to(x_ref, g_ref, h_ref)
    q = jnp.dot(h_ref[...], wq_ref[...], preferred_element_type=F32)
    qg = qg_ref[...]
    for h in range(CA_HEADS):
        lanes = slice(h * HEAD_DIM, (h + 1) * HEAD_DIM)
        qh = (_head_norm(q[:, lanes]) * (qg * QK_SCALE)).astype(BF16)
        s = jnp.dot(qh, kt_ref[h], preferred_element_type=F32)
        e = jnp.exp(s - jnp.max(s, axis=-1, keepdims=True))
        o = jnp.dot(e.astype(BF16), v_ref[:, lanes], preferred_element_type=F32)
        oc_ref[:, lanes] = (o * pl.reciprocal(jnp.sum(e, axis=-1, keepdims=True), approx=False)).astype(BF16)
    o_ref[...] = x_ref[...] + jnp.dot(oc_ref[...], wo_ref[...], preferred_element_type=F32)


def _cross_attn(x2, g_ca, w_q, q_gain, kt, v, w_o, B, S, tm=1024):
    T = x2.shape[0]
    x3 = x2.reshape(B, S, D_MODEL)
    out = pl.pallas_call(
        _cross_attn_kernel,
        grid=(B, S // tm),
        in_specs=[
            pl.BlockSpec((None, tm, D_MODEL), lambda b, i: (b, i, 0)),
            pl.BlockSpec((1, D_MODEL), lambda b, i: (0, 0)),
            pl.BlockSpec((D_MODEL, GROUP_W), lambda b, i: (0, 0)),
            pl.BlockSpec((1, HEAD_DIM), lambda b, i: (0, 0)),
            pl.BlockSpec((None, CA_HEADS, HEAD_DIM, N_MEM), lambda b, i: (b, 0, 0, 0)),
            pl.BlockSpec((None, N_MEM, GROUP_W), lambda b, i: (b, 0, 0)),
            pl.BlockSpec((GROUP_W, D_MODEL), lambda b, i: (0, 0)),
        ],
        out_specs=pl.BlockSpec((None, tm, D_MODEL), lambda b, i: (b, i, 0)),
        out_shape=jax.ShapeDtypeStruct((B, S, D_MODEL), F32),
        scratch_shapes=[pltpu.VMEM((tm, D_MODEL), BF16), pltpu.VMEM((tm, GROUP_W), BF16)],
        compiler_params=_params(("parallel", "parallel")),
        name="cross_attn",
    )(x3, g_ca, w_q, q_gain, kt, v, w_o)
    return out.reshape(T, D_MODEL)


def _mlp_kernel(x_ref, g_ref, wu_ref, wd_ref, o_ref, h_ref):
    @pl.when(pl.program_id(1) == 0)
    def _():
        _norm_rows_to(x_ref, g_ref, h_ref)
        o_ref[...] = x_ref[...]

    up = jnp.dot(h_ref[...], wu_ref[...], preferred_element_type=F32)
    act = jnp.square(jnp.maximum(up, 0.0)).astype(BF16)
    o_ref[...] += jnp.dot(act, wd_ref[...], preferred_element_type=F32)


def _mlp(x2, g, w_up, w_down, tm=512, tf=1024):
    T = x2.shape[0]
    return pl.pallas_call(
        _mlp_kernel,
        grid=(T // tm, D_FF // tf),
        in_specs=[
            pl.BlockSpec((tm, D_MODEL), lambda i, j: (i, 0)),
            pl.BlockSpec((1, D_MODEL), lambda i, j: (0, 0)),
            pl.BlockSpec((D_MODEL, tf), lambda i, j: (0, j)),
            pl.BlockSpec((tf, D_MODEL), lambda i, j: (j, 0)),
        ],
        out_specs=pl.BlockSpec((tm, D_MODEL), lambda i, j: (i, 0)),
        out_shape=jax.ShapeDtypeStruct((T, D_MODEL), F32),
        scratch_shapes=[pltpu.VMEM((tm, D_MODEL), BF16)],
        compiler_params=_params(("parallel", "arbitrary")),
        name="mlp",
    )(x2, g, w_up, w_down)


def _prep_weights(w_in, conv_w, i_bias, f_bias, w_out, w_ca_q, w_ca_kv, w_ca_o, w_up, w_down):
    n_main = 8 * GROUP_W
    w_main = jnp.concatenate([w_in[:, :, :n_main], w_in[:, :, n_main + 16:]], axis=2).astype(BF16)
    pad = jnp.zeros((DEPTH, D_MODEL, HEAD_DIM - 8), F32)
    w_gate = jnp.concatenate([w_in[:, :, n_main:n_main + 8], pad, w_in[:, :, n_main + 8:n_main + 16], pad],
                             axis=2).astype(BF16)
    lane_pad = jnp.zeros((DEPTH, 1, HEAD_DIM - 8), F32)
    ib = jnp.concatenate([i_bias.reshape(DEPTH, 1, 8), lane_pad], axis=2)
    fb = jnp.concatenate([f_bias.reshape(DEPTH, 1, 8), lane_pad], axis=2)
    cw = jnp.concatenate([conv_w, jnp.zeros((DEPTH, 5, GROUP_W), F32)], axis=1)
    return dict(w_main=w_main, w_gate=w_gate, ib=ib, fb=fb, cw=cw, w_out=w_out.astype(BF16),
                w_ca_q=w_ca_q.astype(BF16), w_ca_kv=w_ca_kv.astype(BF16), w_ca_o=w_ca_o.astype(BF16),
                w_up=w_up.astype(BF16), w_down=w_down.astype(BF16))


def _trunk(x, mem, pw, g_mix, conv_b, attn_q_norm, attn_k_norm, out_gain, g_ca, g_mem, ca_q_norm, ca_k_norm, g_mlp):
    B, S, _ = x.shape
    T = B * S
    x2 = x.reshape(T, D_MODEL)
    rope = _rope_tables(S)
    ftab = _fourier_tables(S)
    for l in range(DEPTH):
        u3, gates = _norm_proj(x2, g_mix[l][None], pw["w_main"][l], pw["w_gate"][l])
        ya = _fourier(u3, B, S, ftab)
        hf, hb = _mlstm(u3, gates, pw["ib"][l], pw["fb"][l], B, S)
        yd = _axial_gqa(u3, attn_q_norm[l][None], attn_k_norm[l][None], rope, B, S)
        x2 = _combine_out(x2, ya, u3, hf, hb, yd, pw["cw"][l], conv_b[l][None], out_gain[l][None], pw["w_out"][l], S)
        kt, v = _mem_kv(mem, g_mem[l][None], pw["w_ca_kv"][l], ca_k_norm[l][None])
        x2 = _cross_attn(x2, g_ca[l][None], pw["w_ca_q"][l], ca_q_norm[l][None], kt, v, pw["w_ca_o"][l], B, S)
        x2 = _mlp(x2, g_mlp[l][None], pw["w_up"][l], pw["w_down"][l])
    return x2.reshape(B, S, D_MODEL)


def kernel(x_prompt, x_sample, mem_prompt, mem_sample, g_mix, w_in, conv_w, conv_b, i_bias, f_bias, attn_q_norm,
           attn_k_norm, out_gain, w_out, g_ca, g_mem, w_ca_q, w_ca_kv, ca_q_norm, ca_k_norm, w_ca_o, g_mlp, w_up,
           w_down):
    pw = _prep_weights(w_in, conv_w, i_bias, f_bias, w_out, w_ca_q, w_ca_kv, w_ca_o, w_up, w_down)
    args = (pw, g_mix, conv_b, attn_q_norm, attn_k_norm, out_gain, g_ca, g_mem, ca_q_norm, ca_k_norm, g_mlp)
    y_prompt = _trunk(x_prompt, mem_prompt, *args)
    y_sample = _trunk(x_sample, mem_sample, *args)
    return (y_prompt, y_sample)
```

```python
import functools
import math

import jax
import jax.numpy as jnp
from jax import lax
from jax.experimental import pallas as pl
from jax.experimental.pallas import tpu as pltpu

F32 = jnp.float32
BF16 = jnp.bfloat16

D_MODEL = 2048
DEPTH = 4
HEAD_DIM = 128
GROUP_W = 4 * HEAD_DIM
N_COLBLK = 10
GATE_W = 2 * HEAD_DIM
MLSTM_HEADS = 4
MLSTM_CHUNK = 128
ATTN_KV_HEADS = 2
GRID_W = 64
ROPE_THETA = 10000.0
ROPE_FREQS = HEAD_DIM // 4
CA_HEADS = 4
N_MEM = 256
D_FF = 4 * D_MODEL
RMS_EPS = 1e-6
QK_SCALE = HEAD_DIM ** -0.5
LOG2_E = math.log2(math.e)
FRAME_MARGIN = 64.0

BLK_FOURIER, BLK_CB, BLK_CC, BLK_CX, BLK_MQ, BLK_MK, BLK_MV, BLK_MO, BLK_AQ, BLK_AKV = range(N_COLBLK)

V7X_VMEM_LIMIT = 56 * 1024 * 1024
NORM_ROWS = 256


def _params(semantics, vmem=V7X_VMEM_LIMIT):
    return pltpu.CompilerParams(dimension_semantics=semantics, vmem_limit_bytes=vmem)


def _head_norm(y):
    return y * lax.rsqrt(jnp.mean(y * y, axis=-1, keepdims=True) + RMS_EPS)


def _norm_rows_to(x_ref, g_ref, h_ref):
    tm = x_ref.shape[0]
    g = g_ref[...]

    def body(r, carry):
        rows = pl.ds(pl.multiple_of(r * NORM_ROWS, NORM_ROWS), NORM_ROWS)
        xf = x_ref[rows, :]
        ms = jnp.mean(xf * xf, axis=-1, keepdims=True)
        h_ref[rows, :] = (xf * lax.rsqrt(ms + RMS_EPS) * g).astype(BF16)
        return carry

    lax.fori_loop(0, tm // NORM_ROWS, body, 0)


def _norm_proj_kernel(x_ref, g_ref, w_ref, wg_ref, u_ref, gate_ref, h_ref):
    @pl.when(pl.program_id(1) == 0)
    def _():
        _norm_rows_to(x_ref, g_ref, h_ref)
        gate_ref[...] = jnp.dot(h_ref[...], wg_ref[...], preferred_element_type=F32)

    res = jnp.dot(h_ref[...], w_ref[...], preferred_element_type=F32)
    for k in range(u_ref.shape[0]):
        u_ref[k] = res[:, k * GROUP_W:(k + 1) * GROUP_W].astype(BF16)


def _norm_proj(x2, g, w_main, w_gate, tm=1024, nb=2):
    T = x2.shape[0]
    tn = nb * GROUP_W
    return pl.pallas_call(
        _norm_proj_kernel,
        grid=(T // tm, N_COLBLK // nb),
        in_specs=[
            pl.BlockSpec((tm, D_MODEL), lambda i, j: (i, 0)),
            pl.BlockSpec((1, D_MODEL), lambda i, j: (0, 0)),
            pl.BlockSpec((D_MODEL, tn), lambda i, j: (0, j)),
            pl.BlockSpec((D_MODEL, GATE_W), lambda i, j: (0, 0)),
        ],
        out_specs=[
            pl.BlockSpec((nb, tm, GROUP_W), lambda i, j: (j, i, 0)),
            pl.BlockSpec((tm, GATE_W), lambda i, j: (i, 0)),
        ],
        out_shape=[
            jax.ShapeDtypeStruct((N_COLBLK, T, GROUP_W), BF16),
            jax.ShapeDtypeStruct((T, GATE_W), F32),
        ],
        scratch_shapes=[pltpu.VMEM((tm, D_MODEL), BF16)],
        compiler_params=_params(("parallel", "arbitrary")),
        name="norm_proj",
    )(x2, g, w_main, w_gate)


def _fourier_tables(S):
    n2 = HEAD_DIM
    n1 = S // n2
    i1 = jnp.arange(n1, dtype=jnp.int32)
    a1 = (2.0 * math.pi / n1) * ((i1[:, None] * i1[None, :]) % n1).astype(F32)
    f1 = jnp.concatenate([jnp.cos(a1), -jnp.sin(a1)], axis=0).astype(BF16)
    i2 = jnp.arange(n2, dtype=jnp.int32)
    p = i1[:, None, None] + n1 * i2[None, :, None]
    k = (p * i2[None, None, :]) % S
    a2 = (2.0 * math.pi / S) * k.astype(F32)
    c2, s2 = jnp.cos(a2), jnp.sin(a2)
    g = jnp.concatenate([jnp.concatenate([c2, s2], axis=2),
                         jnp.concatenate([-s2, c2], axis=2)], axis=1).astype(BF16)
    ac = (2.0 * math.pi / HEAD_DIM) * ((i2[:, None] * i2[None, :]) % HEAD_DIM).astype(F32)
    cs = jnp.concatenate([jnp.cos(ac), jnp.sin(ac)], axis=0).astype(BF16)
    return f1, g, cs


def _fourier_stage1_kernel(a_ref, f1_ref, y_ref):
    n1 = a_ref.shape[0]
    res = jnp.dot(f1_ref[...], a_ref[...], preferred_element_type=F32)
    y_ref[0] = res[:n1].astype(BF16)
    y_ref[1] = res[n1:].astype(BF16)


def _fourier_stage2_kernel(y_ref, g_ref, cs_ref, o_ref, *, scale):
    n_p = y_ref.shape[1]
    cs = cs_ref[...]
    for p in range(n_p):
        ycat = jnp.concatenate([y_ref[0, p], y_ref[1, p]], axis=0)
        x = jnp.dot(g_ref[p], ycat, preferred_element_type=F32).astype(BF16)
        for grp in range(4):
            lanes = slice(grp * HEAD_DIM, (grp + 1) * HEAD_DIM)
            xg = jnp.concatenate([x[:HEAD_DIM, lanes], x[HEAD_DIM:, lanes]], axis=1)
            o = jnp.dot(xg, cs, preferred_element_type=F32) * scale
            col = p * GROUP_W + grp * HEAD_DIM
            o_ref[:, col:col + HEAD_DIM] = _head_norm(o).astype(BF16)


def _fourier(u3, B, S, tables):
    f1, g, cs = tables
    n2 = HEAD_DIM
    n1 = S // n2
    T = B * S
    ncol = min(16, n2)
    a_view = u3[BLK_FOURIER].reshape(B, n1, n2 * GROUP_W)
    y = pl.pallas_call(
        _fourier_stage1_kernel,
        grid=(B, n2 // ncol),
        in_specs=[
            pl.BlockSpec((None, n1, ncol * GROUP_W), lambda b, j: (b, 0, j)),
            pl.BlockSpec((2 * n1, n1), lambda b, j: (0, 0)),
        ],
        out_specs=pl.BlockSpec((None, 2, n1, ncol * GROUP_W), lambda b, j: (b, 0, 0, j)),
        out_shape=jax.ShapeDtypeStruct((B, 2, n1, n2 * GROUP_W), BF16),
        compiler_params=_params(("parallel", "parallel")),
        name="fourier_stage1",
    )(a_view, f1)
    y5 = y.reshape(B, 2, n1, n2, GROUP_W)
    n_p = 8
    out = pl.pallas_call(
        functools.partial(_fourier_stage2_kernel, scale=1.0 / math.sqrt(S * HEAD_DIM)),
        grid=(B, n1 // n_p),
        in_specs=[
            pl.BlockSpec((None, 2, n_p, n2, GROUP_W), lambda b, j: (b, 0, j, 0, 0)),
            pl.BlockSpec((n_p, 2 * n2, 2 * n2), lambda b, j: (j, 0, 0)),
            pl.BlockSpec((2 * HEAD_DIM, HEAD_DIM), lambda b, j: (0, 0)),
        ],
        out_specs=pl.BlockSpec((None, n2, n_p * GROUP_W), lambda b, j: (b, 0, j)),
        out_shape=jax.ShapeDtypeStruct((B, n2, n1 * GROUP_W), BF16),
        compiler_params=_params(("parallel", "parallel")),
        name="fourier_stage2",
    )(y5, g, cs)
    return out.reshape(T, GROUP_W)


def _log_sigmoid(x):
    return jnp.minimum(x, 0.0) - jnp.log1p(jnp.exp(-jnp.abs(x)))


def _split3(x):
    hi = x.astype(BF16)
    r1 = x - hi.astype(F32)
    mid = r1.astype(BF16)
    return hi, mid, (r1 - mid.astype(F32)).astype(BF16)


def _mlstm_chunk(d, q, k, v, gi, gf, ib, fb, tri, mask, c_ref, n_ref, m_ref, out_ref, rows):
    li = gi + ib
    lf = _log_sigmoid(gf + fb)
    b = sum(jnp.dot(tri, part, preferred_element_type=F32) for part in _split3(lf))
    g = jnp.sum(lf, axis=0, keepdims=True)
    w_end = g - b + li
    m_loc = jnp.max(w_end, axis=0, keepdims=True)
    e_t = jnp.exp(w_end - m_loc).T * QK_SCALE
    e_parts = _split3(e_t[:16, :])
    a_t = (li - b).T
    m_old = m_ref[d:d + 1, :]
    m_new = jnp.maximum(g + m_old, m_loc)
    a_sc = jnp.exp(g + m_old - m_new)
    b_sc = jnp.exp(m_loc - m_new)
    lane = lax.broadcasted_iota(jnp.int32, (1, HEAD_DIM), 1)

    def pick(x, hp):
        return jnp.sum(jnp.where(lane == hp, x, 0.0), axis=-1, keepdims=True)

    for h in range(MLSTM_HEADS):
        hp = MLSTM_HEADS * d + h
        lanes = slice(h * HEAD_DIM, (h + 1) * HEAD_DIM)
        qh = q[:, lanes]
        vh = v[:, lanes]
        kh = k[:, lanes]
        bc = pick(b, hp)
        dm = jnp.where(mask, bc + a_t[hp:hp + 1, :], -jnp.inf)
        inter_log = bc + pick(m_old, hp)
        m_t = jnp.maximum(inter_log, jnp.max(dm, axis=-1, keepdims=True))
        p = jnp.exp(dm - m_t)
        inter_w = jnp.exp(inter_log - m_t)
        qk = lax.dot_general(qh, kh, (((1,), (1,)), ((), ())), preferred_element_type=F32)
        a_mat = p * (qk * QK_SCALE)
        c_prev = c_ref[hp]
        n_prev = n_ref[hp:hp + 1, :]
        num = (jnp.dot(a_mat.astype(BF16), vh, preferred_element_type=F32)
               + inter_w * jnp.dot(qh, c_prev.astype(BF16), preferred_element_type=F32))
        den = (jnp.sum(a_mat, axis=-1, keepdims=True)
               + inter_w * jnp.sum(qh.astype(F32) * n_prev, axis=-1, keepdims=True))
        hout = num / jnp.maximum(jnp.abs(den), jnp.exp(-m_t))
        out_ref[rows, lanes] = hout.astype(out_ref.dtype)
        ke_t = kh.astype(F32).T * e_t[hp:hp + 1, :]
        c_loc = jnp.dot(ke_t.astype(BF16), vh, preferred_element_type=F32)
        n_loc = sum(jnp.dot(part, kh, preferred_element_type=F32) for part in e_parts)[hp:hp + 1, :]
        ah = pick(a_sc, hp)
        bh = pick(b_sc, hp)
        c_ref[hp] = ah * c_prev + bh * c_loc
        n_ref[hp:hp + 1, :] = ah * n_prev + bh * n_loc
    m_ref[d:d + 1, :] = m_new


def _mlstm_kernel(qf_ref, kf_ref, vf_ref, gf_ref, qb_ref, kb_ref, vb_ref, gb_ref, ib_ref, fb_ref,
                  hf_ref, hb_ref, c_ref, n_ref, m_ref):
    L = MLSTM_CHUNK
    n_sub = qf_ref.shape[0] // L

    @pl.when(pl.program_id(1) == 0)
    def _():
        c_ref[...] = jnp.zeros_like(c_ref)
        n_ref[...] = jnp.zeros_like(n_ref)
        m_ref[...] = jnp.zeros_like(m_ref)

    row = lax.broadcasted_iota(jnp.int32, (L, L), 0)
    col = lax.broadcasted_iota(jnp.int32, (L, L), 1)
    mask_f = col <= row
    mask_b = col >= row
    tri_f = jnp.where(mask_f, 1.0, 0.0).astype(BF16)
    tri_b = jnp.where(mask_b, 1.0, 0.0).astype(BF16)
    ib = ib_ref[...]
    fb = fb_ref[...]

    def body(j, carry):
        rf = pl.ds(pl.multiple_of(j * L, L), L)
        rb = pl.ds(pl.multiple_of((n_sub - 1 - j) * L, L), L)
        _mlstm_chunk(0, qf_ref[rf, :], kf_ref[rf, :], vf_ref[rf, :], gf_ref[rf, :HEAD_DIM], gf_ref[rf, HEAD_DIM:],
                     ib, fb, tri_f, mask_f, c_ref, n_ref, m_ref, hf_ref, rf)
        _mlstm_chunk(1, qb_ref[rb, :], kb_ref[rb, :], vb_ref[rb, :], gb_ref[rb, :HEAD_DIM], gb_ref[rb, HEAD_DIM:],
                     ib, fb, tri_b, mask_b, c_ref, n_ref, m_ref, hb_ref, rb)
        return carry

    lax.fori_loop(0, n_sub, body, 0)


def _mlstm(u3, gates, ib, fb, B, S, n_sub=4):
    T = B * S
    rows = n_sub * MLSTM_CHUNK
    nblk = S // rows
    u4 = u3.reshape(N_COLBLK, B, S, GROUP_W)
    g3 = gates.reshape(B, S, GATE_W)

    def fwd(blk):
        return pl.BlockSpec((None, None, rows, GROUP_W), lambda b, c: (blk, b, c, 0))

    def bwd(blk):
        return pl.BlockSpec((None, None, rows, GROUP_W), lambda b, c: (blk, b, nblk - 1 - c, 0))

    hf, hb = pl.pallas_call(
        _mlstm_kernel,
        grid=(B, nblk),
        in_specs=[
            fwd(BLK_MQ), fwd(BLK_MK), fwd(BLK_MV),
            pl.BlockSpec((None, rows, GATE_W), lambda b, c: (b, c, 0)),
            bwd(BLK_MQ), bwd(BLK_MK), bwd(BLK_MV),
            pl.BlockSpec((None, rows, GATE_W), lambda b, c: (b, nblk - 1 - c, 0)),
            pl.BlockSpec((1, HEAD_DIM), lambda b, c: (0, 0)),
            pl.BlockSpec((1, HEAD_DIM), lambda b, c: (0, 0)),
        ],
        out_specs=[
            pl.BlockSpec((None, rows, GROUP_W), lambda b, c: (b, c, 0)),
            pl.BlockSpec((None, rows, GROUP_W), lambda b, c: (b, nblk - 1 - c, 0)),
        ],
        out_shape=[jax.ShapeDtypeStruct((B, S, GROUP_W), BF16)] * 2,
        scratch_shapes=[
            pltpu.VMEM((2 * MLSTM_HEADS, HEAD_DIM, HEAD_DIM), F32),
            pltpu.VMEM((2 * MLSTM_HEADS, HEAD_DIM), F32),
            pltpu.VMEM((8, HEAD_DIM), F32),
        ],
        compiler_params=_params(("parallel", "arbitrary")),
        name="mlstm",
    )(u4, u4, u4, g3, u4, u4, u4, g3, ib, fb)
    return hf.reshape(T, GROUP_W), hb.reshape(T, GROUP_W)


def _rope_tables(S):
    rows = S // GRID_W
    row = jnp.repeat(jnp.arange(rows), GRID_W).astype(F32)
    colp = jnp.tile(jnp.arange(GRID_W), rows).astype(F32)
    inv = ROPE_THETA ** (-jnp.arange(ROPE_FREQS, dtype=F32) / ROPE_FREQS)
    ar = row[:, None] * inv
    ac = colp[:, None] * inv
    ang = jnp.concatenate([ar, ar, ac, ac], axis=1)
    first_half = (jnp.arange(HEAD_DIM) % (2 * ROPE_FREQS)) < ROPE_FREQS
    return jnp.cos(ang), jnp.where(first_half, -1.0, 1.0) * jnp.sin(ang)


def _rope(x, cos, sin_signed, first_half):
    partner = jnp.where(first_half, pltpu.roll(x, HEAD_DIM - ROPE_FREQS, axis=1), pltpu.roll(x, ROPE_FREQS, axis=1))
    return x * cos + partner * sin_signed


def _attn_prep_kernel(aq_ref, akv_ref, cos_ref, sin_ref, qg_ref, kg_ref, q_ref, kt_ref, va_ref, kn_ref, qn_ref):
    cos = cos_ref[...]
    sin = sin_ref[...]
    lane = lax.broadcasted_iota(jnp.int32, cos.shape, 1)
    first_half = (lane % (2 * ROPE_FREQS)) < ROPE_FREQS
    qg = qg_ref[...]
    kg = kg_ref[...]
    for h in range(4):
        lanes = slice(h * HEAD_DIM, (h + 1) * HEAD_DIM)
        qh = _head_norm(aq_ref[:, lanes].astype(F32)) * qg
        qb = (_rope(qh, cos, sin, first_half) * (QK_SCALE * LOG2_E)).astype(BF16)
        q_ref[:, lanes] = qb
        q2 = jnp.max(jnp.sum(jnp.square(qb.astype(F32)), axis=-1, keepdims=True), axis=0, keepdims=True)
        qn_ref[h] = jnp.broadcast_to(q2, qn_ref.shape[1:])
    for h in range(ATTN_KV_HEADS):
        lanes = slice(h * HEAD_DIM, (h + 1) * HEAD_DIM)
        kh = _head_norm(akv_ref[:, lanes].astype(F32)) * kg
        kb = _rope(kh, cos, sin, first_half).astype(BF16)
        kt_ref[h] = kb.astype(F32).T.astype(BF16)
        k2 = jnp.max(jnp.sum(jnp.square(kb.astype(F32)), axis=-1, keepdims=True), axis=0, keepdims=True)
        kn_ref[h] = jnp.broadcast_to(k2, kn_ref.shape[1:])
        vl = slice((ATTN_KV_HEADS + h) * HEAD_DIM, (ATTN_KV_HEADS + h + 1) * HEAD_DIM)
        va_ref[h, :, :HEAD_DIM] = akv_ref[:, vl]
        va_ref[h, :, HEAD_DIM:] = jnp.ones((akv_ref.shape[0], HEAD_DIM), BF16)


def _flash_kernel(q_ref, kt_ref, va_ref, kn_ref, qn_ref, o_ref, m_ref, acc_ref, p_ref, mp_ref):
    tq = q_ref.shape[0]
    nkc, _, tk = kt_ref.shape
    n_rep = tk // HEAD_DIM
    q2 = jnp.concatenate([q_ref[:, :HEAD_DIM], q_ref[:, HEAD_DIM:]], axis=0)

    def chunk(c):
        return kt_ref[c], va_ref[pl.ds(pl.multiple_of(c * tk, tk), tk), :]

    def lane_max(s):
        part = s[:, :HEAD_DIM]
        for t in range(1, n_rep):
            part = jnp.maximum(part, s[:, t * HEAD_DIM:(t + 1) * HEAD_DIM])
        return part

    q_sq = jnp.max(jnp.max(qn_ref[...], axis=0), axis=0, keepdims=True)
    k_sq = jnp.max(jnp.max(kn_ref[...], axis=0), axis=0, keepdims=True)
    safe = jnp.max(jnp.sqrt(q_sq * k_sq)) <= FRAME_MARGIN
    acc_ref[...] = jnp.zeros_like(acc_ref)

    def fast_scores(c, slot):
        kt, _ = chunk(c)
        s = jnp.dot(q2, kt, preferred_element_type=F32)
        p_ref[slot] = jnp.exp2(s - jnp.tile(m_ref[...], (1, n_rep))).astype(BF16)
        mp_ref[slot] = lane_max(s)

    def fast_accumulate(c, slot):
        _, vc = chunk(c)
        m_prev = m_ref[...]
        m_new = jnp.maximum(m_prev, jnp.max(mp_ref[slot], axis=-1, keepdims=True))
        alpha = jnp.exp2(m_prev - m_new)
        acc_ref[...] = (acc_ref[...] + jnp.dot(p_ref[slot], vc, preferred_element_type=F32)) * jnp.tile(alpha, (1, 2))
        m_ref[...] = m_new

    def fast():
        m_ref[...] = jnp.zeros_like(m_ref)

        group = 8 if nkc % 8 == 0 else 2

        def body(i, carry):
            for t in range(group):
                fast_scores(group * i + t, t % 2)
                fast_accumulate(group * i + t, t % 2)
            return carry

        lax.fori_loop(0, nkc // group, body, 0)

    def exact():
        m_ref[...] = jnp.full_like(m_ref, -jnp.inf)

        def body(c, carry):
            kt, vc = chunk(c)
            s = jnp.dot(q2, kt, preferred_element_type=F32)
            m_prev = m_ref[...]
            m_new = jnp.maximum(m_prev, jnp.max(lane_max(s), axis=-1, keepdims=True))
            p = jnp.exp2(s - jnp.tile(m_new, (1, n_rep))).astype(BF16)
            acc_ref[...] = (jnp.tile(jnp.exp2(m_prev - m_new), (1, 2)) * acc_ref[...]
                            + jnp.dot(p, vc, preferred_element_type=F32))
            m_ref[...] = m_new
            return carry

        lax.fori_loop(0, nkc, body, 0)

    lax.cond(safe, fast, exact)
    o = acc_ref[:, :HEAD_DIM] / acc_ref[:, HEAD_DIM:]
    o_ref[:, :HEAD_DIM] = o[:tq].astype(o_ref.dtype)
    o_ref[:, HEAD_DIM:] = o[tq:].astype(o_ref.dtype)


def _axial_gqa(u3, q_gain, k_gain, rope, B, S, tq=512, tk=1024):
    T = B * S
    cos, sin = rope
    u4 = u3.reshape(N_COLBLK, B, S, GROUP_W)
    tk = min(tk, S // 2)
    nkc = S // tk
    assert nkc >= 2 and nkc % 2 == 0, "flash kernel pipelines kv chunks in pairs"
    q_r, kt, va, kn, qn = pl.pallas_call(
        _attn_prep_kernel,
        grid=(B, nkc),
        in_specs=[
            pl.BlockSpec((None, None, tk, GROUP_W), lambda b, i: (BLK_AQ, b, i, 0)),
            pl.BlockSpec((None, None, tk, GROUP_W), lambda b, i: (BLK_AKV, b, i, 0)),
            pl.BlockSpec((tk, HEAD_DIM), lambda b, i: (i, 0)),
            pl.BlockSpec((tk, HEAD_DIM), lambda b, i: (i, 0)),
            pl.BlockSpec((1, HEAD_DIM), lambda b, i: (0, 0)),
            pl.BlockSpec((1, HEAD_DIM), lambda b, i: (0, 0)),
        ],
        out_specs=[
            pl.BlockSpec((None, tk, GROUP_W), lambda b, i: (b, i, 0)),
            pl.BlockSpec((None, ATTN_KV_HEADS, None, HEAD_DIM, tk), lambda b, i: (b, 0, i, 0, 0)),
            pl.BlockSpec((None, ATTN_KV_HEADS, tk, 2 * HEAD_DIM), lambda b, i: (b, 0, i, 0)),
            pl.BlockSpec((None, ATTN_KV_HEADS, None, 8, HEAD_DIM), lambda b, i: (b, 0, i, 0, 0)),
            pl.BlockSpec((None, 4, None, 8, HEAD_DIM), lambda b, i: (b, 0, i, 0, 0)),
        ],
        out_shape=[
            jax.ShapeDtypeStruct((B, S, GROUP_W), BF16),
            jax.ShapeDtypeStruct((B, ATTN_KV_HEADS, nkc, HEAD_DIM, tk), BF16),
            jax.ShapeDtypeStruct((B, ATTN_KV_HEADS, S, 2 * HEAD_DIM), BF16),
            jax.ShapeDtypeStruct((B, ATTN_KV_HEADS, nkc, 8, HEAD_DIM), F32),
            jax.ShapeDtypeStruct((B, 4, nkc, 8, HEAD_DIM), F32),
        ],
        compiler_params=_params(("parallel", "parallel")),
        name="attn_prep",
    )(u4, u4, cos, sin, q_gain, k_gain)
    out = pl.pallas_call(
        _flash_kernel,
        grid=(B, ATTN_KV_HEADS, S // tq),
        in_specs=[
            pl.BlockSpec((None, tq, 2 * HEAD_DIM), lambda b, k, i: (b, i, k)),
            pl.BlockSpec((None, None, nkc, HEAD_DIM, tk), lambda b, k, i: (b, k, 0, 0, 0)),
            pl.BlockSpec((None, None, S, 2 * HEAD_DIM), lambda b, k, i: (b, k, 0, 0)),
            pl.BlockSpec((None, None, nkc, 8, HEAD_DIM), lambda b, k, i: (b, k, 0, 0, 0)),
            pl.BlockSpec((None, 2, None, 8, HEAD_DIM), lambda b, k, i: (b, k, (i * tq) // tk, 0, 0)),
        ],
        out_specs=pl.BlockSpec((None, tq, 2 * HEAD_DIM), lambda b, k, i: (b, i, k)),
        out_shape=jax.ShapeDtypeStruct((B, S, GROUP_W), BF16),
        scratch_shapes=[
            pltpu.VMEM((2 * tq, HEAD_DIM), F32),
            pltpu.VMEM((2 * tq, 2 * HEAD_DIM), F32),
            pltpu.VMEM((2, 2 * tq, tk), BF16),
            pltpu.VMEM((2, 2 * tq, HEAD_DIM), F32),
        ],
        compiler_params=_params(("parallel", "parallel", "arbitrary")),
        name="flash_gqa",
    )(q_r, kt, va, kn, qn)
    return out.reshape(T, GROUP_W)


def _combine_kernel(x_ref, ya_ref, cb_ref, cc_ref, cx_ref, ccp_ref, cxp_ref, ccn_ref, cxn_ref, mo_ref,
                    hf_ref, hb_ref, yd_ref, cw_ref, cbias_ref, gain_ref, w_ref, o_ref, y_ref, *, tiles_per_seq):
    tm = x_ref.shape[0]
    i = pl.program_id(0)
    halo = ccp_ref.shape[0]
    at_start = (i % tiles_per_seq) == 0
    at_end = (i % tiles_per_seq) == tiles_per_seq - 1
    gain = gain_ref[...]

    z = cc_ref[...].astype(F32) * cx_ref[...].astype(F32)
    z_before = ccp_ref[halo - 1:halo, :].astype(F32) * cxp_ref[halo - 1:halo, :].astype(F32)
    z_after = ccn_ref[0:1, :].astype(F32) * cxn_ref[0:1, :].astype(F32)
    z_before = jnp.where(at_start, 0.0, z_before)
    z_after = jnp.where(at_end, 0.0, z_after)
    rid = lax.broadcasted_iota(jnp.int32, z.shape, 0)
    z_m1 = jnp.where(rid == 0, z_before, pltpu.roll(z, 1, axis=0))
    z_p1 = jnp.where(rid == tm - 1, z_after, pltpu.roll(z, tm - 1, axis=0))
    conv = z_m1 * cw_ref[0:1, :] + z * cw_ref[1:2, :] + z_p1 * cw_ref[2:3, :] + cbias_ref[...]
    yb = cb_ref[...].astype(F32) * conv

    for grp in range(4):
        lanes = slice(grp * HEAD_DIM, (grp + 1) * HEAD_DIM)
        y_ref[:, lanes] = (ya_ref[:, lanes].astype(F32) * gain[:, lanes]).astype(BF16)
        gl = slice(GROUP_W + grp * HEAD_DIM, GROUP_W + (grp + 1) * HEAD_DIM)
        y_ref[:, gl] = (_head_norm(yb[:, lanes]) * gain[:, gl]).astype(BF16)
        gl = slice(2 * GROUP_W + grp * HEAD_DIM, 2 * GROUP_W + (grp + 1) * HEAD_DIM)
        yc = _head_norm(hf_ref[:, lanes].astype(F32) + hb_ref[:, lanes].astype(F32))
        y_ref[:, gl] = (jax.nn.sigmoid(mo_ref[:, lanes].astype(F32)) * yc * gain[:, gl]).astype(BF16)
        gl = slice(3 * GROUP_W + grp * HEAD_DIM, 3 * GROUP_W + (grp + 1) * HEAD_DIM)
        y_ref[:, gl] = (_head_norm(yd_ref[:, lanes].astype(F32)) * gain[:, gl]).astype(BF16)

    o_ref[...] = x_ref[...] + jnp.dot(y_ref[...], w_ref[...], preferred_element_type=F32)


def _combine_out(x2, ya, u3, hf, hb, yd, conv_w, conv_b, out_gain, w_out, S, tm=512, halo=16):
    T = x2.shape[0]
    hb_per_tile = tm // halo
    n_halo = T // halo

    def u_blk(blk):
        return pl.BlockSpec((None, tm, GROUP_W), lambda i: (blk, i, 0))

    def u_prev(blk):
        return pl.BlockSpec((None, halo, GROUP_W), lambda i: (blk, jnp.maximum(i * hb_per_tile - 1, 0), 0))

    def u_next(blk):
        return pl.BlockSpec((None, halo, GROUP_W), lambda i: (blk, jnp.minimum((i + 1) * hb_per_tile, n_halo - 1), 0))

    tok = pl.BlockSpec((tm, GROUP_W), lambda i: (i, 0))
    return pl.pallas_call(
        functools.partial(_combine_kernel, tiles_per_seq=S // tm),
        grid=(T // tm,),
        in_specs=[
            pl.BlockSpec((tm, D_MODEL), lambda i: (i, 0)),
            tok,
            u_blk(BLK_CB), u_blk(BLK_CC), u_blk(BLK_CX),
            u_prev(BLK_CC), u_prev(BLK_CX), u_next(BLK_CC), u_next(BLK_CX),
            u_blk(BLK_MO),
            tok, tok, tok,
            pl.BlockSpec((8, GROUP_W), lambda i: (0, 0)),
            pl.BlockSpec((1, GROUP_W), lambda i: (0, 0)),
            pl.BlockSpec((1, D_MODEL), lambda i: (0, 0)),
            pl.BlockSpec((D_MODEL, D_MODEL), lambda i: (0, 0)),
        ],
        out_specs=pl.BlockSpec((tm, D_MODEL), lambda i: (i, 0)),
        out_shape=jax.ShapeDtypeStruct((T, D_MODEL), F32),
        scratch_shapes=[pltpu.VMEM((tm, D_MODEL), BF16)],
        compiler_params=_params(("parallel",)),
        name="combine_out",
    )(x2, ya, u3, u3, u3, u3, u3, u3, u3, u3, hf, hb, yd, conv_w, conv_b, out_gain, w_out)


def _mem_kv_kernel(mem_ref, g_ref, w_ref, kg_ref, kt_ref, v_ref, h_ref):
    _norm_rows_to(mem_ref, g_ref, h_ref)
    kv = jnp.dot(h_ref[...], w_ref[...], preferred_element_type=F32)
    kg = kg_ref[...]
    for h in range(CA_HEADS):
        lanes = slice(h * HEAD_DIM, (h + 1) * HEAD_DIM)
        kt_ref[h] = (_head_norm(kv[:, lanes]) * kg).T.astype(BF16)
    v_ref[...] = kv[:, CA_HEADS * HEAD_DIM:].astype(BF16)


def _mem_kv(mem, g_mem, w_kv, k_gain):
    B = mem.shape[0]
    return pl.pallas_call(
        _mem_kv_kernel,
        grid=(B,),
        in_specs=[
            pl.BlockSpec((None, N_MEM, D_MODEL), lambda b: (b, 0, 0)),
            pl.BlockSpec((1, D_MODEL), lambda b: (0, 0)),
            pl.BlockSpec((D_MODEL, 2 * GROUP_W), lambda b: (0, 0)),
            pl.BlockSpec((1, HEAD_DIM), lambda b: (0, 0)),
        ],
        out_specs=[
            pl.BlockSpec((None, CA_HEADS, HEAD_DIM, N_MEM), lambda b: (b, 0, 0, 0)),
            pl.BlockSpec((None, N_MEM, GROUP_W), lambda b: (b, 0, 0)),
        ],
        out_shape=[
            jax.ShapeDtypeStruct((B, CA_HEADS, HEAD_DIM, N_MEM), BF16),
            jax.ShapeDtypeStruct((B, N_MEM, GROUP_W), BF16),
        ],
        scratch_shapes=[pltpu.VMEM((N_MEM, D_MODEL), BF16)],
        compiler_params=_params(("parallel",)),
        name="mem_kv",
    )(mem, g_mem, w_kv, k_gain)


def _cross_attn_rows(x_ref, g_ref, wq_ref, qg_ref, kt_ref, v_ref, wo_ref, o_ref, h_ref, oc_ref):
    _norm_rows_to(x_ref, g_ref, h_ref)
    q = jnp.dot(h_ref[...], wq_ref[...], preferred_element_type=F32)
    qg = qg_ref[...]
    for h in range(CA_HEADS):
        lanes = slice(h * HEAD_DIM, (h + 1) * HEAD_DIM)
        qh = (_head_norm(q[:, lanes]) * (qg * QK_SCALE)).astype(BF16)
        s = jnp.dot(qh, kt_ref[h], preferred_element_type=F32)
        e = jnp.exp(s - jnp.max(s, axis=-1, keepdims=True))
        o = jnp.dot(e.astype(BF16), v_ref[:, lanes], preferred_element_type=F32)
        oc_ref[:, lanes] = (o * pl.reciprocal(jnp.sum(e, axis=-1, keepdims=True), approx=False)).astype(BF16)
    o_ref[...] = x_ref[...] + jnp.dot(oc_ref[...], wo_ref[...], preferred_element_type=F32)


def _ca_mlp_kernel(x_ref, gca_ref, wq_ref, qg_ref, kt_ref, v_ref, wo_ref, g_ref, wu_ref, wd_ref, o_ref, h_ref, oc_ref):
    @pl.when(pl.program_id(2) == 0)
    def _():
        _cross_attn_rows(x_ref, gca_ref, wq_ref, qg_ref, kt_ref, v_ref, wo_ref, o_ref, h_ref, oc_ref)
        _norm_rows_to(o_ref, g_ref, h_ref)

    up = jnp.dot(h_ref[...], wu_ref[...], preferred_element_type=F32)
    act = jnp.square(jnp.maximum(up, 0.0)).astype(BF16)
    o_ref[...] += jnp.dot(act, wd_ref[...], preferred_element_type=F32)


def _cross_attn_mlp(x2, g_ca, w_q, q_gain, kt, v, w_o, g_mlp, w_up, w_down, B, S, tm=512, tf=1024):
    T = x2.shape[0]
    x3 = x2.reshape(B, S, D_MODEL)
    const = lambda b, i, j: (0, 0)
    out = pl.pallas_call(
        _ca_mlp_kernel,
        grid=(B, S // tm, D_FF // tf),
        in_specs=[
            pl.BlockSpec((None, tm, D_MODEL), lambda b, i, j: (b, i, 0)),
            pl.BlockSpec((1, D_MODEL), const),
            pl.BlockSpec((D_MODEL, GROUP_W), const),
            pl.BlockSpec((1, HEAD_DIM), const),
            pl.BlockSpec((None, CA_HEADS, HEAD_DIM, N_MEM), lambda b, i, j: (b, 0, 0, 0)),
            pl.BlockSpec((None, N_MEM, GROUP_W), lambda b, i, j: (b, 0, 0)),
            pl.BlockSpec((GROUP_W, D_MODEL), const),
            pl.BlockSpec((1, D_MODEL), const),
            pl.BlockSpec((D_MODEL, tf), lambda b, i, j: (0, j)),
            pl.BlockSpec((tf, D_MODEL), lambda b, i, j: (j, 0)),
        ],
        out_specs=pl.BlockSpec((None, tm, D_MODEL), lambda b, i, j: (b, i, 0)),
        out_shape=jax.ShapeDtypeStruct((B, S, D_MODEL), F32),
        scratch_shapes=[pltpu.VMEM((tm, D_MODEL), BF16), pltpu.VMEM((tm, GROUP_W), BF16)],
        compiler_params=_params(("parallel", "parallel", "arbitrary")),
        name="cross_attn_mlp",
    )(x3, g_ca, w_q, q_gain, kt, v, w_o, g_mlp, w_up, w_down)
    return out.reshape(T, D_MODEL)


def _prep_weights(w_in, conv_w, i_bias, f_bias, w_out, w_ca_q, w_ca_kv, w_ca_o, w_up, w_down):
    n_main = 8 * GROUP_W
    w_main = jnp.concatenate([w_in[:, :, :n_main], w_in[:, :, n_main + 16:]], axis=2).astype(BF16)
    pad = jnp.zeros((DEPTH, D_MODEL, HEAD_DIM - 8), F32)
    w_gate = jnp.concatenate([w_in[:, :, n_main:n_main + 8], pad, w_in[:, :, n_main + 8:n_main + 16], pad],
                             axis=2).astype(BF16)
    lane_pad = jnp.zeros((DEPTH, 1, HEAD_DIM - 8), F32)
    ib = jnp.concatenate([i_bias.reshape(DEPTH, 1, 8), lane_pad], axis=2)
    fb = jnp.concatenate([f_bias.reshape(DEPTH, 1, 8), lane_pad], axis=2)
    cw = jnp.concatenate([conv_w, jnp.zeros((DEPTH, 5, GROUP_W), F32)], axis=1)
    return dict(w_main=w_main, w_gate=w_gate, ib=ib, fb=fb, cw=cw, w_out=w_out.astype(BF16),
                w_ca_q=w_ca_q.astype(BF16), w_ca_kv=w_ca_kv.astype(BF16), w_ca_o=w_ca_o.astype(BF16),
                w_up=w_up.astype(BF16), w_down=w_down.astype(BF16))


def _trunk(x, mem, pw, g_mix, conv_b, attn_q_norm, attn_k_norm, out_gain, g_ca, g_mem, ca_q_norm, ca_k_norm, g_mlp):
    B, S, _ = x.shape
    T = B * S
    x2 = x.reshape(T, D_MODEL)
    rope = _rope_tables(S)
    ftab = _fourier_tables(S)
    for l in range(DEPTH):
        u3, gates = _norm_proj(x2, g_mix[l][None], pw["w_main"][l], pw["w_gate"][l])
        ya = _fourier(u3, B, S, ftab)
        hf, hb = _mlstm(u3, gates, pw["ib"][l], pw["fb"][l], B, S)
        yd = _axial_gqa(u3, attn_q_norm[l][None], attn_k_norm[l][None], rope, B, S)
        x2 = _combine_out(x2, ya, u3, hf, hb, yd, pw["cw"][l], conv_b[l][None], out_gain[l][None], pw["w_out"][l], S)
        kt, v = _mem_kv(mem, g_mem[l][None], pw["w_ca_kv"][l], ca_k_norm[l][None])
        x2 = _cross_attn_mlp(x2, g_ca[l][None], pw["w_ca_q"][l], ca_q_norm[l][None], kt, v, pw["w_ca_o"][l],
                             g_mlp[l][None], pw["w_up"][l], pw["w_down"][l], B, S)
    return x2.reshape(B, S, D_MODEL)


def kernel(x_prompt, x_sample, mem_prompt, mem_sample, g_mix, w_in, conv_w, conv_b, i_bias, f_bias, attn_q_norm,
           attn_k_norm, out_gain, w_out, g_ca, g_mem, w_ca_q, w_ca_kv, ca_q_norm, ca_k_norm, w_ca_o, g_mlp, w_up,
           w_down):
    pw = _prep_weights(w_in, conv_w, i_bias, f_bias, w_out, w_ca_q, w_ca_kv, w_ca_o, w_up, w_down)
    args = (pw, g_mix, conv_b, attn_q_norm, attn_k_norm, out_gain, g_ca, g_mem, ca_q_norm, ca_k_norm, g_mlp)
    y_prompt = _trunk(x_prompt, mem_prompt, *args)
    y_sample = _trunk(x_sample, mem_sample, *args)
    return (y_prompt, y_sample)
```

```python
import functools
import math

import jax
import jax.numpy as jnp
from jax import lax
from jax.experimental import pallas as pl
from jax.experimental.pallas import tpu as pltpu

F32 = jnp.float32
BF16 = jnp.bfloat16

D_MODEL = 2048
DEPTH = 4
HEAD_DIM = 128
GROUP_W = 4 * HEAD_DIM
N_COLBLK = 10
GATE_W = 2 * HEAD_DIM
MLSTM_HEADS = 4
MLSTM_CHUNK = 128
ATTN_KV_HEADS = 2
GRID_W = 64
ROPE_THETA = 10000.0
ROPE_FREQS = HEAD_DIM // 4
CA_HEADS = 4
N_MEM = 256
D_FF = 4 * D_MODEL
RMS_EPS = 1e-6
QK_SCALE = HEAD_DIM ** -0.5
LOG2_E = math.log2(math.e)
FRAME_MARGIN = 64.0

BLK_FOURIER, BLK_CB, BLK_CC, BLK_CX, BLK_MQ, BLK_MK, BLK_MV, BLK_MO, BLK_AQ, BLK_AKV = range(N_COLBLK)

V7X_VMEM_LIMIT = 56 * 1024 * 1024
NORM_ROWS = 256


def _params(semantics, vmem=V7X_VMEM_LIMIT):
    return pltpu.CompilerParams(dimension_semantics=semantics, vmem_limit_bytes=vmem)


def _head_norm(y):
    return y * lax.rsqrt(jnp.mean(y * y, axis=-1, keepdims=True) + RMS_EPS)


def _norm_rows_to(x_ref, g_ref, h_ref):
    tm = x_ref.shape[0]
    g = g_ref[...]

    def body(r, carry):
        rows = pl.ds(pl.multiple_of(r * NORM_ROWS, NORM_ROWS), NORM_ROWS)
        xf = x_ref[rows, :]
        ms = jnp.mean(xf * xf, axis=-1, keepdims=True)
        h_ref[rows, :] = (xf * lax.rsqrt(ms + RMS_EPS) * g).astype(BF16)
        return carry

    lax.fori_loop(0, tm // NORM_ROWS, body, 0)


def _norm_proj_kernel(x_ref, g_ref, w_ref, wg_ref, u_ref, gate_ref, a_ref, h_ref):
    @pl.when(pl.program_id(1) == 0)
    def _():
        _norm_rows_to(x_ref, g_ref, h_ref)
        gate_ref[...] = jnp.dot(h_ref[...], wg_ref[...], preferred_element_type=F32)

    res = jnp.dot(h_ref[...], w_ref[...], preferred_element_type=F32)
    for k in range(u_ref.shape[0]):
        u_ref[k] = res[:, k * GROUP_W:(k + 1) * GROUP_W].astype(BF16)

    @pl.when(pl.program_id(1) == 0)
    def _():
        a_ref[...] = res[:, BLK_FOURIER * GROUP_W:(BLK_FOURIER + 1) * GROUP_W].astype(BF16)


def _norm_proj(x2, g, w_main, w_gate, tm=1024, nb=2):
    T = x2.shape[0]
    tn = nb * GROUP_W
    return pl.pallas_call(
        _norm_proj_kernel,
        grid=(T // tm, N_COLBLK // nb),
        in_specs=[
            pl.BlockSpec((tm, D_MODEL), lambda i, j: (i, 0)),
            pl.BlockSpec((1, D_MODEL), lambda i, j: (0, 0)),
            pl.BlockSpec((D_MODEL, tn), lambda i, j: (0, j)),
            pl.BlockSpec((D_MODEL, GATE_W), lambda i, j: (0, 0)),
        ],
        out_specs=[
            pl.BlockSpec((nb, tm, GROUP_W), lambda i, j: (j, i, 0)),
            pl.BlockSpec((tm, GATE_W), lambda i, j: (i, 0)),
            pl.BlockSpec((tm, GROUP_W), lambda i, j: (i, 0)),
        ],
        out_shape=[
            jax.ShapeDtypeStruct((N_COLBLK, T, GROUP_W), BF16),
            jax.ShapeDtypeStruct((T, GATE_W), F32),
            jax.ShapeDtypeStruct((T, GROUP_W), BF16),
        ],
        scratch_shapes=[pltpu.VMEM((tm, D_MODEL), BF16)],
        compiler_params=_params(("parallel", "arbitrary")),
        name="norm_proj",
    )(x2, g, w_main, w_gate)


def _fourier_tables(S):
    n2 = HEAD_DIM
    n1 = S // n2
    i1 = jnp.arange(n1, dtype=jnp.int32)
    a1 = (2.0 * math.pi / n1) * ((i1[:, None] * i1[None, :]) % n1).astype(F32)
    f1 = jnp.concatenate([jnp.cos(a1), -jnp.sin(a1)], axis=0).astype(BF16)
    i2 = jnp.arange(n2, dtype=jnp.int32)
    p = i1[:, None, None] + n1 * i2[None, :, None]
    k = (p * i2[None, None, :]) % S
    a2 = (2.0 * math.pi / S) * k.astype(F32)
    c2, s2 = jnp.cos(a2), jnp.sin(a2)
    g = jnp.concatenate([jnp.concatenate([c2, s2], axis=2),
                         jnp.concatenate([-s2, c2], axis=2)], axis=1).astype(BF16)
    ac = (2.0 * math.pi / HEAD_DIM) * ((i2[:, None] * i2[None, :]) % HEAD_DIM).astype(F32)
    cs = jnp.concatenate([jnp.cos(ac), jnp.sin(ac)], axis=0).astype(BF16)
    return f1, g, cs


def _fourier_stage1_kernel(a_ref, f1_ref, y_ref):
    n1 = a_ref.shape[0]
    res = jnp.dot(f1_ref[...], a_ref[...], preferred_element_type=F32)
    y_ref[0] = res[:n1].astype(BF16)
    y_ref[1] = res[n1:].astype(BF16)


def _fourier_stage2_kernel(y_ref, g_ref, cs_ref, o_ref, *, scale):
    n_p = y_ref.shape[1]
    cs = cs_ref[...]
    for p in range(n_p):
        ycat = jnp.concatenate([y_ref[0, p], y_ref[1, p]], axis=0)
        x = jnp.dot(g_ref[p], ycat, preferred_element_type=F32).astype(BF16)
        for grp in range(4):
            lanes = slice(grp * HEAD_DIM, (grp + 1) * HEAD_DIM)
            xg = jnp.concatenate([x[:HEAD_DIM, lanes], x[HEAD_DIM:, lanes]], axis=1)
            o = jnp.dot(xg, cs, preferred_element_type=F32) * scale
            col = p * GROUP_W + grp * HEAD_DIM
            o_ref[:, col:col + HEAD_DIM] = _head_norm(o).astype(BF16)


def _fourier(a, B, S, tables):
    f1, g, cs = tables
    n2 = HEAD_DIM
    n1 = S // n2
    T = B * S
    ncol = min(16, n2)
    a_view = a.reshape(B, n1, n2 * GROUP_W)
    y = pl.pallas_call(
        _fourier_stage1_kernel,
        grid=(B, n2 // ncol),
        in_specs=[
            pl.BlockSpec((None, n1, ncol * GROUP_W), lambda b, j: (b, 0, j)),
            pl.BlockSpec((2 * n1, n1), lambda b, j: (0, 0)),
        ],
        out_specs=pl.BlockSpec((None, 2, n1, ncol * GROUP_W), lambda b, j: (b, 0, 0, j)),
        out_shape=jax.ShapeDtypeStruct((B, 2, n1, n2 * GROUP_W), BF16),
        compiler_params=_params(("parallel", "parallel")),
        name="fourier_stage1",
    )(a_view, f1)
    y5 = y.reshape(B, 2, n1, n2, GROUP_W)
    n_p = 8
    out = pl.pallas_call(
        functools.partial(_fourier_stage2_kernel, scale=1.0 / math.sqrt(S * HEAD_DIM)),
        grid=(B, n1 // n_p),
        in_specs=[
            pl.BlockSpec((None, 2, n_p, n2, GROUP_W), lambda b, j: (b, 0, j, 0, 0)),
            pl.BlockSpec((n_p, 2 * n2, 2 * n2), lambda b, j: (j, 0, 0)),
            pl.BlockSpec((2 * HEAD_DIM, HEAD_DIM), lambda b, j: (0, 0)),
        ],
        out_specs=pl.BlockSpec((None, n2, n_p * GROUP_W), lambda b, j: (b, 0, j)),
        out_shape=jax.ShapeDtypeStruct((B, n2, n1 * GROUP_W), BF16),
        compiler_params=_params(("parallel", "parallel")),
        name="fourier_stage2",
    )(y5, g, cs)
    return out.reshape(T, GROUP_W)


def _log_sigmoid(x):
    return jnp.minimum(x, 0.0) - jnp.log1p(jnp.exp(-jnp.abs(x)))


def _split3(x):
    hi = x.astype(BF16)
    r1 = x - hi.astype(F32)
    mid = r1.astype(BF16)
    return hi, mid, (r1 - mid.astype(F32)).astype(BF16)


def _mlstm_chunk(d, q, k, v, gi, gf, ib, fb, tri, mask, c_ref, n_ref, m_ref, out_ref, rows):
    li = gi + ib
    lf = _log_sigmoid(gf + fb)
    b = sum(jnp.dot(tri, part, preferred_element_type=F32) for part in _split3(lf))
    g = jnp.sum(lf, axis=0, keepdims=True)
    w_end = g - b + li
    m_loc = jnp.max(w_end, axis=0, keepdims=True)
    e_t = jnp.exp(w_end - m_loc).T * QK_SCALE
    e_parts = _split3(e_t[:16, :])
    a_t = (li - b).T
    m_old = m_ref[d:d + 1, :]
    m_new = jnp.maximum(g + m_old, m_loc)
    a_sc = jnp.exp(g + m_old - m_new)
    b_sc = jnp.exp(m_loc - m_new)
    lane = lax.broadcasted_iota(jnp.int32, (1, HEAD_DIM), 1)

    def pick(x, hp):
        return jnp.sum(jnp.where(lane == hp, x, 0.0), axis=-1, keepdims=True)

    for h in range(MLSTM_HEADS):
        hp = MLSTM_HEADS * d + h
        lanes = slice(h * HEAD_DIM, (h + 1) * HEAD_DIM)
        qh = q[:, lanes]
        vh = v[:, lanes]
        kh = k[:, lanes]
        bc = pick(b, hp)
        dm = jnp.where(mask, bc + a_t[hp:hp + 1, :], -jnp.inf)
        inter_log = bc + pick(m_old, hp)
        m_t = jnp.maximum(inter_log, jnp.max(dm, axis=-1, keepdims=True))
        p = jnp.exp(dm - m_t)
        inter_w = jnp.exp(inter_log - m_t)
        qk = lax.dot_general(qh, kh, (((1,), (1,)), ((), ())), preferred_element_type=F32)
        a_mat = p * (qk * QK_SCALE)
        c_prev = c_ref[hp]
        n_prev = n_ref[hp:hp + 1, :]
        num = (jnp.dot(a_mat.astype(BF16), vh, preferred_element_type=F32)
               + inter_w * jnp.dot(qh, c_prev.astype(BF16), preferred_element_type=F32))
        den = (jnp.sum(a_mat, axis=-1, keepdims=True)
               + inter_w * jnp.sum(qh.astype(F32) * n_prev, axis=-1, keepdims=True))
        hout = num / jnp.maximum(jnp.abs(den), jnp.exp(-m_t))
        out_ref[rows, lanes] = hout.astype(out_ref.dtype)
        ke_t = kh.astype(F32).T * e_t[hp:hp + 1, :]
        c_loc = jnp.dot(ke_t.astype(BF16), vh, preferred_element_type=F32)
        n_loc = sum(jnp.dot(part, kh, preferred_element_type=F32) for part in e_parts)[hp:hp + 1, :]
        ah = pick(a_sc, hp)
        bh = pick(b_sc, hp)
        c_ref[hp] = ah * c_prev + bh * c_loc
        n_ref[hp:hp + 1, :] = ah * n_prev + bh * n_loc
    m_ref[d:d + 1, :] = m_new


def _mlstm_kernel(qf_ref, kf_ref, vf_ref, gf_ref, qb_ref, kb_ref, vb_ref, gb_ref, ib_ref, fb_ref,
                  hf_ref, hb_ref, c_ref, n_ref, m_ref):
    L = MLSTM_CHUNK
    n_sub = qf_ref.shape[0] // L

    @pl.when(pl.program_id(1) == 0)
    def _():
        c_ref[...] = jnp.zeros_like(c_ref)
        n_ref[...] = jnp.zeros_like(n_ref)
        m_ref[...] = jnp.zeros_like(m_ref)

    row = lax.broadcasted_iota(jnp.int32, (L, L), 0)
    col = lax.broadcasted_iota(jnp.int32, (L, L), 1)
    mask_f = col <= row
    mask_b = col >= row
    tri_f = jnp.where(mask_f, 1.0, 0.0).astype(BF16)
    tri_b = jnp.where(mask_b, 1.0, 0.0).astype(BF16)
    ib = ib_ref[...]
    fb = fb_ref[...]

    def body(j, carry):
        rf = pl.ds(pl.multiple_of(j * L, L), L)
        rb = pl.ds(pl.multiple_of((n_sub - 1 - j) * L, L), L)
        _mlstm_chunk(0, qf_ref[rf, :], kf_ref[rf, :], vf_ref[rf, :], gf_ref[rf, :HEAD_DIM], gf_ref[rf, HEAD_DIM:],
                     ib, fb, tri_f, mask_f, c_ref, n_ref, m_ref, hf_ref, rf)
        _mlstm_chunk(1, qb_ref[rb, :], kb_ref[rb, :], vb_ref[rb, :], gb_ref[rb, :HEAD_DIM], gb_ref[rb, HEAD_DIM:],
                     ib, fb, tri_b, mask_b, c_ref, n_ref, m_ref, hb_ref, rb)
        return carry

    lax.fori_loop(0, n_sub, body, 0)


def _mlstm(u3, gates, ib, fb, B, S, n_sub=4):
    T = B * S
    rows = n_sub * MLSTM_CHUNK
    nblk = S // rows
    u4 = u3.reshape(N_COLBLK, B, S, GROUP_W)
    g3 = gates.reshape(B, S, GATE_W)

    def fwd(blk):
        return pl.BlockSpec((None, None, rows, GROUP_W), lambda b, c: (blk, b, c, 0))

    def bwd(blk):
        return pl.BlockSpec((None, None, rows, GROUP_W), lambda b, c: (blk, b, nblk - 1 - c, 0))

    hf, hb = pl.pallas_call(
        _mlstm_kernel,
        grid=(B, nblk),
        in_specs=[
            fwd(BLK_MQ), fwd(BLK_MK), fwd(BLK_MV),
            pl.BlockSpec((None, rows, GATE_W), lambda b, c: (b, c, 0)),
            bwd(BLK_MQ), bwd(BLK_MK), bwd(BLK_MV),
            pl.BlockSpec((None, rows, GATE_W), lambda b, c: (b, nblk - 1 - c, 0)),
            pl.BlockSpec((1, HEAD_DIM), lambda b, c: (0, 0)),
            pl.BlockSpec((1, HEAD_DIM), lambda b, c: (0, 0)),
        ],
        out_specs=[
            pl.BlockSpec((None, rows, GROUP_W), lambda b, c: (b, c, 0)),
            pl.BlockSpec((None, rows, GROUP_W), lambda b, c: (b, nblk - 1 - c, 0)),
        ],
        out_shape=[jax.ShapeDtypeStruct((B, S, GROUP_W), BF16)] * 2,
        scratch_shapes=[
            pltpu.VMEM((2 * MLSTM_HEADS, HEAD_DIM, HEAD_DIM), F32),
            pltpu.VMEM((2 * MLSTM_HEADS, HEAD_DIM), F32),
            pltpu.VMEM((8, HEAD_DIM), F32),
        ],
        compiler_params=_params(("parallel", "arbitrary")),
        name="mlstm",
    )(u4, u4, u4, g3, u4, u4, u4, g3, ib, fb)
    return hf.reshape(T, GROUP_W), hb.reshape(T, GROUP_W)


def _rope_tables(S):
    rows = S // GRID_W
    row = jnp.repeat(jnp.arange(rows), GRID_W).astype(F32)
    colp = jnp.tile(jnp.arange(GRID_W), rows).astype(F32)
    inv = ROPE_THETA ** (-jnp.arange(ROPE_FREQS, dtype=F32) / ROPE_FREQS)
    ar = row[:, None] * inv
    ac = colp[:, None] * inv
    ang = jnp.concatenate([ar, ar, ac, ac], axis=1)
    first_half = (jnp.arange(HEAD_DIM) % (2 * ROPE_FREQS)) < ROPE_FREQS
    return jnp.cos(ang), jnp.where(first_half, -1.0, 1.0) * jnp.sin(ang)


def _rope(x, cos, sin_signed, first_half):
    partner = jnp.where(first_half, pltpu.roll(x, HEAD_DIM - ROPE_FREQS, axis=1), pltpu.roll(x, ROPE_FREQS, axis=1))
    return x * cos + partner * sin_signed


def _attn_prep_kernel(aq_ref, akv_ref, cos_ref, sin_ref, qg_ref, kg_ref, q_ref, kt_ref, va_ref, kn_ref, qn_ref):
    cos = cos_ref[...]
    sin = sin_ref[...]
    lane = lax.broadcasted_iota(jnp.int32, cos.shape, 1)
    first_half = (lane % (2 * ROPE_FREQS)) < ROPE_FREQS
    qg = qg_ref[...]
    kg = kg_ref[...]
    for h in range(4):
        lanes = slice(h * HEAD_DIM, (h + 1) * HEAD_DIM)
        qh = _head_norm(aq_ref[:, lanes].astype(F32)) * qg
        qb = (_rope(qh, cos, sin, first_half) * (QK_SCALE * LOG2_E)).astype(BF16)
        q_ref[:, lanes] = qb
        q2 = jnp.max(jnp.sum(jnp.square(qb.astype(F32)), axis=-1, keepdims=True), axis=0, keepdims=True)
        qn_ref[h] = jnp.broadcast_to(q2, qn_ref.shape[1:])
    for h in range(ATTN_KV_HEADS):
        lanes = slice(h * HEAD_DIM, (h + 1) * HEAD_DIM)
        kh = _head_norm(akv_ref[:, lanes].astype(F32)) * kg
        kb = _rope(kh, cos, sin, first_half).astype(BF16)
        kt_ref[h] = kb.astype(F32).T.astype(BF16)
        k2 = jnp.max(jnp.sum(jnp.square(kb.astype(F32)), axis=-1, keepdims=True), axis=0, keepdims=True)
        kn_ref[h] = jnp.broadcast_to(k2, kn_ref.shape[1:])
        vl = slice((ATTN_KV_HEADS + h) * HEAD_DIM, (ATTN_KV_HEADS + h + 1) * HEAD_DIM)
        va_ref[h, :, :HEAD_DIM] = akv_ref[:, vl]
        va_ref[h, :, HEAD_DIM:] = jnp.ones((akv_ref.shape[0], HEAD_DIM), BF16)


def _flash_kernel(q_ref, kt_ref, va_ref, kn_ref, qn_ref, o_ref, m_ref, acc_ref, p_ref, mp_ref):
    tq = q_ref.shape[0]
    nkc, _, tk = kt_ref.shape
    n_rep = tk // HEAD_DIM
    q2 = jnp.concatenate([q_ref[:, :HEAD_DIM], q_ref[:, HEAD_DIM:]], axis=0)

    def chunk(c):
        return kt_ref[c], va_ref[pl.ds(pl.multiple_of(c * tk, tk), tk), :]

    def lane_max(s):
        part = s[:, :HEAD_DIM]
        for t in range(1, n_rep):
            part = jnp.maximum(part, s[:, t * HEAD_DIM:(t + 1) * HEAD_DIM])
        return part

    q_sq = jnp.max(jnp.max(qn_ref[...], axis=0), axis=0, keepdims=True)
    k_sq = jnp.max(jnp.max(kn_ref[...], axis=0), axis=0, keepdims=True)
    safe = jnp.max(jnp.sqrt(q_sq * k_sq)) <= FRAME_MARGIN
    acc_ref[...] = jnp.zeros_like(acc_ref)

    def fast_scores(c, slot):
        kt, _ = chunk(c)
        s = jnp.dot(q2, kt, preferred_element_type=F32)
        p_ref[slot] = jnp.exp2(s - jnp.tile(m_ref[...], (1, n_rep))).astype(BF16)
        mp_ref[slot] = lane_max(s)

    def fast_accumulate(c, slot):
        _, vc = chunk(c)
        m_prev = m_ref[...]
        m_new = jnp.maximum(m_prev, jnp.max(mp_ref[slot], axis=-1, keepdims=True))
        alpha = jnp.exp2(m_prev - m_new)
        acc_ref[...] = (acc_ref[...] + jnp.dot(p_ref[slot], vc, preferred_element_type=F32)) * jnp.tile(alpha, (1, 2))
        m_ref[...] = m_new

    def fast():
        m_ref[...] = jnp.zeros_like(m_ref)

        group = 8 if nkc % 8 == 0 else 2

        def body(i, carry):
            for t in range(group):
                fast_scores(group * i + t, t % 2)
                fast_accumulate(group * i + t, t % 2)
            return carry

        lax.fori_loop(0, nkc // group, body, 0)

    def exact():
        m_ref[...] = jnp.full_like(m_ref, -jnp.inf)

        def body(c, carry):
            kt, vc = chunk(c)
            s = jnp.dot(q2, kt, preferred_element_type=F32)
            m_prev = m_ref[...]
            m_new = jnp.maximum(m_prev, jnp.max(lane_max(s), axis=-1, keepdims=True))
            p = jnp.exp2(s - jnp.tile(m_new, (1, n_rep))).astype(BF16)
            acc_ref[...] = (jnp.tile(jnp.exp2(m_prev - m_new), (1, 2)) * acc_ref[...]
                            + jnp.dot(p, vc, preferred_element_type=F32))
            m_ref[...] = m_new
            return carry

        lax.fori_loop(0, nkc, body, 0)

    lax.cond(safe, fast, exact)
    o = acc_ref[:, :HEAD_DIM] / acc_ref[:, HEAD_DIM:]
    o_ref[:, :HEAD_DIM] = o[:tq].astype(o_ref.dtype)
    o_ref[:, HEAD_DIM:] = o[tq:].astype(o_ref.dtype)


def _axial_gqa(u3, q_gain, k_gain, rope, B, S, tq=512, tk=1024):
    T = B * S
    cos, sin = rope
    u4 = u3.reshape(N_COLBLK, B, S, GROUP_W)
    tk = min(tk, S // 2)
    nkc = S // tk
    assert nkc >= 2 and nkc % 2 == 0, "flash kernel pipelines kv chunks in pairs"
    q_r, kt, va, kn, qn = pl.pallas_call(
        _attn_prep_kernel,
        grid=(B, nkc),
        in_specs=[
            pl.BlockSpec((None, None, tk, GROUP_W), lambda b, i: (BLK_AQ, b, i, 0)),
            pl.BlockSpec((None, None, tk, GROUP_W), lambda b, i: (BLK_AKV, b, i, 0)),
            pl.BlockSpec((tk, HEAD_DIM), lambda b, i: (i, 0)),
            pl.BlockSpec((tk, HEAD_DIM), lambda b, i: (i, 0)),
            pl.BlockSpec((1, HEAD_DIM), lambda b, i: (0, 0)),
            pl.BlockSpec((1, HEAD_DIM), lambda b, i: (0, 0)),
        ],
        out_specs=[
            pl.BlockSpec((None, tk, GROUP_W), lambda b, i: (b, i, 0)),
            pl.BlockSpec((None, ATTN_KV_HEADS, None, HEAD_DIM, tk), lambda b, i: (b, 0, i, 0, 0)),
            pl.BlockSpec((None, ATTN_KV_HEADS, tk, 2 * HEAD_DIM), lambda b, i: (b, 0, i, 0)),
            pl.BlockSpec((None, ATTN_KV_HEADS, None, 8, HEAD_DIM), lambda b, i: (b, 0, i, 0, 0)),
            pl.BlockSpec((None, 4, None, 8, HEAD_DIM), lambda b, i: (b, 0, i, 0, 0)),
        ],
        out_shape=[
            jax.ShapeDtypeStruct((B, S, GROUP_W), BF16),
            jax.ShapeDtypeStruct((B, ATTN_KV_HEADS, nkc, HEAD_DIM, tk), BF16),
            jax.ShapeDtypeStruct((B, ATTN_KV_HEADS, S, 2 * HEAD_DIM), BF16),
            jax.ShapeDtypeStruct((B, ATTN_KV_HEADS, nkc, 8, HEAD_DIM), F32),
            jax.ShapeDtypeStruct((B, 4, nkc, 8, HEAD_DIM), F32),
        ],
        compiler_params=_params(("parallel", "parallel")),
        name="attn_prep",
    )(u4, u4, cos, sin, q_gain, k_gain)
    out = pl.pallas_call(
        _flash_kernel,
        grid=(B, ATTN_KV_HEADS, S // tq),
        in_specs=[
            pl.BlockSpec((None, tq, 2 * HEAD_DIM), lambda b, k, i: (b, i, k)),
            pl.BlockSpec((None, None, nkc, HEAD_DIM, tk), lambda b, k, i: (b, k, 0, 0, 0)),
            pl.BlockSpec((None, None, S, 2 * HEAD_DIM), lambda b, k, i: (b, k, 0, 0)),
            pl.BlockSpec((None, None, nkc, 8, HEAD_DIM), lambda b, k, i: (b, k, 0, 0, 0)),
            pl.BlockSpec((None, 2, None, 8, HEAD_DIM), lambda b, k, i: (b, k, (i * tq) // tk, 0, 0)),
        ],
        out_specs=pl.BlockSpec((None, tq, 2 * HEAD_DIM), lambda b, k, i: (b, i, k)),
        out_shape=jax.ShapeDtypeStruct((B, S, GROUP_W), BF16),
        scratch_shapes=[
            pltpu.VMEM((2 * tq, HEAD_DIM), F32),
            pltpu.VMEM((2 * tq, 2 * HEAD_DIM), F32),
            pltpu.VMEM((2, 2 * tq, tk), BF16),
            pltpu.VMEM((2, 2 * tq, HEAD_DIM), F32),
        ],
        compiler_params=_params(("parallel", "parallel", "arbitrary")),
        name="flash_gqa",
    )(q_r, kt, va, kn, qn)
    return out.reshape(T, GROUP_W)


def _combine_kernel(x_ref, ya_ref, cb_ref, cc_ref, cx_ref, ccp_ref, cxp_ref, ccn_ref, cxn_ref, mo_ref,
                    hf_ref, hb_ref, yd_ref, cw_ref, cbias_ref, gain_ref, w_ref, o_ref, y_ref, *, tiles_per_seq):
    tm = x_ref.shape[0]
    i = pl.program_id(0)
    halo = ccp_ref.shape[0]
    at_start = (i % tiles_per_seq) == 0
    at_end = (i % tiles_per_seq) == tiles_per_seq - 1
    gain = gain_ref[...]

    z = cc_ref[...].astype(F32) * cx_ref[...].astype(F32)
    z_before = ccp_ref[halo - 1:halo, :].astype(F32) * cxp_ref[halo - 1:halo, :].astype(F32)
    z_after = ccn_ref[0:1, :].astype(F32) * cxn_ref[0:1, :].astype(F32)
    z_before = jnp.where(at_start, 0.0, z_before)
    z_after = jnp.where(at_end, 0.0, z_after)
    rid = lax.broadcasted_iota(jnp.int32, z.shape, 0)
    z_m1 = jnp.where(rid == 0, z_before, pltpu.roll(z, 1, axis=0))
    z_p1 = jnp.where(rid == tm - 1, z_after, pltpu.roll(z, tm - 1, axis=0))
    conv = z_m1 * cw_ref[0:1, :] + z * cw_ref[1:2, :] + z_p1 * cw_ref[2:3, :] + cbias_ref[...]
    yb = cb_ref[...].astype(F32) * conv

    for grp in range(4):
        lanes = slice(grp * HEAD_DIM, (grp + 1) * HEAD_DIM)
        y_ref[:, lanes] = (ya_ref[:, lanes].astype(F32) * gain[:, lanes]).astype(BF16)
        gl = slice(GROUP_W + grp * HEAD_DIM, GROUP_W + (grp + 1) * HEAD_DIM)
        y_ref[:, gl] = (_head_norm(yb[:, lanes]) * gain[:, gl]).astype(BF16)
        gl = slice(2 * GROUP_W + grp * HEAD_DIM, 2 * GROUP_W + (grp + 1) * HEAD_DIM)
        yc = _head_norm(hf_ref[:, lanes].astype(F32) + hb_ref[:, lanes].astype(F32))
        y_ref[:, gl] = (jax.nn.sigmoid(mo_ref[:, lanes].astype(F32)) * yc * gain[:, gl]).astype(BF16)
        gl = slice(3 * GROUP_W + grp * HEAD_DIM, 3 * GROUP_W + (grp + 1) * HEAD_DIM)
        y_ref[:, gl] = (_head_norm(yd_ref[:, lanes].astype(F32)) * gain[:, gl]).astype(BF16)

    o_ref[...] = x_ref[...] + jnp.dot(y_ref[...], w_ref[...], preferred_element_type=F32)


def _combine_out(x2, ya, u3, hf, hb, yd, conv_w, conv_b, out_gain, w_out, S, tm=512, halo=16):
    T = x2.shape[0]
    hb_per_tile = tm // halo
    n_halo = T // halo

    def u_blk(blk):
        return pl.BlockSpec((None, tm, GROUP_W), lambda i: (blk, i, 0))

    def u_prev(blk):
        return pl.BlockSpec((None, halo, GROUP_W), lambda i: (blk, jnp.maximum(i * hb_per_tile - 1, 0), 0))

    def u_next(blk):
        return pl.BlockSpec((None, halo, GROUP_W), lambda i: (blk, jnp.minimum((i + 1) * hb_per_tile, n_halo - 1), 0))

    tok = pl.BlockSpec((tm, GROUP_W), lambda i: (i, 0))
    return pl.pallas_call(
        functools.partial(_combine_kernel, tiles_per_seq=S // tm),
        grid=(T // tm,),
        in_specs=[
            pl.BlockSpec((tm, D_MODEL), lambda i: (i, 0)),
            tok,
            u_blk(BLK_CB), u_blk(BLK_CC), u_blk(BLK_CX),
            u_prev(BLK_CC), u_prev(BLK_CX), u_next(BLK_CC), u_next(BLK_CX),
            u_blk(BLK_MO),
            tok, tok, tok,
            pl.BlockSpec((8, GROUP_W), lambda i: (0, 0)),
            pl.BlockSpec((1, GROUP_W), lambda i: (0, 0)),
            pl.BlockSpec((1, D_MODEL), lambda i: (0, 0)),
            pl.BlockSpec((D_MODEL, D_MODEL), lambda i: (0, 0)),
        ],
        out_specs=pl.BlockSpec((tm, D_MODEL), lambda i: (i, 0)),
        out_shape=jax.ShapeDtypeStruct((T, D_MODEL), F32),
        scratch_shapes=[pltpu.VMEM((tm, D_MODEL), BF16)],
        compiler_params=_params(("parallel",)),
        name="combine_out",
    )(x2, ya, u3, u3, u3, u3, u3, u3, u3, u3, hf, hb, yd, conv_w, conv_b, out_gain, w_out)


def _mem_kv_kernel(mem_ref, g_ref, w_ref, kg_ref, kt_ref, v_ref, h_ref):
    _norm_rows_to(mem_ref, g_ref, h_ref)
    kv = jnp.dot(h_ref[...], w_ref[...], preferred_element_type=F32)
    kg = kg_ref[...]
    for h in range(CA_HEADS):
        lanes = slice(h * HEAD_DIM, (h + 1) * HEAD_DIM)
        kt_ref[h] = (_head_norm(kv[:, lanes]) * kg).T.astype(BF16)
    v_ref[...] = kv[:, CA_HEADS * HEAD_DIM:].astype(BF16)


def _mem_kv(mem, g_mem, w_kv, k_gain):
    B = mem.shape[0]
    return pl.pallas_call(
        _mem_kv_kernel,
        grid=(B,),
        in_specs=[
            pl.BlockSpec((None, N_MEM, D_MODEL), lambda b: (b, 0, 0)),
            pl.BlockSpec((1, D_MODEL), lambda b: (0, 0)),
            pl.BlockSpec((D_MODEL, 2 * GROUP_W), lambda b: (0, 0)),
            pl.BlockSpec((1, HEAD_DIM), lambda b: (0, 0)),
        ],
        out_specs=[
            pl.BlockSpec((None, CA_HEADS, HEAD_DIM, N_MEM), lambda b: (b, 0, 0, 0)),
            pl.BlockSpec((None, N_MEM, GROUP_W), lambda b: (b, 0, 0)),
        ],
        out_shape=[
            jax.ShapeDtypeStruct((B, CA_HEADS, HEAD_DIM, N_MEM), BF16),
            jax.ShapeDtypeStruct((B, N_MEM, GROUP_W), BF16),
        ],
        scratch_shapes=[pltpu.VMEM((N_MEM, D_MODEL), BF16)],
        compiler_params=_params(("parallel",)),
        name="mem_kv",
    )(mem, g_mem, w_kv, k_gain)


def _cross_attn_kernel(x_ref, g_ref, wq_ref, qg_ref, kt_ref, v_ref, wo_ref, o_ref, h_ref, oc_ref):
    _norm_rows_to(x_ref, g_ref, h_ref)
    q = jnp.dot(h_ref[...], wq_ref[...], preferred_element_type=F32)
    qg = qg_ref[...]
    for h in range(CA_HEADS):
        lanes = slice(h * HEAD_DIM, (h + 1) * HEAD_DIM)
        qh = (_head_norm(q[:, lanes]) * (qg * QK_SCALE)).astype(BF16)
        s = jnp.dot(qh, kt_ref[h], preferred_element_type=F32)
        e = jnp.exp(s - jnp.max(s, axis=-1, keepdims=True))
        o = jnp.dot(e.astype(BF16), v_ref[:, lanes], preferred_element_type=F32)
        oc_ref[:, lanes] = (o * pl.reciprocal(jnp.sum(e, axis=-1, keepdims=True), approx=False)).astype(BF16)
    o_ref[...] = x_ref[...] + jnp.dot(oc_ref[...], wo_ref[...], preferred_element_type=F32)


def _cross_attn(x2, g_ca, w_q, q_gain, kt, v, w_o, B, S, tm=1024):
    T = x2.shape[0]
    x3 = x2.reshape(B, S, D_MODEL)
    out = pl.pallas_call(
        _cross_attn_kernel,
        grid=(B, S // tm),
        in_specs=[
            pl.BlockSpec((None, tm, D_MODEL), lambda b, i: (b, i, 0)),
            pl.BlockSpec((1, D_MODEL), lambda b, i: (0, 0)),
            pl.BlockSpec((D_MODEL, GROUP_W), lambda b, i: (0, 0)),
            pl.BlockSpec((1, HEAD_DIM), lambda b, i: (0, 0)),
            pl.BlockSpec((None, CA_HEADS, HEAD_DIM, N_MEM), lambda b, i: (b, 0, 0, 0)),
            pl.BlockSpec((None, N_MEM, GROUP_W), lambda b, i: (b, 0, 0)),
            pl.BlockSpec((GROUP_W, D_MODEL), lambda b, i: (0, 0)),
        ],
        out_specs=pl.BlockSpec((None, tm, D_MODEL), lambda b, i: (b, i, 0)),
        out_shape=jax.ShapeDtypeStruct((B, S, D_MODEL), F32),
        scratch_shapes=[pltpu.VMEM((tm, D_MODEL), BF16), pltpu.VMEM((tm, GROUP_W), BF16)],
        compiler_params=_params(("parallel", "parallel")),
        name="cross_attn",
    )(x3, g_ca, w_q, q_gain, kt, v, w_o)
    return out.reshape(T, D_MODEL)


def _mlp_kernel(x_ref, g_ref, wu_ref, wd_ref, o_ref, h_ref):
    @pl.when(pl.program_id(1) == 0)
    def _():
        _norm_rows_to(x_ref, g_ref, h_ref)
        o_ref[...] = x_ref[...]

    up = jnp.dot(h_ref[...], wu_ref[...], preferred_element_type=F32)
    act = jnp.square(jnp.maximum(up, 0.0)).astype(BF16)
    o_ref[...] += jnp.dot(act, wd_ref[...], preferred_element_type=F32)


def _mlp(x2, g, w_up, w_down, tm=512, tf=1024):
    T = x2.shape[0]
    return pl.pallas_call(
        _mlp_kernel,
        grid=(T // tm, D_FF // tf),
        in_specs=[
            pl.BlockSpec((tm, D_MODEL), lambda i, j: (i, 0)),
            pl.BlockSpec((1, D_MODEL), lambda i, j: (0, 0)),
            pl.BlockSpec((D_MODEL, tf), lambda i, j: (0, j)),
            pl.BlockSpec((tf, D_MODEL), lambda i, j: (j, 0)),
        ],
        out_specs=pl.BlockSpec((tm, D_MODEL), lambda i, j: (i, 0)),
        out_shape=jax.ShapeDtypeStruct((T, D_MODEL), F32),
        scratch_shapes=[pltpu.VMEM((tm, D_MODEL), BF16)],
        compiler_params=_params(("parallel", "arbitrary")),
        name="mlp",
    )(x2, g, w_up, w_down)


def _prep_weights(w_in, conv_w, i_bias, f_bias, w_out, w_ca_q, w_ca_kv, w_ca_o, w_up, w_down):
    n_main = 8 * GROUP_W
    w_main = jnp.concatenate([w_in[:, :, :n_main], w_in[:, :, n_main + 16:]], axis=2).astype(BF16)
    pad = jnp.zeros((DEPTH, D_MODEL, HEAD_DIM - 8), F32)
    w_gate = jnp.concatenate([w_in[:, :, n_main:n_main + 8], pad, w_in[:, :, n_main + 8:n_main + 16], pad],
                             axis=2).astype(BF16)
    lane_pad = jnp.zeros((DEPTH, 1, HEAD_DIM - 8), F32)
    ib = jnp.concatenate([i_bias.reshape(DEPTH, 1, 8), lane_pad], axis=2)
    fb = jnp.concatenate([f_bias.reshape(DEPTH, 1, 8), lane_pad], axis=2)
    cw = jnp.concatenate([conv_w, jnp.zeros((DEPTH, 5, GROUP_W), F32)], axis=1)
    return dict(w_main=w_main, w_gate=w_gate, ib=ib, fb=fb, cw=cw, w_out=w_out.astype(BF16),
                w_ca_q=w_ca_q.astype(BF16), w_ca_kv=w_ca_kv.astype(BF16), w_ca_o=w_ca_o.astype(BF16),
                w_up=w_up.astype(BF16), w_down=w_down.astype(BF16))


def _trunk(x, mem, pw, g_mix, conv_b, attn_q_norm, attn_k_norm, out_gain, g_ca, g_mem, ca_q_norm, ca_k_norm, g_mlp):
    B, S, _ = x.shape
    T = B * S
    x2 = x.reshape(T, D_MODEL)
    rope = _rope_tables(S)
    ftab = _fourier_tables(S)
    for l in range(DEPTH):
        u3, gates, a_in = _norm_proj(x2, g_mix[l][None], pw["w_main"][l], pw["w_gate"][l])
        ya = _fourier(a_in, B, S, ftab)
        hf, hb = _mlstm(u3, gates, pw["ib"][l], pw["fb"][l], B, S)
        yd = _axial_gqa(u3, attn_q_norm[l][None], attn_k_norm[l][None], rope, B, S)
        x2 = _combine_out(x2, ya, u3, hf, hb, yd, pw["cw"][l], conv_b[l][None], out_gain[l][None], pw["w_out"][l], S)
        kt, v = _mem_kv(mem, g_mem[l][None], pw["w_ca_kv"][l], ca_k_norm[l][None])
        x2 = _cross_attn(x2, g_ca[l][None], pw["w_ca_q"][l], ca_q_norm[l][None], kt, v, pw["w_ca_o"][l], B, S)
        x2 = _mlp(x2, g_mlp[l][None], pw["w_up"][l], pw["w_down"][l])
    return x2.reshape(B, S, D_MODEL)


def kernel(x_prompt, x_sample, mem_prompt, mem_sample, g_mix, w_in, conv_w, conv_b, i_bias, f_bias, attn_q_norm,
           attn_k_norm, out_gain, w_out, g_ca, g_mem, w_ca_q, w_ca_kv, ca_q_norm, ca_k_norm, w_ca_o, g_mlp, w_up,
           w_down):
    pw = _prep_weights(w_in, conv_w, i_bias, f_bias, w_out, w_ca_q, w_ca_kv, w_ca_o, w_up, w_down)
    args = (pw, g_mix, conv_b, attn_q_norm, attn_k_norm, out_gain, g_ca, g_mem, ca_q_norm, ca_k_norm, g_mlp)
    y_prompt = _trunk(x_prompt, mem_prompt, *args)
    y_sample = _trunk(x_sample, mem_sample, *args)
    return (y_prompt, y_sample)
```

```python
import functools
import math

import jax
import jax.numpy as jnp
from jax import lax
from jax.experimental import pallas as pl
from jax.experimental.pallas import tpu as pltpu

F32 = jnp.float32
BF16 = jnp.bfloat16

D_MODEL = 2048
DEPTH = 4
HEAD_DIM = 128
GROUP_W = 4 * HEAD_DIM
N_COLBLK = 10
GATE_W = 2 * HEAD_DIM
MLSTM_HEADS = 4
MLSTM_CHUNK = 128
ATTN_Q_HEADS = 4
ATTN_KV_HEADS = 2
GRID_W = 64
ROPE_THETA = 10000.0
ROPE_FREQS = HEAD_DIM // 4
CA_HEADS = 4
N_MEM = 256
D_FF = 4 * D_MODEL
RMS_EPS = 1e-6
QK_SCALE = HEAD_DIM ** -0.5
LOG2_E = math.log2(math.e)
FRAME_MARGIN = 64.0

BLK_FOURIER, BLK_CB, BLK_CC, BLK_CX, BLK_MQ, BLK_MK, BLK_MV, BLK_MO, BLK_AQ, BLK_AKV = range(N_COLBLK)

V7X_VMEM_LIMIT = 56 * 1024 * 1024
V7X_F32_SUBLANES = 8
V7X_BF16_SUBLANES = 16
NORM_ROWS = 256


def _params(semantics, vmem=V7X_VMEM_LIMIT):
    return pltpu.CompilerParams(dimension_semantics=semantics, vmem_limit_bytes=vmem)


def _head_norm(y):
    return y * lax.rsqrt(jnp.mean(y * y, axis=-1, keepdims=True) + RMS_EPS)


def _norm_rows_to(x_ref, g_ref, h_ref):
    tm = x_ref.shape[0]
    g = g_ref[...]

    def body(r, carry):
        rows = pl.ds(pl.multiple_of(r * NORM_ROWS, NORM_ROWS), NORM_ROWS)
        xf = x_ref[rows, :]
        ms = jnp.mean(xf * xf, axis=-1, keepdims=True)
        h_ref[rows, :] = (xf * lax.rsqrt(ms + RMS_EPS) * g).astype(BF16)
        return carry

    lax.fori_loop(0, tm // NORM_ROWS, body, 0)


def _norm_proj_kernel(x_ref, g_ref, w_ref, wg_ref, u_ref, gate_ref, h_ref):
    @pl.when(pl.program_id(1) == 0)
    def _():
        _norm_rows_to(x_ref, g_ref, h_ref)
        gate_ref[...] = jnp.dot(h_ref[...], wg_ref[...], preferred_element_type=F32)

    res = jnp.dot(h_ref[...], w_ref[...], preferred_element_type=F32)
    for k in range(u_ref.shape[0]):
        u_ref[k] = res[:, k * GROUP_W:(k + 1) * GROUP_W].astype(BF16)


def _norm_proj(x2, g, w_main, w_gate, tm=1024, nb=2):
    T = x2.shape[0]
    tn = nb * GROUP_W
    return pl.pallas_call(
        _norm_proj_kernel,
        grid=(T // tm, N_COLBLK // nb),
        in_specs=[
            pl.BlockSpec((tm, D_MODEL), lambda i, j: (i, 0)),
            pl.BlockSpec((1, D_MODEL), lambda i, j: (0, 0)),
            pl.BlockSpec((D_MODEL, tn), lambda i, j: (0, j)),
            pl.BlockSpec((D_MODEL, GATE_W), lambda i, j: (0, 0)),
        ],
        out_specs=[
            pl.BlockSpec((nb, tm, GROUP_W), lambda i, j: (j, i, 0)),
            pl.BlockSpec((tm, GATE_W), lambda i, j: (i, 0)),
        ],
        out_shape=[
            jax.ShapeDtypeStruct((N_COLBLK, T, GROUP_W), BF16),
            jax.ShapeDtypeStruct((T, GATE_W), F32),
        ],
        scratch_shapes=[pltpu.VMEM((tm, D_MODEL), BF16)],
        compiler_params=_params(("parallel", "arbitrary")),
        name="norm_proj",
    )(x2, g, w_main, w_gate)


def _fourier_tables(S):
    n2 = HEAD_DIM
    n1 = S // n2
    i1 = jnp.arange(n1, dtype=jnp.int32)
    a1 = (2.0 * math.pi / n1) * ((i1[:, None] * i1[None, :]) % n1).astype(F32)
    f1 = jnp.concatenate([jnp.cos(a1), -jnp.sin(a1)], axis=0).astype(BF16)
    i2 = jnp.arange(n2, dtype=jnp.int32)
    p = i1[:, None, None] + n1 * i2[None, :, None]
    k = (p * i2[None, None, :]) % S
    a2 = (2.0 * math.pi / S) * k.astype(F32)
    c2, s2 = jnp.cos(a2), jnp.sin(a2)
    g = jnp.concatenate([jnp.concatenate([c2, s2], axis=2),
                         jnp.concatenate([-s2, c2], axis=2)], axis=1).astype(BF16)
    ac = (2.0 * math.pi / HEAD_DIM) * ((i2[:, None] * i2[None, :]) % HEAD_DIM).astype(F32)
    cs = jnp.concatenate([jnp.cos(ac), jnp.sin(ac)], axis=0).astype(BF16)
    return f1, g, cs


def _fourier_stage1_kernel(a_ref, f1_ref, y_ref):
    n1 = a_ref.shape[0]
    res = jnp.dot(f1_ref[...], a_ref[...], preferred_element_type=F32)
    y_ref[0] = res[:n1].astype(BF16)
    y_ref[1] = res[n1:].astype(BF16)


def _fourier_stage2_kernel(y_ref, g_ref, cs_ref, o_ref, *, scale):
    n_p = y_ref.shape[1]
    cs = cs_ref[...]
    for p in range(n_p):
        ycat = jnp.concatenate([y_ref[0, p], y_ref[1, p]], axis=0)
        x = jnp.dot(g_ref[p], ycat, preferred_element_type=F32).astype(BF16)
        for grp in range(GROUP_W // HEAD_DIM):
            lanes = slice(grp * HEAD_DIM, (grp + 1) * HEAD_DIM)
            xg = jnp.concatenate([x[:HEAD_DIM, lanes], x[HEAD_DIM:, lanes]], axis=1)
            o = jnp.dot(xg, cs, preferred_element_type=F32) * scale
            col = p * GROUP_W + grp * HEAD_DIM
            o_ref[:, col:col + HEAD_DIM] = _head_norm(o).astype(BF16)


def _fourier(u3, B, S, tables):
    f1, g, cs = tables
    n2 = HEAD_DIM
    n1 = S // n2
    T = B * S
    ncol = min(16, n2)
    a_view = u3[BLK_FOURIER].reshape(B, n1, n2 * GROUP_W)
    y = pl.pallas_call(
        _fourier_stage1_kernel,
        grid=(B, n2 // ncol),
        in_specs=[
            pl.BlockSpec((None, n1, ncol * GROUP_W), lambda b, j: (b, 0, j)),
            pl.BlockSpec((2 * n1, n1), lambda b, j: (0, 0)),
        ],
        out_specs=pl.BlockSpec((None, 2, n1, ncol * GROUP_W), lambda b, j: (b, 0, 0, j)),
        out_shape=jax.ShapeDtypeStruct((B, 2, n1, n2 * GROUP_W), BF16),
        compiler_params=_params(("parallel", "parallel")),
        name="fourier_stage1",
    )(a_view, f1)
    y5 = y.reshape(B, 2, n1, n2, GROUP_W)
    n_p = 8
    out = pl.pallas_call(
        functools.partial(_fourier_stage2_kernel, scale=1.0 / math.sqrt(S * HEAD_DIM)),
        grid=(B, n1 // n_p),
        in_specs=[
            pl.BlockSpec((None, 2, n_p, n2, GROUP_W), lambda b, j: (b, 0, j, 0, 0)),
            pl.BlockSpec((n_p, 2 * n2, 2 * n2), lambda b, j: (j, 0, 0)),
            pl.BlockSpec((2 * HEAD_DIM, HEAD_DIM), lambda b, j: (0, 0)),
        ],
        out_specs=pl.BlockSpec((None, n2, n_p * GROUP_W), lambda b, j: (b, 0, j)),
        out_shape=jax.ShapeDtypeStruct((B, n2, n1 * GROUP_W), BF16),
        compiler_params=_params(("parallel", "parallel")),
        name="fourier_stage2",
    )(y5, g, cs)
    return out.reshape(T, GROUP_W)


def _log_sigmoid(x):
    return jnp.minimum(x, 0.0) - jnp.log1p(jnp.exp(-jnp.abs(x)))


def _split3(x):
    hi = x.astype(BF16)
    r1 = x - hi.astype(F32)
    mid = r1.astype(BF16)
    return hi, mid, (r1 - mid.astype(F32)).astype(BF16)


def _mlstm_chunk(d, q, k, v, gi, gf, ib, fb, tri, mask, c_ref, n_ref, m_ref, out_ref, rows):
    li = gi + ib
    lf = _log_sigmoid(gf + fb)
    b = sum(jnp.dot(tri, part, preferred_element_type=F32) for part in _split3(lf))
    g = jnp.sum(lf, axis=0, keepdims=True)
    w_end = g - b + li
    m_loc = jnp.max(w_end, axis=0, keepdims=True)
    e_t = jnp.exp(w_end - m_loc).T * QK_SCALE
    e_parts = _split3(e_t[:V7X_BF16_SUBLANES, :])
    a_t = (li - b).T
    m_old = m_ref[d:d + 1, :]
    m_new = jnp.maximum(g + m_old, m_loc)
    a_sc = jnp.exp(g + m_old - m_new)
    b_sc = jnp.exp(m_loc - m_new)
    lane = lax.broadcasted_iota(jnp.int32, (1, HEAD_DIM), 1)

    def pick(x, hp):
        return jnp.sum(jnp.where(lane == hp, x, 0.0), axis=-1, keepdims=True)

    for h in range(MLSTM_HEADS):
        hp = MLSTM_HEADS * d + h
        lanes = slice(h * HEAD_DIM, (h + 1) * HEAD_DIM)
        qh = q[:, lanes]
        vh = v[:, lanes]
        kh = k[:, lanes]
        bc = pick(b, hp)
        dm = jnp.where(mask, bc + a_t[hp:hp + 1, :], -jnp.inf)
        inter_log = bc + pick(m_old, hp)
        m_t = jnp.maximum(inter_log, jnp.max(dm, axis=-1, keepdims=True))
        p = jnp.exp(dm - m_t)
        inter_w = jnp.exp(inter_log - m_t)
        qk = lax.dot_general(qh, kh, (((1,), (1,)), ((), ())), preferred_element_type=F32)
        a_mat = p * (qk * QK_SCALE)
        c_prev = c_ref[hp]
        n_prev = n_ref[hp:hp + 1, :]
        num = (jnp.dot(a_mat.astype(BF16), vh, preferred_element_type=F32)
               + inter_w * jnp.dot(qh, c_prev.astype(BF16), preferred_element_type=F32))
        den = (jnp.sum(a_mat, axis=-1, keepdims=True)
               + inter_w * jnp.sum(qh.astype(F32) * n_prev, axis=-1, keepdims=True))
        hout = num / jnp.maximum(jnp.abs(den), jnp.exp(-m_t))
        out_ref[rows, lanes] = hout.astype(out_ref.dtype)
        ke_t = kh.astype(F32).T * e_t[hp:hp + 1, :]
        c_loc = jnp.dot(ke_t.astype(BF16), vh, preferred_element_type=F32)
        n_loc = sum(jnp.dot(part, kh, preferred_element_type=F32) for part in e_parts)[hp:hp + 1, :]
        ah = pick(a_sc, hp)
        bh = pick(b_sc, hp)
        c_ref[hp] = ah * c_prev + bh * c_loc
        n_ref[hp:hp + 1, :] = ah * n_prev + bh * n_loc
    m_ref[d:d + 1, :] = m_new


def _mlstm_kernel(qf_ref, kf_ref, vf_ref, gf_ref, qb_ref, kb_ref, vb_ref, gb_ref, ib_ref, fb_ref,
                  hf_ref, hb_ref, c_ref, n_ref, m_ref):
    L = MLSTM_CHUNK
    n_sub = qf_ref.shape[0] // L

    @pl.when(pl.program_id(1) == 0)
    def _():
        c_ref[...] = jnp.zeros_like(c_ref)
        n_ref[...] = jnp.zeros_like(n_ref)
        m_ref[...] = jnp.zeros_like(m_ref)

    row = lax.broadcasted_iota(jnp.int32, (L, L), 0)
    col = lax.broadcasted_iota(jnp.int32, (L, L), 1)
    mask_f = col <= row
    mask_b = col >= row
    tri_f = jnp.where(mask_f, 1.0, 0.0).astype(BF16)
    tri_b = jnp.where(mask_b, 1.0, 0.0).astype(BF16)
    ib = ib_ref[...]
    fb = fb_ref[...]

    def body(j, carry):
        rf = pl.ds(pl.multiple_of(j * L, L), L)
        rb = pl.ds(pl.multiple_of((n_sub - 1 - j) * L, L), L)
        _mlstm_chunk(0, qf_ref[rf, :], kf_ref[rf, :], vf_ref[rf, :], gf_ref[rf, :HEAD_DIM], gf_ref[rf, HEAD_DIM:],
                     ib, fb, tri_f, mask_f, c_ref, n_ref, m_ref, hf_ref, rf)
        _mlstm_chunk(1, qb_ref[rb, :], kb_ref[rb, :], vb_ref[rb, :], gb_ref[rb, :HEAD_DIM], gb_ref[rb, HEAD_DIM:],
                     ib, fb, tri_b, mask_b, c_ref, n_ref, m_ref, hb_ref, rb)
        return carry

    lax.fori_loop(0, n_sub, body, 0)


def _mlstm(u3, gates, ib, fb, B, S, n_sub=4):
    T = B * S
    rows = n_sub * MLSTM_CHUNK
    nblk = S // rows
    u4 = u3.reshape(N_COLBLK, B, S, GROUP_W)
    g3 = gates.reshape(B, S, GATE_W)

    def fwd(blk):
        return pl.BlockSpec((None, None, rows, GROUP_W), lambda b, c: (blk, b, c, 0))

    def bwd(blk):
        return pl.BlockSpec((None, None, rows, GROUP_W), lambda b, c: (blk, b, nblk - 1 - c, 0))

    hf, hb = pl.pallas_call(
        _mlstm_kernel,
        grid=(B, nblk),
        in_specs=[
            fwd(BLK_MQ), fwd(BLK_MK), fwd(BLK_MV),
            pl.BlockSpec((None, rows, GATE_W), lambda b, c: (b, c, 0)),
            bwd(BLK_MQ), bwd(BLK_MK), bwd(BLK_MV),
            pl.BlockSpec((None, rows, GATE_W), lambda b, c: (b, nblk - 1 - c, 0)),
            pl.BlockSpec((1, HEAD_DIM), lambda b, c: (0, 0)),
            pl.BlockSpec((1, HEAD_DIM), lambda b, c: (0, 0)),
        ],
        out_specs=[
            pl.BlockSpec((None, rows, GROUP_W), lambda b, c: (b, c, 0)),
            pl.BlockSpec((None, rows, GROUP_W), lambda b, c: (b, nblk - 1 - c, 0)),
        ],
        out_shape=[jax.ShapeDtypeStruct((B, S, GROUP_W), BF16)] * 2,
        scratch_shapes=[
            pltpu.VMEM((2 * MLSTM_HEADS, HEAD_DIM, HEAD_DIM), F32),
            pltpu.VMEM((2 * MLSTM_HEADS, HEAD_DIM), F32),
            pltpu.VMEM((V7X_F32_SUBLANES, HEAD_DIM), F32),
        ],
        compiler_params=_params(("parallel", "arbitrary")),
        name="mlstm",
    )(u4, u4, u4, g3, u4, u4, u4, g3, ib, fb)
    return hf.reshape(T, GROUP_W), hb.reshape(T, GROUP_W)


def _rope_tables(S):
    rows = S // GRID_W
    row = jnp.repeat(jnp.arange(rows), GRID_W).astype(F32)
    colp = jnp.tile(jnp.arange(GRID_W), rows).astype(F32)
    inv = ROPE_THETA ** (-jnp.arange(ROPE_FREQS, dtype=F32) / ROPE_FREQS)
    ar = row[:, None] * inv
    ac = colp[:, None] * inv
    ang = jnp.concatenate([ar, ar, ac, ac], axis=1)
    first_half = (jnp.arange(HEAD_DIM) % (2 * ROPE_FREQS)) < ROPE_FREQS
    return jnp.cos(ang), jnp.where(first_half, -1.0, 1.0) * jnp.sin(ang)


def _rope(x, cos, sin_signed, first_half):
    partner = jnp.where(first_half, pltpu.roll(x, HEAD_DIM - ROPE_FREQS, axis=1), pltpu.roll(x, ROPE_FREQS, axis=1))
    return x * cos + partner * sin_signed


def _attn_prep_kernel(aq_ref, akv_ref, cos_ref, sin_ref, qg_ref, kg_ref, q_ref, kt_ref, va_ref, kn_ref, qn_ref):
    cos = cos_ref[...]
    sin = sin_ref[...]
    lane = lax.broadcasted_iota(jnp.int32, cos.shape, 1)
    first_half = (lane % (2 * ROPE_FREQS)) < ROPE_FREQS
    qg = qg_ref[...]
    kg = kg_ref[...]
    for h in range(ATTN_Q_HEADS):
        lanes = slice(h * HEAD_DIM, (h + 1) * HEAD_DIM)
        qh = _head_norm(aq_ref[:, lanes].astype(F32)) * qg
        qb = (_rope(qh, cos, sin, first_half) * (QK_SCALE * LOG2_E)).astype(BF16)
        q_ref[:, lanes] = qb
        q2 = jnp.max(jnp.sum(jnp.square(qb.astype(F32)), axis=-1, keepdims=True), axis=0, keepdims=True)
        qn_ref[h] = jnp.broadcast_to(q2, qn_ref.shape[1:])
    for h in range(ATTN_KV_HEADS):
        lanes = slice(h * HEAD_DIM, (h + 1) * HEAD_DIM)
        kh = _head_norm(akv_ref[:, lanes].astype(F32)) * kg
        kb = _rope(kh, cos, sin, first_half).astype(BF16)
        kt_ref[h] = kb.astype(F32).T.astype(BF16)
        k2 = jnp.max(jnp.sum(jnp.square(kb.astype(F32)), axis=-1, keepdims=True), axis=0, keepdims=True)
        kn_ref[h] = jnp.broadcast_to(k2, kn_ref.shape[1:])
        vl = slice((ATTN_KV_HEADS + h) * HEAD_DIM, (ATTN_KV_HEADS + h + 1) * HEAD_DIM)
        va_ref[h, :, :HEAD_DIM] = akv_ref[:, vl]
        va_ref[h, :, HEAD_DIM:] = jnp.ones((akv_ref.shape[0], HEAD_DIM), BF16)


def _flash_kernel(q_ref, kt_ref, va_ref, kn_ref, qn_ref, o_ref, m_ref, acc_ref, p_ref, mp_ref):
    tq = q_ref.shape[0]
    nkc, _, tk = kt_ref.shape
    n_rep = tk // HEAD_DIM
    q2 = jnp.concatenate([q_ref[:, :HEAD_DIM], q_ref[:, HEAD_DIM:]], axis=0)

    def chunk(c):
        return kt_ref[c], va_ref[pl.ds(pl.multiple_of(c * tk, tk), tk), :]

    def lane_max(s):
        part = s[:, :HEAD_DIM]
        for t in range(1, n_rep):
            part = jnp.maximum(part, s[:, t * HEAD_DIM:(t + 1) * HEAD_DIM])
        return part

    q_sq = jnp.max(jnp.max(qn_ref[...], axis=0), axis=0, keepdims=True)
    k_sq = jnp.max(jnp.max(kn_ref[...], axis=0), axis=0, keepdims=True)
    safe = jnp.max(jnp.sqrt(q_sq * k_sq)) <= FRAME_MARGIN
    acc_ref[...] = jnp.zeros_like(acc_ref)

    def fast_scores(c, slot):
        kt, _ = chunk(c)
        s = jnp.dot(q2, kt, preferred_element_type=F32)
        p_ref[slot] = jnp.exp2(s - jnp.tile(m_ref[...], (1, n_rep))).astype(BF16)
        mp_ref[slot] = lane_max(s)

    def fast_accumulate(c, slot):
        _, vc = chunk(c)
        m_prev = m_ref[...]
        m_new = jnp.maximum(m_prev, jnp.max(mp_ref[slot], axis=-1, keepdims=True))
        alpha = jnp.exp2(m_prev - m_new)
        acc_ref[...] = (acc_ref[...] + jnp.dot(p_ref[slot], vc, preferred_element_type=F32)) * jnp.tile(alpha, (1, 2))
        m_ref[...] = m_new

    def fast():
        m_ref[...] = jnp.zeros_like(m_ref)

        group = 8 if nkc % 8 == 0 else 2

        def body(i, carry):
            for t in range(group):
                fast_scores(group * i + t, t % 2)
                fast_accumulate(group * i + t, t % 2)
            return carry

        lax.fori_loop(0, nkc // group, body, 0)

    def exact():
        m_ref[...] = jnp.full_like(m_ref, -jnp.inf)

        def body(c, carry):
            kt, vc = chunk(c)
            s = jnp.dot(q2, kt, preferred_element_type=F32)
            m_prev = m_ref[...]
            m_new = jnp.maximum(m_prev, jnp.max(lane_max(s), axis=-1, keepdims=True))
            p = jnp.exp2(s - jnp.tile(m_new, (1, n_rep))).astype(BF16)
            acc_ref[...] = (jnp.tile(jnp.exp2(m_prev - m_new), (1, 2)) * acc_ref[...]
                            + jnp.dot(p, vc, preferred_element_type=F32))
            m_ref[...] = m_new
            return carry

        lax.fori_loop(0, nkc, body, 0)

    lax.cond(safe, fast, exact)
    o = acc_ref[:, :HEAD_DIM] / acc_ref[:, HEAD_DIM:]
    o_ref[:, :HEAD_DIM] = o[:tq].astype(o_ref.dtype)
    o_ref[:, HEAD_DIM:] = o[tq:].astype(o_ref.dtype)


def _axial_gqa(u3, q_gain, k_gain, rope, B, S, tq=512, tk=1024):
    T = B * S
    cos, sin = rope
    u4 = u3.reshape(N_COLBLK, B, S, GROUP_W)
    tk = min(tk, S // 2)
    nkc = S // tk
    assert nkc >= 2 and nkc % 2 == 0, "flash kernel pipelines kv chunks in pairs"
    q_r, kt, va, kn, qn = pl.pallas_call(
        _attn_prep_kernel,
        grid=(B, nkc),
        in_specs=[
            pl.BlockSpec((None, None, tk, GROUP_W), lambda b, i: (BLK_AQ, b, i, 0)),
            pl.BlockSpec((None, None, tk, GROUP_W), lambda b, i: (BLK_AKV, b, i, 0)),
            pl.BlockSpec((tk, HEAD_DIM), lambda b, i: (i, 0)),
            pl.BlockSpec((tk, HEAD_DIM), lambda b, i: (i, 0)),
            pl.BlockSpec((1, HEAD_DIM), lambda b, i: (0, 0)),
            pl.BlockSpec((1, HEAD_DIM), lambda b, i: (0, 0)),
        ],
        out_specs=[
            pl.BlockSpec((None, tk, GROUP_W), lambda b, i: (b, i, 0)),
            pl.BlockSpec((None, ATTN_KV_HEADS, None, HEAD_DIM, tk), lambda b, i: (b, 0, i, 0, 0)),
            pl.BlockSpec((None, ATTN_KV_HEADS, tk, 2 * HEAD_DIM), lambda b, i: (b, 0, i, 0)),
            pl.BlockSpec((None, ATTN_KV_HEADS, None, V7X_F32_SUBLANES, HEAD_DIM), lambda b, i: (b, 0, i, 0, 0)),
            pl.BlockSpec((None, ATTN_Q_HEADS, None, V7X_F32_SUBLANES, HEAD_DIM), lambda b, i: (b, 0, i, 0, 0)),
        ],
        out_shape=[
            jax.ShapeDtypeStruct((B, S, GROUP_W), BF16),
            jax.ShapeDtypeStruct((B, ATTN_KV_HEADS, nkc, HEAD_DIM, tk), BF16),
            jax.ShapeDtypeStruct((B, ATTN_KV_HEADS, S, 2 * HEAD_DIM), BF16),
            jax.ShapeDtypeStruct((B, ATTN_KV_HEADS, nkc, V7X_F32_SUBLANES, HEAD_DIM), F32),
            jax.ShapeDtypeStruct((B, ATTN_Q_HEADS, nkc, V7X_F32_SUBLANES, HEAD_DIM), F32),
        ],
        compiler_params=_params(("parallel", "parallel")),
        name="attn_prep",
    )(u4, u4, cos, sin, q_gain, k_gain)
    out = pl.pallas_call(
        _flash_kernel,
        grid=(B, ATTN_KV_HEADS, S // tq),
        in_specs=[
            pl.BlockSpec((None, tq, 2 * HEAD_DIM), lambda b, k, i: (b, i, k)),
            pl.BlockSpec((None, None, nkc, HEAD_DIM, tk), lambda b, k, i: (b, k, 0, 0, 0)),
            pl.BlockSpec((None, None, S, 2 * HEAD_DIM), lambda b, k, i: (b, k, 0, 0)),
            pl.BlockSpec((None, None, nkc, V7X_F32_SUBLANES, HEAD_DIM), lambda b, k, i: (b, k, 0, 0, 0)),
            pl.BlockSpec((None, ATTN_Q_HEADS // ATTN_KV_HEADS, None, V7X_F32_SUBLANES, HEAD_DIM),
                         lambda b, k, i: (b, k, (i * tq) // tk, 0, 0)),
        ],
        out_specs=pl.BlockSpec((None, tq, 2 * HEAD_DIM), lambda b, k, i: (b, i, k)),
        out_shape=jax.ShapeDtypeStruct((B, S, GROUP_W), BF16),
        scratch_shapes=[
            pltpu.VMEM((2 * tq, HEAD_DIM), F32),
            pltpu.VMEM((2 * tq, 2 * HEAD_DIM), F32),
            pltpu.VMEM((2, 2 * tq, tk), BF16),
            pltpu.VMEM((2, 2 * tq, HEAD_DIM), F32),
        ],
        compiler_params=_params(("parallel", "parallel", "arbitrary")),
        name="flash_gqa",
    )(q_r, kt, va, kn, qn)
    return out.reshape(T, GROUP_W)


def _combine_kernel(x_ref, ya_ref, cb_ref, cc_ref, cx_ref, ccp_ref, cxp_ref, ccn_ref, cxn_ref, mo_ref,
                    hf_ref, hb_ref, yd_ref, cw_ref, cbias_ref, gain_ref, w_ref, o_ref, y_ref, *, tiles_per_seq):
    tm = x_ref.shape[0]
    i = pl.program_id(0)
    halo = ccp_ref.shape[0]
    at_start = (i % tiles_per_seq) == 0
    at_end = (i % tiles_per_seq) == tiles_per_seq - 1
    gain = gain_ref[...]

    z = cc_ref[...].astype(F32) * cx_ref[...].astype(F32)
    z_before = ccp_ref[halo - 1:halo, :].astype(F32) * cxp_ref[halo - 1:halo, :].astype(F32)
    z_after = ccn_ref[0:1, :].astype(F32) * cxn_ref[0:1, :].astype(F32)
    z_before = jnp.where(at_start, 0.0, z_before)
    z_after = jnp.where(at_end, 0.0, z_after)
    rid = lax.broadcasted_iota(jnp.int32, z.shape, 0)
    z_m1 = jnp.where(rid == 0, z_before, pltpu.roll(z, 1, axis=0))
    z_p1 = jnp.where(rid == tm - 1, z_after, pltpu.roll(z, tm - 1, axis=0))
    conv = z_m1 * cw_ref[0:1, :] + z * cw_ref[1:2, :] + z_p1 * cw_ref[2:3, :] + cbias_ref[...]
    yb = cb_ref[...].astype(F32) * conv

    for grp in range(GROUP_W // HEAD_DIM):
        lanes = slice(grp * HEAD_DIM, (grp + 1) * HEAD_DIM)
        y_ref[:, lanes] = (ya_ref[:, lanes].astype(F32) * gain[:, lanes]).astype(BF16)
        gl = slice(GROUP_W + grp * HEAD_DIM, GROUP_W + (grp + 1) * HEAD_DIM)
        y_ref[:, gl] = (_head_norm(yb[:, lanes]) * gain[:, gl]).astype(BF16)
        gl = slice(2 * GROUP_W + grp * HEAD_DIM, 2 * GROUP_W + (grp + 1) * HEAD_DIM)
        yc = _head_norm(hf_ref[:, lanes].astype(F32) + hb_ref[:, lanes].astype(F32))
        y_ref[:, gl] = (jax.nn.sigmoid(mo_ref[:, lanes].astype(F32)) * yc * gain[:, gl]).astype(BF16)
        gl = slice(3 * GROUP_W + grp * HEAD_DIM, 3 * GROUP_W + (grp + 1) * HEAD_DIM)
        y_ref[:, gl] = (_head_norm(yd_ref[:, lanes].astype(F32)) * gain[:, gl]).astype(BF16)

    o_ref[...] = x_ref[...] + jnp.dot(y_ref[...], w_ref[...], preferred_element_type=F32)


def _combine_out(x2, ya, u3, hf, hb, yd, conv_w, conv_b, out_gain, w_out, S, tm=512, halo=V7X_BF16_SUBLANES):
    T = x2.shape[0]
    hb_per_tile = tm // halo
    n_halo = T // halo

    def u_blk(blk):
        return pl.BlockSpec((None, tm, GROUP_W), lambda i: (blk, i, 0))

    def u_prev(blk):
        return pl.BlockSpec((None, halo, GROUP_W), lambda i: (blk, jnp.maximum(i * hb_per_tile - 1, 0), 0))

    def u_next(blk):
        return pl.BlockSpec((None, halo, GROUP_W), lambda i: (blk, jnp.minimum((i + 1) * hb_per_tile, n_halo - 1), 0))

    tok = pl.BlockSpec((tm, GROUP_W), lambda i: (i, 0))
    return pl.pallas_call(
        functools.partial(_combine_kernel, tiles_per_seq=S // tm),
        grid=(T // tm,),
        in_specs=[
            pl.BlockSpec((tm, D_MODEL), lambda i: (i, 0)),
            tok,
            u_blk(BLK_CB), u_blk(BLK_CC), u_blk(BLK_CX),
            u_prev(BLK_CC), u_prev(BLK_CX), u_next(BLK_CC), u_next(BLK_CX),
            u_blk(BLK_MO),
            tok, tok, tok,
            pl.BlockSpec((V7X_F32_SUBLANES, GROUP_W), lambda i: (0, 0)),
            pl.BlockSpec((1, GROUP_W), lambda i: (0, 0)),
            pl.BlockSpec((1, D_MODEL), lambda i: (0, 0)),
            pl.BlockSpec((D_MODEL, D_MODEL), lambda i: (0, 0)),
        ],
        out_specs=pl.BlockSpec((tm, D_MODEL), lambda i: (i, 0)),
        out_shape=jax.ShapeDtypeStruct((T, D_MODEL), F32),
        scratch_shapes=[pltpu.VMEM((tm, D_MODEL), BF16)],
        compiler_params=_params(("parallel",)),
        name="combine_out",
    )(x2, ya, u3, u3, u3, u3, u3, u3, u3, u3, hf, hb, yd, conv_w, conv_b, out_gain, w_out)


def _mem_kv_kernel(mem_ref, g_ref, w_ref, kg_ref, kt_ref, v_ref, h_ref):
    _norm_rows_to(mem_ref, g_ref, h_ref)
    kv = jnp.dot(h_ref[...], w_ref[...], preferred_element_type=F32)
    kg = kg_ref[...]
    for h in range(CA_HEADS):
        lanes = slice(h * HEAD_DIM, (h + 1) * HEAD_DIM)
        kt_ref[h] = (_head_norm(kv[:, lanes]) * kg).T.astype(BF16)
    v_ref[...] = kv[:, CA_HEADS * HEAD_DIM:].astype(BF16)


def _mem_kv(mem, g_mem, w_kv, k_gain):
    B = mem.shape[0]
    return pl.pallas_call(
        _mem_kv_kernel,
        grid=(B,),
        in_specs=[
            pl.BlockSpec((None, N_MEM, D_MODEL), lambda b: (b, 0, 0)),
            pl.BlockSpec((1, D_MODEL), lambda b: (0, 0)),
            pl.BlockSpec((D_MODEL, 2 * GROUP_W), lambda b: (0, 0)),
            pl.BlockSpec((1, HEAD_DIM), lambda b: (0, 0)),
        ],
        out_specs=[
            pl.BlockSpec((None, CA_HEADS, HEAD_DIM, N_MEM), lambda b: (b, 0, 0, 0)),
            pl.BlockSpec((None, N_MEM, GROUP_W), lambda b: (b, 0, 0)),
        ],
        out_shape=[
            jax.ShapeDtypeStruct((B, CA_HEADS, HEAD_DIM, N_MEM), BF16),
            jax.ShapeDtypeStruct((B, N_MEM, GROUP_W), BF16),
        ],
        scratch_shapes=[pltpu.VMEM((N_MEM, D_MODEL), BF16)],
        compiler_params=_params(("parallel",)),
        name="mem_kv",
    )(mem, g_mem, w_kv, k_gain)


def _cross_attn_kernel(x_ref, g_ref, wq_ref, qg_ref, kt_ref, v_ref, wo_ref, o_ref, h_ref, oc_ref):
    _norm_rows_to(x_ref, g_ref, h_ref)
    q = jnp.dot(h_ref[...], wq_ref[...], preferred_element_type=F32)
    qg = qg_ref[...]
    for h in range(CA_HEADS):
        lanes = slice(h * HEAD_DIM, (h + 1) * HEAD_DIM)
        qh = (_head_norm(q[:, lanes]) * (qg * QK_SCALE)).astype(BF16)
        s = jnp.dot(qh, kt_ref[h], preferred_element_type=F32)
        e = jnp.exp(s - jnp.max(s, axis=-1, keepdims=True))
        o = jnp.dot(e.astype(BF16), v_ref[:, lanes], preferred_element_type=F32)
        oc_ref[:, lanes] = (o * pl.reciprocal(jnp.sum(e, axis=-1, keepdims=True), approx=False)).astype(BF16)
    o_ref[...] = x_ref[...] + jnp.dot(oc_ref[...], wo_ref[...], preferred_element_type=F32)


def _cross_attn(x2, g_ca, w_q, q_gain, kt, v, w_o, B, S, tm=1024):
    T = x2.shape[0]
    x3 = x2.reshape(B, S, D_MODEL)
    out = pl.pallas_call(
        _cross_attn_kernel,
        grid=(B, S // tm),
        in_specs=[
            pl.BlockSpec((None, tm, D_MODEL), lambda b, i: (b, i, 0)),
            pl.BlockSpec((1, D_MODEL), lambda b, i: (0, 0)),
            pl.BlockSpec((D_MODEL, GROUP_W), lambda b, i: (0, 0)),
            pl.BlockSpec((1, HEAD_DIM), lambda b, i: (0, 0)),
            pl.BlockSpec((None, CA_HEADS, HEAD_DIM, N_MEM), lambda b, i: (b, 0, 0, 0)),
            pl.BlockSpec((None, N_MEM, GROUP_W), lambda b, i: (b, 0, 0)),
            pl.BlockSpec((GROUP_W, D_MODEL), lambda b, i: (0, 0)),
        ],
        out_specs=pl.BlockSpec((None, tm, D_MODEL), lambda b, i: (b, i, 0)),
        out_shape=jax.ShapeDtypeStruct((B, S, D_MODEL), F32),
        scratch_shapes=[pltpu.VMEM((tm, D_MODEL), BF16), pltpu.VMEM((tm, GROUP_W), BF16)],
        compiler_params=_params(("parallel", "parallel")),
        name="cross_attn",
    )(x3, g_ca, w_q, q_gain, kt, v, w_o)
    return out.reshape(T, D_MODEL)


def _mlp_kernel(x_ref, g_ref, wu_ref, wd_ref, o_ref, h_ref):
    @pl.when(pl.program_id(1) == 0)
    def _():
        _norm_rows_to(x_ref, g_ref, h_ref)
        o_ref[...] = x_ref[...]

    up = jnp.dot(h_ref[...], wu_ref[...], preferred_element_type=F32)
    act = jnp.square(jnp.maximum(up, 0.0)).astype(BF16)
    o_ref[...] += jnp.dot(act, wd_ref[...], preferred_element_type=F32)


def _mlp(x2, g, w_up, w_down, tm=512, tf=1024):
    T = x2.shape[0]
    return pl.pallas_call(
        _mlp_kernel,
        grid=(T // tm, D_FF // tf),
        in_specs=[
            pl.BlockSpec((tm, D_MODEL), lambda i, j: (i, 0)),
            pl.BlockSpec((1, D_MODEL), lambda i, j: (0, 0)),
            pl.BlockSpec((D_MODEL, tf), lambda i, j: (0, j)),
            pl.BlockSpec((tf, D_MODEL), lambda i, j: (j, 0)),
        ],
        out_specs=pl.BlockSpec((tm, D_MODEL), lambda i, j: (i, 0)),
        out_shape=jax.ShapeDtypeStruct((T, D_MODEL), F32),
        scratch_shapes=[pltpu.VMEM((tm, D_MODEL), BF16)],
        compiler_params=_params(("parallel", "arbitrary")),
        name="mlp",
    )(x2, g, w_up, w_down)


def _prep_weights(w_in, conv_w, i_bias, f_bias, w_out, w_ca_q, w_ca_kv, w_ca_o, w_up, w_down):
    n_main = 8 * GROUP_W
    w_main = jnp.concatenate([w_in[:, :, :n_main], w_in[:, :, n_main + 16:]], axis=2).astype(BF16)
    pad = jnp.zeros((DEPTH, D_MODEL, HEAD_DIM - 8), F32)
    w_gate = jnp.concatenate([w_in[:, :, n_main:n_main + 8], pad, w_in[:, :, n_main + 8:n_main + 16], pad],
                             axis=2).astype(BF16)
    lane_pad = jnp.zeros((DEPTH, 1, HEAD_DIM - 8), F32)
    ib = jnp.concatenate([i_bias.reshape(DEPTH, 1, 8), lane_pad], axis=2)
    fb = jnp.concatenate([f_bias.reshape(DEPTH, 1, 8), lane_pad], axis=2)
    cw = jnp.concatenate([conv_w, jnp.zeros((DEPTH, V7X_F32_SUBLANES - conv_w.shape[1], GROUP_W), F32)], axis=1)
    return dict(w_main=w_main, w_gate=w_gate, ib=ib, fb=fb, cw=cw, w_out=w_out.astype(BF16),
                w_ca_q=w_ca_q.astype(BF16), w_ca_kv=w_ca_kv.astype(BF16), w_ca_o=w_ca_o.astype(BF16),
                w_up=w_up.astype(BF16), w_down=w_down.astype(BF16))


def _trunk(x, mem, pw, g_mix, conv_b, attn_q_norm, attn_k_norm, out_gain, g_ca, g_mem, ca_q_norm, ca_k_norm, g_mlp):
    B, S, _ = x.shape
    T = B * S
    x2 = x.reshape(T, D_MODEL)
    rope = _rope_tables(S)
    ftab = _fourier_tables(S)
    for l in range(DEPTH):
        u3, gates = _norm_proj(x2, g_mix[l][None], pw["w_main"][l], pw["w_gate"][l])
        ya = _fourier(u3, B, S, ftab)
        hf, hb = _mlstm(u3, gates, pw["ib"][l], pw["fb"][l], B, S)
        yd = _axial_gqa(u3, attn_q_norm[l][None], attn_k_norm[l][None], rope, B, S)
        x2 = _combine_out(x2, ya, u3, hf, hb, yd, pw["cw"][l], conv_b[l][None], out_gain[l][None], pw["w_out"][l], S)
        kt, v = _mem_kv(mem, g_mem[l][None], pw["w_ca_kv"][l], ca_k_norm[l][None])
        x2 = _cross_attn(x2, g_ca[l][None], pw["w_ca_q"][l], ca_q_norm[l][None], kt, v, pw["w_ca_o"][l], B, S)
        x2 = _mlp(x2, g_mlp[l][None], pw["w_up"][l], pw["w_down"][l])
    return x2.reshape(B, S, D_MODEL)


def kernel(x_prompt, x_sample, mem_prompt, mem_sample, g_mix, w_in, conv_w, conv_b, i_bias, f_bias, attn_q_norm,
           attn_k_norm, out_gain, w_out, g_ca, g_mem, w_ca_q, w_ca_kv, ca_q_norm, ca_k_norm, w_ca_o, g_mlp, w_up,
           w_down):
    pw = _prep_weights(w_in, conv_w, i_bias, f_bias, w_out, w_ca_q, w_ca_kv, w_ca_o, w_up, w_down)
    args = (pw, g_mix, conv_b, attn_q_norm, attn_k_norm, out_gain, g_ca, g_mem, ca_q_norm, ca_k_norm, g_mlp)
    y_prompt = _trunk(x_prompt, mem_prompt, *args)
    y_sample = _trunk(x_sample, mem_sample, *args)
    return (y_prompt, y_sample)
```

```python
import functools
import math

import jax
import jax.numpy as jnp
from jax import lax
from jax.experimental import pallas as pl
from jax.experimental.pallas import tpu as pltpu

F32 = jnp.float32
BF16 = jnp.bfloat16

D_MODEL = 2048
DEPTH = 4
HEAD_DIM = 128
GROUP_W = 4 * HEAD_DIM
N_COLBLK = 10
GATE_W = 2 * HEAD_DIM
MLSTM_HEADS = 4
MLSTM_CHUNK = 128
ATTN_KV_HEADS = 2
GRID_W = 64
ROPE_THETA = 10000.0
ROPE_FREQS = HEAD_DIM // 4
CA_HEADS = 4
N_MEM = 256
D_FF = 4 * D_MODEL
RMS_EPS = 1e-6
QK_SCALE = HEAD_DIM ** -0.5
LOG2_E = math.log2(math.e)
FRAME_MARGIN = 64.0

BLK_FOURIER, BLK_CB, BLK_CC, BLK_CX, BLK_MQ, BLK_MK, BLK_MV, BLK_MO, BLK_AQ, BLK_AKV = range(N_COLBLK)

V7X_VMEM_LIMIT = 56 * 1024 * 1024
NORM_ROWS = 256


def _params(semantics, vmem=V7X_VMEM_LIMIT):
    return pltpu.CompilerParams(dimension_semantics=semantics, vmem_limit_bytes=vmem)


def _head_norm(y):
    return y * lax.rsqrt(jnp.mean(y * y, axis=-1, keepdims=True) + RMS_EPS)


def _norm_rows_to(x_ref, g_ref, h_ref):
    tm = x_ref.shape[0]
    g = g_ref[...]

    def body(r, carry):
        rows = pl.ds(pl.multiple_of(r * NORM_ROWS, NORM_ROWS), NORM_ROWS)
        xf = x_ref[rows, :]
        ms = jnp.mean(xf * xf, axis=-1, keepdims=True)
        h_ref[rows, :] = (xf * lax.rsqrt(ms + RMS_EPS) * g).astype(BF16)
        return carry

    lax.fori_loop(0, tm // NORM_ROWS, body, 0)


def _norm_proj_kernel(x_ref, g_ref, w_ref, wg_ref, u_ref, gate_ref, h_ref):
    @pl.when(pl.program_id(1) == 0)
    def _():
        _norm_rows_to(x_ref, g_ref, h_ref)
        gate_ref[...] = jnp.dot(h_ref[...], wg_ref[...], preferred_element_type=F32)

    res = jnp.dot(h_ref[...], w_ref[...], preferred_element_type=F32)
    for k in range(u_ref.shape[0]):
        u_ref[k] = res[:, k * GROUP_W:(k + 1) * GROUP_W].astype(BF16)


def _norm_proj(x2, g, w_main, w_gate, tm=1024, nb=2):
    T = x2.shape[0]
    tn = nb * GROUP_W
    return pl.pallas_call(
        _norm_proj_kernel,
        grid=(T // tm, N_COLBLK // nb),
        in_specs=[
            pl.BlockSpec((tm, D_MODEL), lambda i, j: (i, 0)),
            pl.BlockSpec((1, D_MODEL), lambda i, j: (0, 0)),
            pl.BlockSpec((D_MODEL, tn), lambda i, j: (0, j)),
            pl.BlockSpec((D_MODEL, GATE_W), lambda i, j: (0, 0)),
        ],
        out_specs=[
            pl.BlockSpec((nb, tm, GROUP_W), lambda i, j: (j, i, 0)),
            pl.BlockSpec((tm, GATE_W), lambda i, j: (i, 0)),
        ],
        out_shape=[
            jax.ShapeDtypeStruct((N_COLBLK, T, GROUP_W), BF16),
            jax.ShapeDtypeStruct((T, GATE_W), F32),
        ],
        scratch_shapes=[pltpu.VMEM((tm, D_MODEL), BF16)],
        compiler_params=_params(("parallel", "arbitrary")),
        name="norm_proj",
    )(x2, g, w_main, w_gate)


def _fourier_tables(S):
    n2 = HEAD_DIM
    n1 = S // n2
    i1 = jnp.arange(n1, dtype=jnp.int32)
    a1 = (2.0 * math.pi / n1) * ((i1[:, None] * i1[None, :]) % n1).astype(F32)
    f1 = jnp.concatenate([jnp.cos(a1), -jnp.sin(a1)], axis=0).astype(BF16)
    i2 = jnp.arange(n2, dtype=jnp.int32)
    p = i1[:, None, None] + n1 * i2[None, :, None]
    k = (p * i2[None, None, :]) % S
    a2 = (2.0 * math.pi / S) * k.astype(F32)
    c2, s2 = jnp.cos(a2), jnp.sin(a2)
    g = jnp.concatenate([jnp.concatenate([c2, s2], axis=2),
                         jnp.concatenate([-s2, c2], axis=2)], axis=1).astype(BF16)
    ac = (2.0 * math.pi / HEAD_DIM) * ((i2[:, None] * i2[None, :]) % HEAD_DIM).astype(F32)
    cs = jnp.concatenate([jnp.cos(ac), jnp.sin(ac)], axis=0).astype(BF16)
    return f1, g, cs


def _fourier_stage1_kernel(a_ref, f1_ref, y_ref):
    n1 = a_ref.shape[0]
    res = jnp.dot(f1_ref[...], a_ref[...], preferred_element_type=F32)
    y_ref[0] = res[:n1].astype(BF16)
    y_ref[1] = res[n1:].astype(BF16)


def _fourier_stage2_kernel(y_ref, g_ref, cs_ref, o_ref, *, scale):
    n_p = y_ref.shape[1]
    cs = cs_ref[...]
    for p in range(n_p):
        ycat = jnp.concatenate([y_ref[0, p], y_ref[1, p]], axis=0)
        x = jnp.dot(g_ref[p], ycat, preferred_element_type=F32).astype(BF16)
        for grp in range(4):
            lanes = slice(grp * HEAD_DIM, (grp + 1) * HEAD_DIM)
            xg = jnp.concatenate([x[:HEAD_DIM, lanes], x[HEAD_DIM:, lanes]], axis=1)
            o = jnp.dot(xg, cs, preferred_element_type=F32) * scale
            col = p * GROUP_W + grp * HEAD_DIM
            o_ref[:, col:col + HEAD_DIM] = _head_norm(o).astype(BF16)


def _fourier(u3, B, S, tables):
    f1, g, cs = tables
    n2 = HEAD_DIM
    n1 = S // n2
    T = B * S
    ncol = min(16, n2)
    a_view = u3[BLK_FOURIER].reshape(B, n1, n2 * GROUP_W)
    y = pl.pallas_call(
        _fourier_stage1_kernel,
        grid=(B, n2 // ncol),
        in_specs=[
            pl.BlockSpec((None, n1, ncol * GROUP_W), lambda b, j: (b, 0, j)),
            pl.BlockSpec((2 * n1, n1), lambda b, j: (0, 0)),
        ],
        out_specs=pl.BlockSpec((None, 2, n1, ncol * GROUP_W), lambda b, j: (b, 0, 0, j)),
        out_shape=jax.ShapeDtypeStruct((B, 2, n1, n2 * GROUP_W), BF16),
        compiler_params=_params(("parallel", "parallel")),
        name="fourier_stage1",
    )(a_view, f1)
    y5 = y.reshape(B, 2, n1, n2, GROUP_W)
    n_p = 8
    out = pl.pallas_call(
        functools.partial(_fourier_stage2_kernel, scale=1.0 / math.sqrt(S * HEAD_DIM)),
        grid=(B, n1 // n_p),
        in_specs=[
            pl.BlockSpec((None, 2, n_p, n2, GROUP_W), lambda b, j: (b, 0, j, 0, 0)),
            pl.BlockSpec((n_p, 2 * n2, 2 * n2), lambda b, j: (j, 0, 0)),
            pl.BlockSpec((2 * HEAD_DIM, HEAD_DIM), lambda b, j: (0, 0)),
        ],
        out_specs=pl.BlockSpec((None, n2, n_p * GROUP_W), lambda b, j: (b, 0, j)),
        out_shape=jax.ShapeDtypeStruct((B, n2, n1 * GROUP_W), BF16),
        compiler_params=_params(("parallel", "parallel")),
        name="fourier_stage2",
    )(y5, g, cs)
    return out.reshape(T, GROUP_W)


def _log_sigmoid(x):
    return jnp.minimum(x, 0.0) - jnp.log1p(jnp.exp(-jnp.abs(x)))


def _split3(x):
    hi = x.astype(BF16)
    r1 = x - hi.astype(F32)
    mid = r1.astype(BF16)
    return hi, mid, (r1 - mid.astype(F32)).astype(BF16)


def _mlstm_chunk(d, q, k, v, gi, gf, ib, fb, tri, mask, c_ref, n_ref, m_ref, out_ref, rows):
    li = gi + ib
    lf = _log_sigmoid(gf + fb)
    b = sum(jnp.dot(tri, part, preferred_element_type=F32) for part in _split3(lf))
    g = jnp.sum(lf, axis=0, keepdims=True)
    w_end = g - b + li
    m_loc = jnp.max(w_end, axis=0, keepdims=True)
    e_t = jnp.exp(w_end - m_loc).T * QK_SCALE
    e_parts = _split3(e_t[:16, :])
    a_t = (li - b).T
    m_old = m_ref[d:d + 1, :]
    m_new = jnp.maximum(g + m_old, m_loc)
    a_sc = jnp.exp(g + m_old - m_new)
    b_sc = jnp.exp(m_loc - m_new)
    lane = lax.broadcasted_iota(jnp.int32, (1, HEAD_DIM), 1)

    def pick(x, hp):
        return jnp.sum(jnp.where(lane == hp, x, 0.0), axis=-1, keepdims=True)

    for h in range(MLSTM_HEADS):
        hp = MLSTM_HEADS * d + h
        lanes = slice(h * HEAD_DIM, (h + 1) * HEAD_DIM)
        qh = q[:, lanes]
        vh = v[:, lanes]
        kh = k[:, lanes]
        bc = pick(b, hp)
        dm = jnp.where(mask, bc + a_t[hp:hp + 1, :], -jnp.inf)
        inter_log = bc + pick(m_old, hp)
        m_t = jnp.maximum(inter_log, jnp.max(dm, axis=-1, keepdims=True))
        p = jnp.exp(dm - m_t)
        inter_w = jnp.exp(inter_log - m_t)
        qk = lax.dot_general(qh, kh, (((1,), (1,)), ((), ())), preferred_element_type=F32)
        a_mat = p * (qk * QK_SCALE)
        c_prev = c_ref[hp]
        n_prev = n_ref[hp:hp + 1, :]
        num = (jnp.dot(a_mat.astype(BF16), vh, preferred_element_type=F32)
               + inter_w * jnp.dot(qh, c_prev.astype(BF16), preferred_element_type=F32))
        den = (jnp.sum(a_mat, axis=-1, keepdims=True)
               + inter_w * jnp.sum(qh.astype(F32) * n_prev, axis=-1, keepdims=True))
        hout = num / jnp.maximum(jnp.abs(den), jnp.exp(-m_t))
        out_ref[rows, lanes] = hout.astype(out_ref.dtype)
        ke_t = kh.astype(F32).T * e_t[hp:hp + 1, :]
        c_loc = jnp.dot(ke_t.astype(BF16), vh, preferred_element_type=F32)
        n_loc = sum(jnp.dot(part, kh, preferred_element_type=F32) for part in e_parts)[hp:hp + 1, :]
        ah = pick(a_sc, hp)
        bh = pick(b_sc, hp)
        c_ref[hp] = ah * c_prev + bh * c_loc
        n_ref[hp:hp + 1, :] = ah * n_prev + bh * n_loc
    m_ref[d:d + 1, :] = m_new


def _mlstm_kernel(qf_ref, kf_ref, vf_ref, gf_ref, qb_ref, kb_ref, vb_ref, gb_ref, ib_ref, fb_ref,
                  hf_ref, hb_ref, c_ref, n_ref, m_ref):
    L = MLSTM_CHUNK
    n_sub = qf_ref.shape[0] // L

    @pl.when(pl.program_id(1) == 0)
    def _():
        c_ref[...] = jnp.zeros_like(c_ref)
        n_ref[...] = jnp.zeros_like(n_ref)
        m_ref[...] = jnp.zeros_like(m_ref)

    row = lax.broadcasted_iota(jnp.int32, (L, L), 0)
    col = lax.broadcasted_iota(jnp.int32, (L, L), 1)
    mask_f = col <= row
    mask_b = col >= row
    tri_f = jnp.where(mask_f, 1.0, 0.0).astype(BF16)
    tri_b = jnp.where(mask_b, 1.0, 0.0).astype(BF16)
    ib = ib_ref[...]
    fb = fb_ref[...]

    def body(j, carry):
        rf = pl.ds(pl.multiple_of(j * L, L), L)
        rb = pl.ds(pl.multiple_of((n_sub - 1 - j) * L, L), L)
        _mlstm_chunk(0, qf_ref[rf, :], kf_ref[rf, :], vf_ref[rf, :], gf_ref[rf, :HEAD_DIM], gf_ref[rf, HEAD_DIM:],
                     ib, fb, tri_f, mask_f, c_ref, n_ref, m_ref, hf_ref, rf)
        _mlstm_chunk(1, qb_ref[rb, :], kb_ref[rb, :], vb_ref[rb, :], gb_ref[rb, :HEAD_DIM], gb_ref[rb, HEAD_DIM:],
                     ib, fb, tri_b, mask_b, c_ref, n_ref, m_ref, hb_ref, rb)
        return carry

    lax.fori_loop(0, n_sub, body, 0)


def _mlstm(u3, gates, ib, fb, B, S, n_sub=4):
    T = B * S
    rows = n_sub * MLSTM_CHUNK
    nblk = S // rows
    u4 = u3.reshape(N_COLBLK, B, S, GROUP_W)
    g3 = gates.reshape(B, S, GATE_W)

    def fwd(blk):
        return pl.BlockSpec((None, None, rows, GROUP_W), lambda b, c: (blk, b, c, 0))

    def bwd(blk):
        return pl.BlockSpec((None, None, rows, GROUP_W), lambda b, c: (blk, b, nblk - 1 - c, 0))

    hf, hb = pl.pallas_call(
        _mlstm_kernel,
        grid=(B, nblk),
        in_specs=[
            fwd(BLK_MQ), fwd(BLK_MK), fwd(BLK_MV),
            pl.BlockSpec((None, rows, GATE_W), lambda b, c: (b, c, 0)),
            bwd(BLK_MQ), bwd(BLK_MK), bwd(BLK_MV),
            pl.BlockSpec((None, rows, GATE_W), lambda b, c: (b, nblk - 1 - c, 0)),
            pl.BlockSpec((1, HEAD_DIM), lambda b, c: (0, 0)),
            pl.BlockSpec((1, HEAD_DIM), lambda b, c: (0, 0)),
        ],
        out_specs=[
            pl.BlockSpec((None, rows, GROUP_W), lambda b, c: (b, c, 0)),
            pl.BlockSpec((None, rows, GROUP_W), lambda b, c: (b, nblk - 1 - c, 0)),
        ],
        out_shape=[jax.ShapeDtypeStruct((B, S, GROUP_W), BF16)] * 2,
        scratch_shapes=[
            pltpu.VMEM((2 * MLSTM_HEADS, HEAD_DIM, HEAD_DIM), F32),
            pltpu.VMEM((2 * MLSTM_HEADS, HEAD_DIM), F32),
            pltpu.VMEM((8, HEAD_DIM), F32),
        ],
        compiler_params=_params(("parallel", "arbitrary")),
        name="mlstm",
    )(u4, u4, u4, g3, u4, u4, u4, g3, ib, fb)
    return hf.reshape(T, GROUP_W), hb.reshape(T, GROUP_W)


def _rope_tables(S):
    rows = S // GRID_W
    row = jnp.repeat(jnp.arange(rows), GRID_W).astype(F32)
    colp = jnp.tile(jnp.arange(GRID_W), rows).astype(F32)
    inv = ROPE_THETA ** (-jnp.arange(ROPE_FREQS, dtype=F32) / ROPE_FREQS)
    ar = row[:, None] * inv
    ac = colp[:, None] * inv
    ang = jnp.concatenate([ar, ar, ac, ac], axis=1)
    first_half = (jnp.arange(HEAD_DIM) % (2 * ROPE_FREQS)) < ROPE_FREQS
    return jnp.cos(ang), jnp.where(first_half, -1.0, 1.0) * jnp.sin(ang)


def _rope(x, cos, sin_signed, first_half):
    partner = jnp.where(first_half, pltpu.roll(x, HEAD_DIM - ROPE_FREQS, axis=1), pltpu.roll(x, ROPE_FREQS, axis=1))
    return x * cos + partner * sin_signed


def _attn_prep_kernel(aq_ref, akv_ref, cos_ref, sin_ref, qg_ref, kg_ref, q_ref, kt_ref, kn_ref, qn_ref):
    cos = cos_ref[...]
    sin = sin_ref[...]
    lane = lax.broadcasted_iota(jnp.int32, cos.shape, 1)
    first_half = (lane % (2 * ROPE_FREQS)) < ROPE_FREQS
    qg = qg_ref[...]
    kg = kg_ref[...]
    for h in range(4):
        lanes = slice(h * HEAD_DIM, (h + 1) * HEAD_DIM)
        qh = _head_norm(aq_ref[:, lanes].astype(F32)) * qg
        qb = (_rope(qh, cos, sin, first_half) * (QK_SCALE * LOG2_E)).astype(BF16)
        q_ref[:, lanes] = qb
        q2 = jnp.max(jnp.sum(jnp.square(qb.astype(F32)), axis=-1, keepdims=True), axis=0, keepdims=True)
        qn_ref[h] = jnp.broadcast_to(q2, qn_ref.shape[1:])
    for h in range(ATTN_KV_HEADS):
        lanes = slice(h * HEAD_DIM, (h + 1) * HEAD_DIM)
        kh = _head_norm(akv_ref[:, lanes].astype(F32)) * kg
        kb = _rope(kh, cos, sin, first_half).astype(BF16)
        kt_ref[h] = kb.astype(F32).T.astype(BF16)
        k2 = jnp.max(jnp.sum(jnp.square(kb.astype(F32)), axis=-1, keepdims=True), axis=0, keepdims=True)
        kn_ref[h] = jnp.broadcast_to(k2, kn_ref.shape[1:])


def _flash_kernel(q_ref, kt_ref, v_ref, kn_ref, qn_ref, o_ref, m_ref, acc_ref, p_ref, mp_ref):
    tq = q_ref.shape[0]
    nkc, _, tk = kt_ref.shape
    n_rep = tk // HEAD_DIM
    q2 = jnp.concatenate([q_ref[:, :HEAD_DIM], q_ref[:, HEAD_DIM:]], axis=0)

    ones = jnp.ones((tk, HEAD_DIM), BF16)

    def chunk(c):
        v = v_ref[pl.ds(pl.multiple_of(c * tk, tk), tk), :]
        return kt_ref[c], jnp.concatenate([v, ones], axis=1)

    def lane_max(s):
        part = s[:, :HEAD_DIM]
        for t in range(1, n_rep):
            part = jnp.maximum(part, s[:, t * HEAD_DIM:(t + 1) * HEAD_DIM])
        return part

    q_sq = jnp.max(jnp.max(qn_ref[...], axis=0), axis=0, keepdims=True)
    k_sq = jnp.max(jnp.max(kn_ref[...], axis=0), axis=0, keepdims=True)
    safe = jnp.max(jnp.sqrt(q_sq * k_sq)) <= FRAME_MARGIN
    acc_ref[...] = jnp.zeros_like(acc_ref)

    def fast_scores(c, slot):
        kt, _ = chunk(c)
        s = jnp.dot(q2, kt, preferred_element_type=F32)
        p_ref[slot] = jnp.exp2(s - jnp.tile(m_ref[...], (1, n_rep))).astype(BF16)
        mp_ref[slot] = lane_max(s)

    def fast_accumulate(c, slot):
        _, vc = chunk(c)
        m_prev = m_ref[...]
        m_new = jnp.maximum(m_prev, jnp.max(mp_ref[slot], axis=-1, keepdims=True))
        alpha = jnp.exp2(m_prev - m_new)
        acc_ref[...] = (acc_ref[...] + jnp.dot(p_ref[slot], vc, preferred_element_type=F32)) * jnp.tile(alpha, (1, 2))
        m_ref[...] = m_new

    def fast():
        m_ref[...] = jnp.zeros_like(m_ref)

        group = 8 if nkc % 8 == 0 else 2

        def body(i, carry):
            for t in range(group):
                fast_scores(group * i + t, t % 2)
                fast_accumulate(group * i + t, t % 2)
            return carry

        lax.fori_loop(0, nkc // group, body, 0)

    def exact():
        m_ref[...] = jnp.full_like(m_ref, -jnp.inf)

        def body(c, carry):
            kt, vc = chunk(c)
            s = jnp.dot(q2, kt, preferred_element_type=F32)
            m_prev = m_ref[...]
            m_new = jnp.maximum(m_prev, jnp.max(lane_max(s), axis=-1, keepdims=True))
            p = jnp.exp2(s - jnp.tile(m_new, (1, n_rep))).astype(BF16)
            acc_ref[...] = (jnp.tile(jnp.exp2(m_prev - m_new), (1, 2)) * acc_ref[...]
                            + jnp.dot(p, vc, preferred_element_type=F32))
            m_ref[...] = m_new
            return carry

        lax.fori_loop(0, nkc, body, 0)

    lax.cond(safe, fast, exact)
    o = acc_ref[:, :HEAD_DIM] / acc_ref[:, HEAD_DIM:]
    o_ref[:, :HEAD_DIM] = o[:tq].astype(o_ref.dtype)
    o_ref[:, HEAD_DIM:] = o[tq:].astype(o_ref.dtype)


def _axial_gqa(u3, q_gain, k_gain, rope, B, S, tq=512, tk=1024):
    T = B * S
    cos, sin = rope
    u4 = u3.reshape(N_COLBLK, B, S, GROUP_W)
    tk = min(tk, S // 2)
    nkc = S // tk
    assert nkc >= 2 and nkc % 2 == 0, "flash kernel pipelines kv chunks in pairs"
    q_r, kt, kn, qn = pl.pallas_call(
        _attn_prep_kernel,
        grid=(B, nkc),
        in_specs=[
            pl.BlockSpec((None, None, tk, GROUP_W), lambda b, i: (BLK_AQ, b, i, 0)),
            pl.BlockSpec((None, None, tk, GROUP_W), lambda b, i: (BLK_AKV, b, i, 0)),
            pl.BlockSpec((tk, HEAD_DIM), lambda b, i: (i, 0)),
            pl.BlockSpec((tk, HEAD_DIM), lambda b, i: (i, 0)),
            pl.BlockSpec((1, HEAD_DIM), lambda b, i: (0, 0)),
            pl.BlockSpec((1, HEAD_DIM), lambda b, i: (0, 0)),
        ],
        out_specs=[
            pl.BlockSpec((None, tk, GROUP_W), lambda b, i: (b, i, 0)),
            pl.BlockSpec((None, ATTN_KV_HEADS, None, HEAD_DIM, tk), lambda b, i: (b, 0, i, 0, 0)),
            pl.BlockSpec((None, ATTN_KV_HEADS, None, 8, HEAD_DIM), lambda b, i: (b, 0, i, 0, 0)),
            pl.BlockSpec((None, 4, None, 8, HEAD_DIM), lambda b, i: (b, 0, i, 0, 0)),
        ],
        out_shape=[
            jax.ShapeDtypeStruct((B, S, GROUP_W), BF16),
            jax.ShapeDtypeStruct((B, ATTN_KV_HEADS, nkc, HEAD_DIM, tk), BF16),
            jax.ShapeDtypeStruct((B, ATTN_KV_HEADS, nkc, 8, HEAD_DIM), F32),
            jax.ShapeDtypeStruct((B, 4, nkc, 8, HEAD_DIM), F32),
        ],
        compiler_params=_params(("parallel", "parallel")),
        name="attn_prep",
    )(u4, u4, cos, sin, q_gain, k_gain)
    out = pl.pallas_call(
        _flash_kernel,
        grid=(B, ATTN_KV_HEADS, S // tq),
        in_specs=[
            pl.BlockSpec((None, tq, 2 * HEAD_DIM), lambda b, k, i: (b, i, k)),
            pl.BlockSpec((None, None, nkc, HEAD_DIM, tk), lambda b, k, i: (b, k, 0, 0, 0)),
            pl.BlockSpec((None, None, S, HEAD_DIM), lambda b, k, i: (BLK_AKV, b, 0, ATTN_KV_HEADS + k)),
            pl.BlockSpec((None, None, nkc, 8, HEAD_DIM), lambda b, k, i: (b, k, 0, 0, 0)),
            pl.BlockSpec((None, 2, None, 8, HEAD_DIM), lambda b, k, i: (b, k, (i * tq) // tk, 0, 0)),
        ],
        out_specs=pl.BlockSpec((None, tq, 2 * HEAD_DIM), lambda b, k, i: (b, i, k)),
        out_shape=jax.ShapeDtypeStruct((B, S, GROUP_W), BF16),
        scratch_shapes=[
            pltpu.VMEM((2 * tq, HEAD_DIM), F32),
            pltpu.VMEM((2 * tq, 2 * HEAD_DIM), F32),
            pltpu.VMEM((2, 2 * tq, tk), BF16),
            pltpu.VMEM((2, 2 * tq, HEAD_DIM), F32),
        ],
        compiler_params=_params(("parallel", "parallel", "arbitrary")),
        name="flash_gqa",
    )(q_r, kt, u4, kn, qn)
    return out.reshape(T, GROUP_W)


def _combine_kernel(x_ref, ya_ref, cb_ref, cc_ref, cx_ref, ccp_ref, cxp_ref, ccn_ref, cxn_ref, mo_ref,
                    hf_ref, hb_ref, yd_ref, cw_ref, cbias_ref, gain_ref, w_ref, o_ref, y_ref, *, tiles_per_seq):
    tm = x_ref.shape[0]
    i = pl.program_id(0)
    halo = ccp_ref.shape[0]
    at_start = (i % tiles_per_seq) == 0
    at_end = (i % tiles_per_seq) == tiles_per_seq - 1
    gain = gain_ref[...]

    z = cc_ref[...].astype(F32) * cx_ref[...].astype(F32)
    z_before = ccp_ref[halo - 1:halo, :].astype(F32) * cxp_ref[halo - 1:halo, :].astype(F32)
    z_after = ccn_ref[0:1, :].astype(F32) * cxn_ref[0:1, :].astype(F32)
    z_before = jnp.where(at_start, 0.0, z_before)
    z_after = jnp.where(at_end, 0.0, z_after)
    rid = lax.broadcasted_iota(jnp.int32, z.shape, 0)
    z_m1 = jnp.where(rid == 0, z_before, pltpu.roll(z, 1, axis=0))
    z_p1 = jnp.where(rid == tm - 1, z_after, pltpu.roll(z, tm - 1, axis=0))
    conv = z_m1 * cw_ref[0:1, :] + z * cw_ref[1:2, :] + z_p1 * cw_ref[2:3, :] + cbias_ref[...]
    yb = cb_ref[...].astype(F32) * conv

    for grp in range(4):
        lanes = slice(grp * HEAD_DIM, (grp + 1) * HEAD_DIM)
        y_ref[:, lanes] = (ya_ref[:, lanes].astype(F32) * gain[:, lanes]).astype(BF16)
        gl = slice(GROUP_W + grp * HEAD_DIM, GROUP_W + (grp + 1) * HEAD_DIM)
        y_ref[:, gl] = (_head_norm(yb[:, lanes]) * gain[:, gl]).astype(BF16)
        gl = slice(2 * GROUP_W + grp * HEAD_DIM, 2 * GROUP_W + (grp + 1) * HEAD_DIM)
        yc = _head_norm(hf_ref[:, lanes].astype(F32) + hb_ref[:, lanes].astype(F32))
        y_ref[:, gl] = (jax.nn.sigmoid(mo_ref[:, lanes].astype(F32)) * yc * gain[:, gl]).astype(BF16)
        gl = slice(3 * GROUP_W + grp * HEAD_DIM, 3 * GROUP_W + (grp + 1) * HEAD_DIM)
        y_ref[:, gl] = (_head_norm(yd_ref[:, lanes].astype(F32)) * gain[:, gl]).astype(BF16)

    o_ref[...] = x_ref[...] + jnp.dot(y_ref[...], w_ref[...], preferred_element_type=F32)


def _combine_out(x2, ya, u3, hf, hb, yd, conv_w, conv_b, out_gain, w_out, S, tm=512, halo=16):
    T = x2.shape[0]
    hb_per_tile = tm // halo
    n_halo = T // halo

    def u_blk(blk):
        return pl.BlockSpec((None, tm, GROUP_W), lambda i: (blk, i, 0))

    def u_prev(blk):
        return pl.BlockSpec((None, halo, GROUP_W), lambda i: (blk, jnp.maximum(i * hb_per_tile - 1, 0), 0))

    def u_next(blk):
        return pl.BlockSpec((None, halo, GROUP_W), lambda i: (blk, jnp.minimum((i + 1) * hb_per_tile, n_halo - 1), 0))

    tok = pl.BlockSpec((tm, GROUP_W), lambda i: (i, 0))
    return pl.pallas_call(
        functools.partial(_combine_kernel, tiles_per_seq=S // tm),
        grid=(T // tm,),
        in_specs=[
            pl.BlockSpec((tm, D_MODEL), lambda i: (i, 0)),
            tok,
            u_blk(BLK_CB), u_blk(BLK_CC), u_blk(BLK_CX),
            u_prev(BLK_CC), u_prev(BLK_CX), u_next(BLK_CC), u_next(BLK_CX),
            u_blk(BLK_MO),
            tok, tok, tok,
            pl.BlockSpec((8, GROUP_W), lambda i: (0, 0)),
            pl.BlockSpec((1, GROUP_W), lambda i: (0, 0)),
            pl.BlockSpec((1, D_MODEL), lambda i: (0, 0)),
            pl.BlockSpec((D_MODEL, D_MODEL), lambda i: (0, 0)),
        ],
        out_specs=pl.BlockSpec((tm, D_MODEL), lambda i: (i, 0)),
        out_shape=jax.ShapeDtypeStruct((T, D_MODEL), F32),
        scratch_shapes=[pltpu.VMEM((tm, D_MODEL), BF16)],
        compiler_params=_params(("parallel",)),
        name="combine_out",
    )(x2, ya, u3, u3, u3, u3, u3, u3, u3, u3, hf, hb, yd, conv_w, conv_b, out_gain, w_out)


def _mem_kv_kernel(mem_ref, g_ref, w_ref, kg_ref, kt_ref, v_ref, h_ref):
    _norm_rows_to(mem_ref, g_ref, h_ref)
    kv = jnp.dot(h_ref[...], w_ref[...], preferred_element_type=F32)
    kg = kg_ref[...]
    for h in range(CA_HEADS):
        lanes = slice(h * HEAD_DIM, (h + 1) * HEAD_DIM)
        kt_ref[h] = (_head_norm(kv[:, lanes]) * kg).T.astype(BF16)
    v_ref[...] = kv[:, CA_HEADS * HEAD_DIM:].astype(BF16)


def _mem_kv(mem, g_mem, w_kv, k_gain):
    B = mem.shape[0]
    return pl.pallas_call(
        _mem_kv_kernel,
        grid=(B,),
        in_specs=[
            pl.BlockSpec((None, N_MEM, D_MODEL), lambda b: (b, 0, 0)),
            pl.BlockSpec((1, D_MODEL), lambda b: (0, 0)),
            pl.BlockSpec((D_MODEL, 2 * GROUP_W), lambda b: (0, 0)),
            pl.BlockSpec((1, HEAD_DIM), lambda b: (0, 0)),
        ],
        out_specs=[
            pl.BlockSpec((None, CA_HEADS, HEAD_DIM, N_MEM), lambda b: (b, 0, 0, 0)),
            pl.BlockSpec((None, N_MEM, GROUP_W), lambda b: (b, 0, 0)),
        ],
        out_shape=[
            jax.ShapeDtypeStruct((B, CA_HEADS, HEAD_DIM, N_MEM), BF16),
            jax.ShapeDtypeStruct((B, N_MEM, GROUP_W), BF16),
        ],
        scratch_shapes=[pltpu.VMEM((N_MEM, D_MODEL), BF16)],
        compiler_params=_params(("parallel",)),
        name="mem_kv",
    )(mem, g_mem, w_kv, k_gain)


def _cross_attn_kernel(x_ref, g_ref, wq_ref, qg_ref, kt_ref, v_ref, wo_ref, o_ref, h_ref, oc_ref):
    _norm_rows_to(x_ref, g_ref, h_ref)
    q = jnp.dot(h_ref[...], wq_ref[...], preferred_element_type=F32)
    qg = qg_ref[...]
    for h in range(CA_HEADS):
        lanes = slice(h * HEAD_DIM, (h + 1) * HEAD_DIM)
        qh = (_head_norm(q[:, lanes]) * (qg * QK_SCALE)).astype(BF16)
        s = jnp.dot(qh, kt_ref[h], preferred_element_type=F32)
        e = jnp.exp(s - jnp.max(s, axis=-1, keepdims=True))
        o = jnp.dot(e.astype(BF16), v_ref[:, lanes], preferred_element_type=F32)
        oc_ref[:, lanes] = (o * pl.reciprocal(jnp.sum(e, axis=-1, keepdims=True), approx=False)).astype(BF16)
    o_ref[...] = x_ref[...] + jnp.dot(oc_ref[...], wo_ref[...], preferred_element_type=F32)


def _cross_attn(x2, g_ca, w_q, q_gain, kt, v, w_o, B, S, tm=1024):
    T = x2.shape[0]
    x3 = x2.reshape(B, S, D_MODEL)
    out = pl.pallas_call(
        _cross_attn_kernel,
        grid=(B, S // tm),
        in_specs=[
            pl.BlockSpec((None, tm, D_MODEL), lambda b, i: (b, i, 0)),
            pl.BlockSpec((1, D_MODEL), lambda b, i: (0, 0)),
            pl.BlockSpec((D_MODEL, GROUP_W), lambda b, i: (0, 0)),
            pl.BlockSpec((1, HEAD_DIM), lambda b, i: (0, 0)),
            pl.BlockSpec((None, CA_HEADS, HEAD_DIM, N_MEM), lambda b, i: (b, 0, 0, 0)),
            pl.BlockSpec((None, N_MEM, GROUP_W), lambda b, i: (b, 0, 0)),
            pl.BlockSpec((GROUP_W, D_MODEL), lambda b, i: (0, 0)),
        ],
        out_specs=pl.BlockSpec((None, tm, D_MODEL), lambda b, i: (b, i, 0)),
        out_shape=jax.ShapeDtypeStruct((B, S, D_MODEL), F32),
        scratch_shapes=[pltpu.VMEM((tm, D_MODEL), BF16), pltpu.VMEM((tm, GROUP_W), BF16)],
        compiler_params=_params(("parallel", "parallel")),
        name="cross_attn",
    )(x3, g_ca, w_q, q_gain, kt, v, w_o)
    return out.reshape(T, D_MODEL)


def _mlp_kernel(x_ref, g_ref, wu_ref, wd_ref, o_ref, h_ref):
    @pl.when(pl.program_id(1) == 0)
    def _():
        _norm_rows_to(x_ref, g_ref, h_ref)
        o_ref[...] = x_ref[...]

    up = jnp.dot(h_ref[...], wu_ref[...], preferred_element_type=F32)
    act = jnp.square(jnp.maximum(up, 0.0)).astype(BF16)
    o_ref[...] += jnp.dot(act, wd_ref[...], preferred_element_type=F32)


def _mlp(x2, g, w_up, w_down, tm=512, tf=1024):
    T = x2.shape[0]
    return pl.pallas_call(
        _mlp_kernel,
        grid=(T // tm, D_FF // tf),
        in_specs=[
            pl.BlockSpec((tm, D_MODEL), lambda i, j: (i, 0)),
            pl.BlockSpec((1, D_MODEL), lambda i, j: (0, 0)),
            pl.BlockSpec((D_MODEL, tf), lambda i, j: (0, j)),
            pl.BlockSpec((tf, D_MODEL), lambda i, j: (j, 0)),
        ],
        out_specs=pl.BlockSpec((tm, D_MODEL), lambda i, j: (i, 0)),
        out_shape=jax.ShapeDtypeStruct((T, D_MODEL), F32),
        scratch_shapes=[pltpu.VMEM((tm, D_MODEL), BF16)],
        compiler_params=_params(("parallel", "arbitrary")),
        name="mlp",
    )(x2, g, w_up, w_down)


def _prep_weights(w_in, conv_w, i_bias, f_bias, w_out, w_ca_q, w_ca_kv, w_ca_o, w_up, w_down):
    n_main = 8 * GROUP_W
    w_main = jnp.concatenate([w_in[:, :, :n_main], w_in[:, :, n_main + 16:]], axis=2).astype(BF16)
    pad = jnp.zeros((DEPTH, D_MODEL, HEAD_DIM - 8), F32)
    w_gate = jnp.concatenate([w_in[:, :, n_main:n_main + 8], pad, w_in[:, :, n_main + 8:n_main + 16], pad],
                             axis=2).astype(BF16)
    lane_pad = jnp.zeros((DEPTH, 1, HEAD_DIM - 8), F32)
    ib = jnp.concatenate([i_bias.reshape(DEPTH, 1, 8), lane_pad], axis=2)
    fb = jnp.concatenate([f_bias.reshape(DEPTH, 1, 8), lane_pad], axis=2)
    cw = jnp.concatenate([conv_w, jnp.zeros((DEPTH, 5, GROUP_W), F32)], axis=1)
    return dict(w_main=w_main, w_gate=w_gate, ib=ib, fb=fb, cw=cw, w_out=w_out.astype(BF16),
                w_ca_q=w_ca_q.astype(BF16), w_ca_kv=w_ca_kv.astype(BF16), w_ca_o=w_ca_o.astype(BF16),
                w_up=w_up.astype(BF16), w_down=w_down.astype(BF16))


def _trunk(x, mem, pw, g_mix, conv_b, attn_q_norm, attn_k_norm, out_gain, g_ca, g_mem, ca_q_norm, ca_k_norm, g_mlp):
    B, S, _ = x.shape
    T = B * S
    x2 = x.reshape(T, D_MODEL)
    rope = _rope_tables(S)
    ftab = _fourier_tables(S)
    for l in range(DEPTH):
        u3, gates = _norm_proj(x2, g_mix[l][None], pw["w_main"][l], pw["w_gate"][l])
        ya = _fourier(u3, B, S, ftab)
        hf, hb = _mlstm(u3, gates, pw["ib"][l], pw["fb"][l], B, S)
        yd = _axial_gqa(u3, attn_q_norm[l][None], attn_k_norm[l][None], rope, B, S)
        x2 = _combine_out(x2, ya, u3, hf, hb, yd, pw["cw"][l], conv_b[l][None], out_gain[l][None], pw["w_out"][l], S)
        kt, v = _mem_kv(mem, g_mem[l][None], pw["w_ca_kv"][l], ca_k_norm[l][None])
        x2 = _cross_attn(x2, g_ca[l][None], pw["w_ca_q"][l], ca_q_norm[l][None], kt, v, pw["w_ca_o"][l], B, S)
        x2 = _mlp(x2, g_mlp[l][None], pw["w_up"][l], pw["w_down"][l])
    return x2.reshape(B, S, D_MODEL)


def kernel(x_prompt, x_sample, mem_prompt, mem_sample, g_mix, w_in, conv_w, conv_b, i_bias, f_bias, attn_q_norm,
           attn_k_norm, out_gain, w_out, g_ca, g_mem, w_ca_q, w_ca_kv, ca_q_norm, ca_k_norm, w_ca_o, g_mlp, w_up,
           w_down):
    pw = _prep_weights(w_in, conv_w, i_bias, f_bias, w_out, w_ca_q, w_ca_kv, w_ca_o, w_up, w_down)
    args = (pw, g_mix, conv_b, attn_q_norm, attn_k_norm, out_gain, g_ca, g_mem, ca_q_norm, ca_k_norm, g_mlp)
    y_prompt = _trunk(x_prompt, mem_prompt, *args)
    y_sample = _trunk(x_sample, mem_sample, *args)
    return (y_prompt, y_sample)
```
